```python
import jax
import jax.numpy as jnp
from jax import lax
import numpy as np

D_MODEL = 1024
BATCH = 16
SEQ = 2048
DEPTH = 4

HEAD_DIM = 64
N_HEADS_SB = 4
N_HEADS_GM = 4
N_HEADS_RW = 4
N_HEADS_FX = 4
W_SB = N_HEADS_SB * HEAD_DIM
W_GM = N_HEADS_GM * HEAD_DIM
W_RW = N_HEADS_RW * HEAD_DIM
W_FX = N_HEADS_FX * HEAD_DIM
D_MIX = W_SB + W_GM + W_RW + W_FX
BLOCK_Q = 128
CHUNK = 128
RW_DECAY_LORA = 64
RW_AAA_LORA = 64
RW_GATE_LORA = 128
SB_COLS = 3 * W_SB
GM_COLS = 2 * W_GM
RW_COLS = 3 * W_RW + RW_DECAY_LORA + RW_AAA_LORA + RW_GATE_LORA
FX_COLS = 3 * W_FX + N_HEADS_FX
D_IN_PROJ = SB_COLS + GM_COLS + RW_COLS + FX_COLS
N_EXPERT_GROUPS = 4
EXPERTS_PER_GROUP = 8
N_EXPERTS = N_EXPERT_GROUPS * EXPERTS_PER_GROUP
TOP_K = 2
EXPERT_HIDDEN = 512
MOE_BLOCK = 128
RMS_EPS = 1e-6
LN_EPS = 1e-5
GN_EPS = 64e-5
L2_EPS = 1e-12

kernel_name = 'hymba_style_sb_gmlp_rwkv7_fox_hiermoe'


def rms_norm(x, g):
    xf = x.astype(jnp.float32)
    y = xf * lax.rsqrt(jnp.mean(xf * xf, axis=-1, keepdims=True) + RMS_EPS)
    return (y * g.astype(jnp.float32)).astype(x.dtype)


def split_heads(t, n_heads):
    bsz, seq, _ = t.shape
    return t.reshape(bsz, seq, n_heads, HEAD_DIM).transpose(0, 2, 1, 3)


def merge_heads(t):
    bsz, n_heads, seq, dh = t.shape
    return t.transpose(0, 2, 1, 3).reshape(bsz, seq, n_heads * dh)


def qkv_heads(p, n_heads, width):
    q = split_heads(p[..., 0:width], n_heads)
    k = split_heads(p[..., width:2 * width], n_heads)
    v = split_heads(p[..., 2 * width:3 * width], n_heads)
    return q, k, v


def stick_breaking_attention(q, k, v):
    seq = q.shape[2]
    scale = HEAD_DIM ** -0.5
    outs = []
    for q0 in range(0, seq, BLOCK_Q):
        k_end = q0 + BLOCK_Q
        z = jnp.einsum('bhqd,bhkd->bhqk', q[:, :, q0:k_end], k[:, :, :k_end]).astype(jnp.float32) * scale
        strict = jnp.arange(k_end)[None, :] < (q0 + jnp.arange(BLOCK_Q))[:, None]
        log_one_minus = jnp.where(strict, jax.nn.log_sigmoid(-z), 0.0)
        between = lax.cumsum(log_one_minus, axis=3, reverse=True) - log_one_minus
        attn = jnp.where(strict, jnp.exp(jax.nn.log_sigmoid(z) + between), 0.0)
        outs.append(jnp.einsum('bhqk,bhkd->bhqd', attn.astype(v.dtype), v[:, :, :k_end]))
    return jnp.concatenate(outs, axis=2)


def forgetting_attention(q, k, v, log_f):
    seq = q.shape[2]
    scale = HEAD_DIM ** -0.5
    cum = lax.cumsum(log_f, axis=2)
    outs = []
    for q0 in range(0, seq, BLOCK_Q):
        k_end = q0 + BLOCK_Q
        logits = jnp.einsum('bhqd,bhkd->bhqk', q[:, :, q0:k_end], k[:, :, :k_end]).astype(jnp.float32) * scale
        logits = logits + cum[:, :, q0:k_end, None] - cum[:, :, None, :k_end]
        causal = jnp.arange(k_end)[None, :] <= (q0 + jnp.arange(BLOCK_Q))[:, None]
        probs = jax.nn.softmax(jnp.where(causal, logits, -jnp.inf), axis=-1)
        outs.append(jnp.einsum('bhqk,bhkd->bhqd', probs.astype(v.dtype), v[:, :, :k_end]))
    return jnp.concatenate(outs, axis=2)


def chunked_spatial_gating(p, w_s, b_s):
    bsz, seq, _ = p.shape
    hid = jax.nn.gelu(p)
    u, v = hid[..., :W_GM], hid[..., W_GM:]
    v = v.reshape(bsz, seq // CHUNK, CHUNK, N_HEADS_GM, HEAD_DIM).astype(jnp.float32)
    mean = jnp.mean(v, axis=-1, keepdims=True)
    var = jnp.mean(jnp.square(v - mean), axis=-1, keepdims=True)
    v = (v - mean) * lax.rsqrt(var + LN_EPS)
    w_causal = jnp.tril(w_s.astype(jnp.float32))
    mixed = jnp.einsum('gts,bcsgd->bctgd', w_causal, v) + b_s.astype(jnp.float32).T[None, None, :, :, None]
    return u * mixed.reshape(bsz, seq, W_GM).astype(p.dtype)


def rwkv7_time_mix(p, mu, w0, w2, a0, a2, g2, k_k, k_a, r_k, gn_w, gn_b):
    bsz, seq, _ = p.shape
    f32 = jnp.float32
    p = p.astype(f32)
    prev = jnp.pad(p, ((0, 0), (1, 0), (0, 0)))[:, :-1]
    p = p + (prev - p) * mu.astype(f32)
    r = p[..., 0:W_RW]
    k = p[..., W_RW:2 * W_RW]
    v = p[..., 2 * W_RW:3 * W_RW]
    o = 3 * W_RW
    xw = p[..., o:o + RW_DECAY_LORA]
    o = o + RW_DECAY_LORA
    xa = p[..., o:o + RW_AAA_LORA]
    o = o + RW_AAA_LORA
    xg = p[..., o:o + RW_GATE_LORA]
    w = -jax.nn.softplus(-(w0.astype(f32) + jnp.tanh(xw) @ w2.astype(f32))) - 0.5
    decay = jnp.exp(-jnp.exp(w))
    a = jax.nn.sigmoid(a0.astype(f32) + xa @ a2.astype(f32))
    g = jax.nn.sigmoid(xg) @ g2.astype(f32)
    kk = (k * k_k.astype(f32)).reshape(bsz, seq, N_HEADS_RW, HEAD_DIM)
    kk = kk / jnp.maximum(jnp.sqrt(jnp.sum(kk * kk, axis=-1, keepdims=True)), L2_EPS)
    k = k * (1.0 + (a - 1.0) * k_a.astype(f32))
    shp = (bsz, seq, N_HEADS_RW, HEAD_DIM)
    r_h, k_h, v_h, a_h, d_h = r.reshape(shp), k.reshape(shp), v.reshape(shp), a.reshape(shp), decay.reshape(shp)

    def step(state, inp):
        r_t, d_t, k_t, v_t, kk_t, a_t = inp
        sa = jnp.einsum('bhvk,bhk->bhv', state, -kk_t)
        state = (state * d_t[:, :, None, :]
                 + sa[..., None] * (kk_t * a_t)[:, :, None, :]
                 + v_t[..., None] * k_t[:, :, None, :])
        return state, jnp.einsum('bhvk,bhk->bhv', state, r_t)

    xs = tuple(jnp.swapaxes(t, 0, 1) for t in (r_h, d_h, k_h, v_h, kk, a_h))
    state0 = jnp.zeros((bsz, N_HEADS_RW, HEAD_DIM, HEAD_DIM), f32)
    _, y = lax.scan(step, state0, xs)
    y = jnp.swapaxes(y, 0, 1)
    mean = jnp.mean(y, axis=-1, keepdims=True)
    var = jnp.mean(jnp.square(y - mean), axis=-1, keepdims=True)
    y = ((y - mean) * lax.rsqrt(var + GN_EPS)).reshape(bsz, seq, W_RW)
    y = y * gn_w.astype(f32) + gn_b.astype(f32)
    bonus = (jnp.sum(r_h * k_h * r_k.astype(f32), axis=-1, keepdims=True) * v_h).reshape(bsz, seq, W_RW)
    return (y + bonus) * g


def routed_experts(hf, expert_idx, gates, w_gate, w_up, w_down):
    n_tok, d = hf.shape
    n_assign = n_tok * TOP_K
    flat_e = expert_idx.reshape(n_assign)
    order = jnp.argsort(flat_e)
    sorted_e = flat_e[order]
    counts = jnp.bincount(flat_e, length=N_EXPERTS)
    padded = (counts + MOE_BLOCK - 1) // MOE_BLOCK * MOE_BLOCK
    pad_end = jnp.cumsum(padded)
    pad_start = pad_end - padded
    seg_start = jnp.cumsum(counts) - counts
    slot = pad_start[sorted_e] + jnp.arange(n_assign) - seg_start[sorted_e]
    n_blocks = -(-n_assign // MOE_BLOCK) + N_EXPERTS
    n_slots = n_blocks * MOE_BLOCK
    slot_token = jnp.full((n_slots,), n_tok, dtype=jnp.int32).at[slot].set((order // TOP_K).astype(jnp.int32))
    block_expert = jnp.minimum(
        jnp.searchsorted(pad_end, jnp.arange(n_blocks) * MOE_BLOCK, side='right'), N_EXPERTS - 1)
    h_pad = jnp.concatenate([hf, jnp.zeros((1, d), hf.dtype)], axis=0)
    xb = h_pad[slot_token].reshape(n_blocks, MOE_BLOCK, d)

    def expert_block(args):
        xe, e = args
        return (jax.nn.silu(xe @ w_gate[e]) * (xe @ w_up[e])) @ w_down[e]

    yb = lax.map(expert_block, (xb, block_expert)).reshape(n_slots, d)
    slot_of_assign = jnp.zeros((n_assign,), slot.dtype).at[order].set(slot)
    y_assign = yb[slot_of_assign].reshape(n_tok, TOP_K, d)
    return jnp.einsum('nkd,nk->nd', y_assign, gates.astype(yb.dtype))


def hierarchical_moe(h, wg, bg, we, be, w_gate, w_up, w_down):
    bsz, seq, d = h.shape
    n_tok = bsz * seq
    hf = h.reshape(n_tok, d)
    rows = jnp.arange(n_tok)
    group_logits = (hf @ wg + bg).astype(jnp.float32)
    group_prob = jax.nn.softmax(group_logits, axis=-1)
    g_sel = jnp.argmax(group_logits, axis=-1)
    g_gate = group_prob[rows, g_sel]
    expert_logits = (hf @ we + be).astype(jnp.float32).reshape(n_tok, N_EXPERT_GROUPS, EXPERTS_PER_GROUP)
    in_group = expert_logits[rows, g_sel]
    top_logits, top_local = lax.top_k(in_group, TOP_K)
    gates = jax.nn.softmax(top_logits, axis=-1) * g_gate[:, None]
    expert_idx = g_sel[:, None] * EXPERTS_PER_GROUP + top_local
    y = routed_experts(hf, expert_idx, gates, w_gate, w_up, w_down)
    return y.reshape(bsz, seq, d).astype(h.dtype)


def setup_inputs(seed: int = 0) -> dict:
    key = jax.random.key(seed)
    ks = jax.random.split(key, 32)
    f32 = jnp.float32

    def nrm(k, shape, scale):
        return jax.random.normal(k, shape, f32) * scale

    resid_scale = (2 * DEPTH) ** -0.5
    return {
        'x': nrm(ks[0], (BATCH, SEQ, D_MODEL), 1.0),
        'norm1_g': 1.0 + nrm(ks[1], (DEPTH, D_MODEL), 0.02),
        'w_in': nrm(ks[2], (DEPTH, D_MODEL, D_IN_PROJ), D_MODEL ** -0.5),
        'gm_w_s': nrm(ks[3], (DEPTH, N_HEADS_GM, CHUNK, CHUNK), CHUNK ** -0.5),
        'gm_b': nrm(ks[4], (DEPTH, N_HEADS_GM, CHUNK), 0.02),
        'rw_mu': jax.random.uniform(ks[5], (DEPTH, RW_COLS), f32),
        'rw_w0': nrm(ks[6], (DEPTH, W_RW), 0.5),
        'rw_w2': nrm(ks[7], (DEPTH, RW_DECAY_LORA, W_RW), 0.5 * RW_DECAY_LORA ** -0.5),
        'rw_a0': nrm(ks[8], (DEPTH, W_RW), 0.1),
        'rw_a2': nrm(ks[9], (DEPTH, RW_AAA_LORA, W_RW), 0.5 * RW_AAA_LORA ** -0.5),
        'rw_g2': nrm(ks[10], (DEPTH, RW_GATE_LORA, W_RW), RW_GATE_LORA ** -0.5),
        'rw_k_k': 0.85 + nrm(ks[11], (DEPTH, W_RW), 0.02),
        'rw_k_a': 1.0 + nrm(ks[12], (DEPTH, W_RW), 0.02),
        'rw_r_k': nrm(ks[13], (DEPTH, N_HEADS_RW, HEAD_DIM), 0.1),
        'rw_gn_w': 1.0 + nrm(ks[14], (DEPTH, W_RW), 0.02),
        'rw_gn_b': nrm(ks[15], (DEPTH, W_RW), 0.02),
        'fx_b_f': nrm(ks[16], (DEPTH, N_HEADS_FX), 0.1),
        'w_out': nrm(ks[17], (DEPTH, D_MIX, D_MODEL), D_MIX ** -0.5 * resid_scale),
        'norm2_g': 1.0 + nrm(ks[18], (DEPTH, D_MODEL), 0.02),
        'router_group_w': nrm(ks[19], (DEPTH, D_MODEL, N_EXPERT_GROUPS), D_MODEL ** -0.5),
        'router_group_b': nrm(ks[20], (DEPTH, N_EXPERT_GROUPS), 0.01),
        'router_expert_w': nrm(ks[21], (DEPTH, D_MODEL, N_EXPERTS), D_MODEL ** -0.5),
        'router_expert_b': nrm(ks[22], (DEPTH, N_EXPERTS), 0.01),
        'exp_w_gate': nrm(ks[23], (DEPTH, N_EXPERTS, D_MODEL, EXPERT_HIDDEN), D_MODEL ** -0.5),
        'exp_w_up': nrm(ks[24], (DEPTH, N_EXPERTS, D_MODEL, EXPERT_HIDDEN), D_MODEL ** -0.5),
        'exp_w_down': nrm(ks[25], (DEPTH, N_EXPERTS, EXPERT_HIDDEN, D_MODEL), EXPERT_HIDDEN ** -0.5 * resid_scale),
        'final_norm_g': 1.0 + nrm(ks[26], (D_MODEL,), 0.02),
    }


def reference(x, norm1_g, w_in, gm_w_s, gm_b, rw_mu, rw_w0, rw_w2, rw_a0, rw_a2, rw_g2,
              rw_k_k, rw_k_a, rw_r_k, rw_gn_w, rw_gn_b, fx_b_f, w_out, norm2_g,
              router_group_w, router_group_b, router_expert_w, router_expert_b,
              exp_w_gate, exp_w_up, exp_w_down, final_norm_g):
    splits = [SB_COLS, SB_COLS + GM_COLS, SB_COLS + GM_COLS + RW_COLS]
    for layer in range(DEPTH):
        h = rms_norm(x, norm1_g[layer])
        proj = h @ w_in[layer]
        p_sb, p_gm, p_rw, p_fx = jnp.split(proj, splits, axis=-1)

        q, k, v = qkv_heads(p_sb, N_HEADS_SB, W_SB)
        y_sb = merge_heads(stick_breaking_attention(q, k, v))

        y_gm = chunked_spatial_gating(p_gm, gm_w_s[layer], gm_b[layer])

        y_rw = rwkv7_time_mix(p_rw, rw_mu[layer], rw_w0[layer], rw_w2[layer], rw_a0[layer],
                              rw_a2[layer], rw_g2[layer], rw_k_k[layer], rw_k_a[layer],
                              rw_r_k[layer], rw_gn_w[layer], rw_gn_b[layer]).astype(x.dtype)

        q, k, v = qkv_heads(p_fx, N_HEADS_FX, W_FX)
        log_f = jax.nn.log_sigmoid(p_fx[..., 3 * W_FX:].astype(jnp.float32)
                                   + fx_b_f[layer].astype(jnp.float32))
        y_fx = merge_heads(forgetting_attention(q, k, v, jnp.transpose(log_f, (0, 2, 1))))

        mix = jnp.concatenate([y_sb, y_gm, y_rw, y_fx], axis=-1)
        x = x + mix @ w_out[layer]

        h2 = rms_norm(x, norm2_g[layer])
        x = x + hierarchical_moe(h2, router_group_w[layer], router_group_b[layer],
                                 router_expert_w[layer], router_expert_b[layer],
                                 exp_w_gate[layer], exp_w_up[layer], exp_w_down[layer])
    return rms_norm(x, final_norm_g)
```

```python
import functools

import jax
import jax.numpy as jnp
from jax import lax
from jax.experimental import pallas as pl
from jax.experimental.pallas import tpu as pltpu

F32 = jnp.float32
BF16 = jnp.bfloat16
HIGHEST = lax.Precision.HIGHEST

HEAD_DIM = 64
N_HEADS = 4
W_MIX = N_HEADS * HEAD_DIM
GM_CHUNK = 128
RW_DECAY_LORA = 64
RW_AAA_LORA = 64
RW_GATE_LORA = 128
RW_COLS = 3 * W_MIX + RW_DECAY_LORA + RW_AAA_LORA + RW_GATE_LORA
N_EXPERT_GROUPS = 4
EXPERTS_PER_GROUP = 8
N_EXPERTS = N_EXPERT_GROUPS * EXPERTS_PER_GROUP
TOP_K = 2
RMS_EPS = 1e-6
LN_EPS = 1e-5
GN_EPS = 64e-5
L2_EPS = 1e-12

V7X_VMEM_LIMIT_BYTES = 56 * 1024 * 1024
LANES = 128
SUBLANES = 8

ATT_TQ = 128
ATT_TK = 128
RW_CHUNK = 64
ROW_TILE = 512
MOE_TB = 256


def _cparams(*sem):
    return pltpu.CompilerParams(dimension_semantics=sem, vmem_limit_bytes=V7X_VMEM_LIMIT_BYTES)


def _dot(a, b, precision=None):
    return jnp.dot(a, b, preferred_element_type=F32, precision=precision)


def _dot_nt(a, b, precision=None):
    return lax.dot_general(a, b, (((1,), (1,)), ((), ())), preferred_element_type=F32,
                           precision=precision)


def _dot_tn(a, b, precision=None):
    return lax.dot_general(a, b, (((0,), (0,)), ((), ())), preferred_element_type=F32,
                           precision=precision)


def _iota2(shape, dim):
    return lax.broadcasted_iota(jnp.int32, shape, dim)


def _norm_inproj_kernel(x_ref, g_ref, w_ref, wgt_ref, sb_ref, gm_ref, rw_ref, fx_ref, grow_ref):
    x = x_ref[...]
    h = x * lax.rsqrt(jnp.mean(x * x, axis=-1, keepdims=True) + RMS_EPS) * g_ref[...]
    hb = h.astype(BF16)
    p = _dot(hb, w_ref[...])
    scale = HEAD_DIM ** -0.5
    o = 0
    sb_ref[:, :W_MIX] = (p[:, o:o + W_MIX] * scale).astype(BF16)
    sb_ref[:, W_MIX:] = p[:, o + W_MIX:o + 3 * W_MIX].astype(BF16)
    o += 3 * W_MIX
    gm_ref[...] = p[:, o:o + 2 * W_MIX]
    o += 2 * W_MIX
    rw_ref[...] = p[:, o:o + RW_COLS]
    o += RW_COLS
    fx_ref[:, :W_MIX] = (p[:, o:o + W_MIX] * scale).astype(BF16)
    fx_ref[:, W_MIX:] = p[:, o + W_MIX:o + 3 * W_MIX].astype(BF16)
    grow_ref[...] = _dot_nt(wgt_ref[...], hb)


def _norm_inproj(x2, g, w_main, w_gate_t):
    n, d = x2.shape
    tm = min(ROW_TILE, n)
    cols = w_main.shape[1]
    return pl.pallas_call(
        _norm_inproj_kernel,
        grid=(n // tm,),
        in_specs=[
            pl.BlockSpec((tm, d), lambda i: (i, 0)),
            pl.BlockSpec((1, d), lambda i: (0, 0)),
            pl.BlockSpec((d, cols), lambda i: (0, 0)),
            pl.BlockSpec((SUBLANES, d), lambda i: (0, 0)),
        ],
        out_specs=[
            pl.BlockSpec((tm, 3 * W_MIX), lambda i: (i, 0)),
            pl.BlockSpec((tm, 2 * W_MIX), lambda i: (i, 0)),
            pl.BlockSpec((tm, RW_COLS), lambda i: (i, 0)),
            pl.BlockSpec((tm, 3 * W_MIX), lambda i: (i, 0)),
            pl.BlockSpec((SUBLANES, tm), lambda i: (0, i)),
        ],
        out_shape=[
            jax.ShapeDtypeStruct((n, 3 * W_MIX), BF16),
            jax.ShapeDtypeStruct((n, 2 * W_MIX), F32),
            jax.ShapeDtypeStruct((n, RW_COLS), F32),
            jax.ShapeDtypeStruct((n, 3 * W_MIX), BF16),
            jax.ShapeDtypeStruct((SUBLANES, n), F32),
        ],
        compiler_params=_cparams("parallel"),
        name="norm_inproj",
    )(x2, g, w_main, w_gate_t)


def _sb_kernel(q_ref, k_ref, v_ref, tri_ref, o_ref):
    i = pl.program_id(1)
    row = _iota2((ATT_TQ, ATT_TK), 0)
    col = _iota2((ATT_TQ, ATT_TK), 1)
    strict = col < row

    for h in range(N_HEADS):
        sl = slice(h * HEAD_DIM, (h + 1) * HEAD_DIM)
        q = q_ref[0, :, sl]

        def tile(j, c, acc, masked):
            start = pl.multiple_of(j * ATT_TK, ATT_TK)
            k = k_ref[0, pl.ds(start, ATT_TK), sl]
            v = v_ref[0, pl.ds(start, ATT_TK), sl]
            z = _dot_nt(q, k)
            lom = -(jnp.maximum(z, 0.0) + jnp.log(1.0 + jnp.exp(-jnp.abs(z))))
            if masked:
                lom = jnp.where(strict, lom, 0.0)
            hi = lom.astype(BF16)
            lo = (lom - hi.astype(F32)).astype(BF16)
            rr = _dot(jnp.concatenate([hi, lo], axis=1), tri_ref[...])
            p = jnp.exp(z + rr[:, :ATT_TK] + c)
            if masked:
                p = jnp.where(strict, p, 0.0)
            acc = acc + _dot(p.astype(BF16), v)
            return c + rr[:, ATT_TK:], acc

        c0 = jnp.zeros((ATT_TQ, ATT_TK), F32)
        acc0 = jnp.zeros((ATT_TQ, HEAD_DIM), F32)
        c, acc = tile(i, c0, acc0, True)

        def body(jj, carry):
            return tile(i - 1 - jj, carry[0], carry[1], False)

        c, acc = lax.fori_loop(0, i, body, (c, acc))
        o_ref[0, :, sl] = acc.astype(o_ref.dtype)


def _sb_attention(qkv, bsz, seq):
    qkv3 = qkv.reshape(bsz, seq, 3 * W_MIX)
    r = jnp.arange(ATT_TK)
    tri = (r[:, None] >= r[None, :]).astype(BF16)
    blk = jnp.concatenate([tri, jnp.ones((ATT_TK, ATT_TK), BF16)], axis=1)
    tri2 = jnp.concatenate([blk, blk], axis=0)
    out = pl.pallas_call(
        _sb_kernel,
        grid=(bsz, seq // ATT_TQ),
        in_specs=[
            pl.BlockSpec((1, ATT_TQ, W_MIX), lambda b, i: (b, i, 0)),
            pl.BlockSpec((1, seq, W_MIX), lambda b, i: (b, 0, 1)),
            pl.BlockSpec((1, seq, W_MIX), lambda b, i: (b, 0, 2)),
            pl.BlockSpec((2 * ATT_TK, 2 * ATT_TK), lambda b, i: (0, 0)),
        ],
        out_specs=pl.BlockSpec((1, ATT_TQ, W_MIX), lambda b, i: (b, i, 0)),
        out_shape=jax.ShapeDtypeStruct((bsz, seq, W_MIX), BF16),
        compiler_params=_cparams("parallel", "parallel"),
        name="sb_attention",
    )(qkv3, qkv3, qkv3, tri2)
    return out.reshape(bsz * seq, W_MIX)


def _fx_cum_kernel(g_ref, b_ref, cum_ref):
    seq = g_ref.shape[2]
    blk = 2 * LANES
    r = _iota2((blk, blk), 0)
    c = _iota2((blk, blk), 1)
    tri = (r <= c).astype(F32)
    carry = jnp.zeros((SUBLANES, 1), F32)
    for s in range(seq // blk):
        x = g_ref[0, :, s * blk:(s + 1) * blk] + b_ref[...]
        lf = -(jnp.maximum(-x, 0.0) + jnp.log(1.0 + jnp.exp(-jnp.abs(x))))
        cs = _dot(lf, tri, precision=HIGHEST) + carry
        cum_ref[0, :, s * blk:(s + 1) * blk] = cs
        carry = cs[:, blk - 1:blk]


def _fx_cum(grow3, bias_col):
    bsz, _, seq = grow3.shape
    return pl.pallas_call(
        _fx_cum_kernel,
        grid=(bsz,),
        in_specs=[
            pl.BlockSpec((1, SUBLANES, seq), lambda b: (b, 0, 0)),
            pl.BlockSpec((SUBLANES, 1), lambda b: (0, 0)),
        ],
        out_specs=pl.BlockSpec((1, SUBLANES, seq), lambda b: (b, 0, 0)),
        out_shape=jax.ShapeDtypeStruct((bsz, SUBLANES, seq), F32),
        compiler_params=_cparams("parallel"),
        name="fx_cum",
    )(grow3, bias_col)


def _fx_kernel(q_ref, k_ref, v_ref, crow_ref, ccol_ref, o_ref):
    i = pl.program_id(1)
    row = _iota2((ATT_TQ, ATT_TK), 0)
    col = _iota2((ATT_TQ, ATT_TK), 1)
    causal = col <= row

    for h in range(N_HEADS):
        sl = slice(h * HEAD_DIM, (h + 1) * HEAD_DIM)
        q = q_ref[0, :, sl]
        ct = ccol_ref[0, :, h:h + 1]

        def logits(j):
            start = pl.multiple_of(j * ATT_TK, ATT_TK)
            k = k_ref[0, pl.ds(start, ATT_TK), sl]
            v = v_ref[0, pl.ds(start, ATT_TK), sl]
            cs = crow_ref[0, h:h + 1, pl.ds(start, ATT_TK)]
            return _dot_nt(q, k) + ct - cs, v

        x, v = logits(i)
        x = jnp.where(causal, x, -jnp.inf)
        m = jnp.max(x, axis=-1, keepdims=True)
        p = jnp.exp(x - m)
        l = jnp.sum(p, axis=-1, keepdims=True)
        acc = _dot(p.astype(BF16), v)

        def body(j, carry):
            m, l, acc = carry
            x, v = logits(j)
            m_new = jnp.maximum(m, jnp.max(x, axis=-1, keepdims=True))
            a = jnp.exp(m - m_new)
            p = jnp.exp(x - m_new)
            l = a * l + jnp.sum(p, axis=-1, keepdims=True)
            acc = a * acc + _dot(p.astype(BF16), v)
            return m_new, l, acc

        m, l, acc = lax.fori_loop(0, i, body, (m, l, acc))
        o_ref[0, :, sl] = (acc / l).astype(o_ref.dtype)


def _fx_attention(qkv, cum_row, bsz, seq):
    qkv3 = qkv.reshape(bsz, seq, 3 * W_MIX)
    cum_col = jnp.transpose(cum_row, (0, 2, 1))
    out = pl.pallas_call(
        _fx_kernel,
        grid=(bsz, seq // ATT_TQ),
        in_specs=[
            pl.BlockSpec((1, ATT_TQ, W_MIX), lambda b, i: (b, i, 0)),
            pl.BlockSpec((1, seq, W_MIX), lambda b, i: (b, 0, 1)),
            pl.BlockSpec((1, seq, W_MIX), lambda b, i: (b, 0, 2)),
            pl.BlockSpec((1, SUBLANES, seq), lambda b, i: (b, 0, 0)),
            pl.BlockSpec((1, ATT_TQ, SUBLANES), lambda b, i: (b, i, 0)),
        ],
        out_specs=pl.BlockSpec((1, ATT_TQ, W_MIX), lambda b, i: (b, i, 0)),
        out_shape=jax.ShapeDtypeStruct((bsz, seq, W_MIX), BF16),
        compiler_params=_cparams("parallel", "parallel"),
        name="fx_attention",
    )(qkv3, qkv3, qkv3, cum_row, cum_col)
    return out.reshape(bsz * seq, W_MIX)


def _gm_kernel(p_ref, w_ref, b_ref, o_ref):
    hid = jax.nn.gelu(p_ref[...])
    r = _iota2((GM_CHUNK, GM_CHUNK), 0)
    c = _iota2((GM_CHUNK, GM_CHUNK), 1)
    lower = c <= r
    for g in range(N_HEADS):
        u = hid[:, g * HEAD_DIM:(g + 1) * HEAD_DIM]
        v = hid[:, W_MIX + g * HEAD_DIM:W_MIX + (g + 1) * HEAD_DIM]
        mean = jnp.mean(v, axis=-1, keepdims=True)
        var = jnp.mean(jnp.square(v - mean), axis=-1, keepdims=True)
        vn = (v - mean) * lax.rsqrt(var + LN_EPS)
        w = jnp.where(lower, w_ref[g], 0.0).astype(BF16)
        mixed = _dot(w, vn.astype(BF16)) + b_ref[:, g:g + 1]
        o_ref[:, g * HEAD_DIM:(g + 1) * HEAD_DIM] = (u * mixed).astype(o_ref.dtype)


def _gm_mix(p_gm, w_s, b_col):
    n = p_gm.shape[0]
    return pl.pallas_call(
        _gm_kernel,
        grid=(n // GM_CHUNK,),
        in_specs=[
            pl.BlockSpec((GM_CHUNK, 2 * W_MIX), lambda i: (i, 0)),
            pl.BlockSpec((N_HEADS, GM_CHUNK, GM_CHUNK), lambda i: (0, 0, 0)),
            pl.BlockSpec((GM_CHUNK, N_HEADS), lambda i: (0, 0)),
        ],
        out_specs=pl.BlockSpec((GM_CHUNK, W_MIX), lambda i: (i, 0)),
        out_shape=jax.ShapeDtypeStruct((n, W_MIX), BF16),
        compiler_params=_cparams("parallel"),
        name="gm_mix",
    )(p_gm, w_s, b_col)


def _head_sum(x, ones_bd):
    return _dot(x, ones_bd, precision=HIGHEST)


def _rw_prep_kernel(p_ref, prev_ref, mu_ref, w0_ref, w2_ref, a0_ref, a2_ref, g2_ref,
                    kk_ref, ka_ref, rk_ref, bd_ref,
                    r_out, ld_out, k_out, v_out, kk_out, ba_out, bonus_out, g_out, *, seq):
    i = pl.program_id(0)
    tm = p_ref.shape[0]
    p = p_ref[...]
    first = (i * tm) % seq == 0
    prev_row = jnp.where(first, 0.0, prev_ref[SUBLANES - 1:SUBLANES, :])
    rows = _iota2(p.shape, 0)
    prev = jnp.where(rows == 0, prev_row, pltpu.roll(p, 1, axis=0))
    p = p + (prev - p) * mu_ref[...]
    r = p[:, 0:W_MIX]
    k = p[:, W_MIX:2 * W_MIX]
    v = p[:, 2 * W_MIX:3 * W_MIX]
    o = 3 * W_MIX
    xw = p[:, o:o + RW_DECAY_LORA]
    o += RW_DECAY_LORA
    xa = p[:, o:o + RW_AAA_LORA]
    o += RW_AAA_LORA
    xg = p[:, o:o + RW_GATE_LORA]
    wpre = -(w0_ref[...] + _dot(jnp.tanh(xw).astype(BF16), w2_ref[...]))
    w = -(jnp.maximum(wpre, 0.0) + jnp.log(1.0 + jnp.exp(-jnp.abs(wpre)))) - 0.5
    a = jax.nn.sigmoid(a0_ref[...] + _dot(xa.astype(BF16), a2_ref[...]))
    g = _dot(jax.nn.sigmoid(xg).astype(BF16), g2_ref[...])
    kk = k * kk_ref[...]
    nrm = jnp.maximum(jnp.sqrt(_head_sum(kk * kk, bd_ref[...])), L2_EPS)
    kk = kk / nrm
    k2 = k * (1.0 + (a - 1.0) * ka_ref[...])
    r_out[...] = r
    ld_out[...] = -jnp.exp(w)
    k_out[...] = k2
    v_out[...] = v
    kk_out[...] = kk
    ba_out[...] = kk * a
    bonus_out[...] = _head_sum(r * k2 * rk_ref[...], bd_ref[...]) * v
    g_out[...] = g


def _rw_prep(p_rw, seq, mu, w0, w2, a0, a2, g2, k_k, k_a, r_k, ones_bd):
    n = p_rw.shape[0]
    tm = min(256, seq)
    vec = lambda width: pl.BlockSpec((1, width), lambda i: (0, 0))
    mat = lambda rws: pl.BlockSpec((rws, W_MIX), lambda i: (0, 0))
    out_spec = pl.BlockSpec((tm, W_MIX), lambda i: (i, 0))
    return pl.pallas_call(
        functools.partial(_rw_prep_kernel, seq=seq),
        grid=(n // tm,),
        in_specs=[
            pl.BlockSpec((tm, RW_COLS), lambda i: (i, 0)),
            pl.BlockSpec((SUBLANES, RW_COLS),
                         lambda i: (jnp.maximum(i * (tm // SUBLANES) - 1, 0), 0)),
            vec(RW_COLS), vec(W_MIX), mat(RW_DECAY_LORA), vec(W_MIX), mat(RW_AAA_LORA),
            mat(RW_GATE_LORA), vec(W_MIX), vec(W_MIX), vec(W_MIX), mat(W_MIX),
        ],
        out_specs=[out_spec] * 8,
        out_shape=[jax.ShapeDtypeStruct((n, W_MIX), F32)] * 8,
        compiler_params=_cparams("parallel"),
        name="rw_prep",
    )(p_rw, p_rw, mu, w0, w2, a0, a2, g2, k_k, k_a, r_k, ones_bd)


def _rw_chunk_kernel(r_ref, ld_ref, k_ref, v_ref, kk_ref, ba_ref, bonus_ref, g_ref,
                     gnw_ref, gnb_ref, o_ref, state_ref):
    cs = RW_CHUNK

    @pl.when(pl.program_id(1) == 0)
    def _():
        state_ref[...] = jnp.zeros_like(state_ref)

    row = _iota2((cs, cs), 0)
    col = _iota2((cs, cs), 1)
    lower = col <= row
    strict = col < row
    tri = lower.astype(F32)
    eye = (row == col).astype(F32)

    for h in range(N_HEADS):
        sl = slice(h * HEAD_DIM, (h + 1) * HEAD_DIM)
        ld = ld_ref[0, :, sl]
        cl = _dot(tri, ld, precision=HIGHEST)
        p_in = jnp.exp(cl)
        p_inv = jnp.exp(-cl)
        at = -kk_ref[0, :, sl] * jnp.exp(cl - ld)
        bt = ba_ref[0, :, sl] * p_inv
        kt = k_ref[0, :, sl] * p_inv
        r = r_ref[0, :, sl]
        rt = r * p_in
        v = v_ref[0, :, sl]
        s0 = state_ref[h]

        a_ab = jnp.where(strict, _dot_nt(at, bt, HIGHEST), 0.0)
        a_ak = jnp.where(strict, _dot_nt(at, kt, HIGHEST), 0.0)
        m_rb = jnp.where(lower, _dot_nt(rt, bt, HIGHEST), 0.0)
        m_rk = jnp.where(lower, _dot_nt(rt, kt, HIGHEST), 0.0)

        inv = eye + a_ab
        pw = a_ab
        steps = 1
        while steps * 2 < cs:
            pw = _dot(pw, pw, HIGHEST)
            inv = inv + _dot(inv, pw, HIGHEST)
            steps *= 2

        rhs = _dot_nt(at, s0, HIGHEST) + _dot(a_ak, v, HIGHEST)
        u = _dot(inv, rhs, HIGHEST)
        y = _dot_nt(rt, s0, HIGHEST) + _dot(m_rb, u, HIGHEST) + _dot(m_rk, v, HIGHEST)
        s_new = (s0 + _dot_tn(u, bt, HIGHEST) + _dot_tn(v, kt, HIGHEST)) * p_in[cs - 1:cs, :]
        state_ref[h] = s_new

        mean = jnp.mean(y, axis=-1, keepdims=True)
        var = jnp.mean(jnp.square(y - mean), axis=-1, keepdims=True)
        yn = (y - mean) * lax.rsqrt(var + GN_EPS)
        yn = yn * gnw_ref[:, sl] + gnb_ref[:, sl]
        o_ref[0, :, sl] = ((yn + bonus_ref[0, :, sl]) * g_ref[0, :, sl]).astype(o_ref.dtype)


def _rw_chunks(prep, gn_w, gn_b, bsz, seq):
    arrs = [a.reshape(bsz, seq, W_MIX) for a in prep]
    spec = pl.BlockSpec((1, RW_CHUNK, W_MIX), lambda b, c: (b, c, 0))
    vec = pl.BlockSpec((1, W_MIX), lambda b, c: (0, 0))
    out = pl.pallas_call(
        _rw_chunk_kernel,
        grid=(bsz, seq // RW_CHUNK),
        in_specs=[spec] * 8 + [vec, vec],
        out_specs=spec,
        out_shape=jax.ShapeDtypeStruct((bsz, seq, W_MIX), BF16),
        scratch_shapes=[pltpu.VMEM((N_HEADS, HEAD_DIM, HEAD_DIM), F32)],
        compiler_params=_cparams("parallel", "arbitrary"),
        name="rw_chunks",
    )(*arrs, gn_w, gn_b)
    return out.reshape(bsz * seq, W_MIX)


def _store_token_rows(ref, val):
    tm, d = val.shape
    ch = d // LANES
    for s in range(ch):
        ref[pl.ds(s, tm, stride=ch), :] = val[:, s * LANES:(s + 1) * LANES]


def _load_token_rows(ref, tm, ch):
    return jnp.concatenate([ref[pl.ds(s, tm, stride=ch), :] for s in range(ch)], axis=1)


def _outproj_router_kernel(x_ref, sb_ref, gm_ref, rw_ref, fx_ref, wo_ref, g_ref, wr_ref, br_ref,
                           x_out, h_out, idx_out, gate_out):
    mix = jnp.concatenate([sb_ref[...], gm_ref[...], rw_ref[...], fx_ref[...]], axis=1)
    x = x_ref[...] + _dot(mix, wo_ref[...])
    x_out[...] = x
    h = x * lax.rsqrt(jnp.mean(x * x, axis=-1, keepdims=True) + RMS_EPS) * g_ref[...]
    _store_token_rows(h_out, h)

    lg = _dot_nt(wr_ref[...], h, HIGHEST) + br_ref[...]
    tm = lg.shape[1]
    gl = [lg[g:g + 1, :] for g in range(N_EXPERT_GROUPS)]
    gmax = gl[0]
    gsel = jnp.zeros((1, tm), jnp.int32)
    for g in range(1, N_EXPERT_GROUPS):
        better = gl[g] > gmax
        gsel = jnp.where(better, g, gsel)
        gmax = jnp.where(better, gl[g], gmax)
    denom = gl[0] * 0.0
    for g in range(N_EXPERT_GROUPS):
        denom = denom + jnp.exp(gl[g] - gmax)
    g_gate = 1.0 / denom

    e0 = SUBLANES
    ing = lg[e0:e0 + EXPERTS_PER_GROUP, :]
    for g in range(1, N_EXPERT_GROUPS):
        ing = jnp.where(gsel == g, lg[e0 + g * EXPERTS_PER_GROUP:e0 + (g + 1) * EXPERTS_PER_GROUP, :], ing)
    ridx = _iota2(ing.shape, 0)
    m1 = jnp.max(ing, axis=0, keepdims=True)
    i1 = jnp.min(jnp.where(ing == m1, ridx, EXPERTS_PER_GROUP), axis=0, keepdims=True)
    rest = jnp.where(ridx == i1, -jnp.inf, ing)
    m2 = jnp.max(rest, axis=0, keepdims=True)
    i2 = jnp.min(jnp.where(rest == m2, ridx, EXPERTS_PER_GROUP), axis=0, keepdims=True)
    e2 = jnp.exp(m2 - m1)
    s = 1.0 + e2
    idx_out[0:1, :] = gsel * EXPERTS_PER_GROUP + i1
    idx_out[1:2, :] = gsel * EXPERTS_PER_GROUP + i2
    gate_out[0:1, :] = (1.0 / s) * g_gate
    gate_out[1:2, :] = (e2 / s) * g_gate


def _outproj_router(x2, ys, w_out, g2, w_router_t, b_router):
    n, d = x2.shape
    tm = min(ROW_TILE, n)
    nr = w_router_t.shape[0]
    ymix = pl.BlockSpec((tm, W_MIX), lambda i: (i, 0))
    return pl.pallas_call(
        _outproj_router_kernel,
        grid=(n // tm,),
        in_specs=[
            pl.BlockSpec((tm, d), lambda i: (i, 0)),
            ymix, ymix, ymix, ymix,
            pl.BlockSpec((4 * W_MIX, d), lambda i: (0, 0)),
            pl.BlockSpec((1, d), lambda i: (0, 0)),
            pl.BlockSpec((nr, d), lambda i: (0, 0)),
            pl.BlockSpec((nr, 1), lambda i: (0, 0)),
        ],
        out_specs=[
            pl.BlockSpec((tm, d), lambda i: (i, 0)),
            pl.BlockSpec((tm * (d // LANES), LANES), lambda i: (i, 0)),
            pl.BlockSpec((TOP_K, tm), lambda i: (0, i)),
            pl.BlockSpec((TOP_K, tm), lambda i: (0, i)),
        ],
        out_shape=[
            jax.ShapeDtypeStruct((n, d), F32),
            jax.ShapeDtypeStruct((n * (d // LANES), LANES), F32),
            jax.ShapeDtypeStruct((TOP_K, n), jnp.int32),
            jax.ShapeDtypeStruct((TOP_K, n), F32),
        ],
        compiler_params=_cparams("parallel"),
        name="outproj_router",
    )(x2, *ys, w_out, g2, w_router_t, b_router)


def _token_copy(src_hbm, dst, src_tok, dst_tok, sem, ch):
    return pltpu.make_async_copy(
        src_hbm.at[pl.ds(pl.multiple_of(src_tok * ch, ch), ch)],
        dst.at[pl.ds(pl.multiple_of(dst_tok * ch, ch), ch)], sem)


def _gather_rows_kernel(tok_ref, h_hbm, o_ref, sem, *, ch):
    tb = o_ref.shape[0] // ch

    def start(r, _):
        _token_copy(h_hbm, o_ref, tok_ref[0, 0, r], r, sem, ch).start()
        return 0

    lax.fori_loop(0, tb, start, 0)

    def wait(r, _):
        _token_copy(h_hbm, o_ref, 0, r, sem, ch).wait()
        return 0

    lax.fori_loop(0, tb, wait, 0)


def _gather_rows(slot_token, h_rows, ch):
    n_slots = slot_token.shape[0]
    n_blocks = n_slots // MOE_TB
    return pl.pallas_call(
        functools.partial(_gather_rows_kernel, ch=ch),
        grid=(n_blocks,),
        in_specs=[
            pl.BlockSpec((1, 1, MOE_TB), lambda i: (i, 0, 0), memory_space=pltpu.SMEM),
            pl.BlockSpec(memory_space=pl.ANY),
        ],
        out_specs=pl.BlockSpec((MOE_TB * ch, LANES), lambda i: (i, 0)),
        scratch_shapes=[pltpu.SemaphoreType.DMA],
        out_shape=jax.ShapeDtypeStruct((n_slots * ch, LANES), h_rows.dtype),
        compiler_params=_cparams("arbitrary"),
        name="moe_gather",
    )(slot_token.reshape(n_blocks, 1, MOE_TB), h_rows)


def _expert_kernel(be_ref, nb_ref, x_ref, wg_ref, wu_ref, wd_ref, o_ref, *, ch):
    i = pl.program_id(0)

    @pl.when(i < nb_ref[0])
    def _():
        x = _load_token_rows(x_ref, MOE_TB, ch).astype(BF16)
        a = _dot(x, wg_ref[0])
        u = _dot(x, wu_ref[0])
        hid = (a * jax.nn.sigmoid(a) * u).astype(BF16)
        _store_token_rows(o_ref, _dot(hid, wd_ref[0]))

    @pl.when(i >= nb_ref[0])
    def _():
        o_ref[...] = jnp.zeros_like(o_ref)


def _expert_blocks(block_expert, n_used, xb_rows, w_gate, w_up, w_down):
    d, hid = w_gate.shape[1], w_gate.shape[2]
    ch = d // LANES
    n_slots = xb_rows.shape[0] // ch
    return pl.pallas_call(
        functools.partial(_expert_kernel, ch=ch),
        grid_spec=pltpu.PrefetchScalarGridSpec(
            num_scalar_prefetch=2,
            grid=(n_slots // MOE_TB,),
            in_specs=[
                pl.BlockSpec((MOE_TB * ch, LANES), lambda i, be, nb: (i, 0)),
                pl.BlockSpec((1, d, hid), lambda i, be, nb: (be[i], 0, 0)),
                pl.BlockSpec((1, d, hid), lambda i, be, nb: (be[i], 0, 0)),
                pl.BlockSpec((1, hid, d), lambda i, be, nb: (be[i], 0, 0)),
            ],
            out_specs=pl.BlockSpec((MOE_TB * ch, LANES), lambda i, be, nb: (i, 0)),
        ),
        out_shape=jax.ShapeDtypeStruct((n_slots * ch, LANES), F32),
        compiler_params=_cparams("arbitrary"),
        name="moe_experts",
    )(block_expert, n_used, xb_rows, w_gate, w_up, w_down)


def _combine_kernel(slot_ref, x_ref, gate_ref, yb_hbm, o_ref, buf, sem, *, ch):
    tm = x_ref.shape[0]

    def start(r, _):
        for k in range(TOP_K):
            _token_copy(yb_hbm, buf.at[k], slot_ref[0, 0, r * TOP_K + k], r, sem, ch).start()
        return 0

    lax.fori_loop(0, tm, start, 0)

    def wait(r, _):
        for k in range(TOP_K):
            _token_copy(yb_hbm, buf.at[k], 0, r, sem, ch).wait()
        return 0

    lax.fori_loop(0, tm, wait, 0)
    y = (_load_token_rows(buf.at[0], tm, ch) * gate_ref[:, 0:1]
         + _load_token_rows(buf.at[1], tm, ch) * gate_ref[:, 1:2])
    o_ref[...] = x_ref[...] + y


def _combine(slot_flat, x2, gates, yb_rows):
    n, d = x2.shape
    ch = d // LANES
    tm = min(256, n)
    return pl.pallas_call(
        functools.partial(_combine_kernel, ch=ch),
        grid=(n // tm,),
        in_specs=[
            pl.BlockSpec((1, 1, tm * TOP_K), lambda i: (i, 0, 0), memory_space=pltpu.SMEM),
            pl.BlockSpec((tm, d), lambda i: (i, 0)),
            pl.BlockSpec((tm, TOP_K), lambda i: (i, 0)),
            pl.BlockSpec(memory_space=pl.ANY),
        ],
        out_specs=pl.BlockSpec((tm, d), lambda i: (i, 0)),
        scratch_shapes=[pltpu.VMEM((TOP_K, tm * ch, LANES), F32), pltpu.SemaphoreType.DMA],
        out_shape=jax.ShapeDtypeStruct((n, d), F32),
        compiler_params=_cparams("arbitrary"),
        name="moe_combine",
    )(slot_flat.reshape(n // tm, 1, tm * TOP_K), x2, gates, yb_rows)


def _routing_plan(expert_idx_t, n_tok):
    flat_e = jnp.transpose(expert_idx_t).reshape(n_tok * TOP_K)
    onehot = (flat_e[:, None] == jnp.arange(N_EXPERTS)[None, :]).astype(jnp.int32)
    csum = jnp.cumsum(onehot, axis=0)
    counts = csum[-1]
    rank = jnp.take_along_axis(csum, flat_e[:, None], axis=1)[:, 0] - 1
    padded = (counts + MOE_TB - 1) // MOE_TB * MOE_TB
    pad_end = jnp.cumsum(padded)
    pad_start = pad_end - padded
    slot = pad_start[flat_e] + rank
    n_blocks = -(-(n_tok * TOP_K) // MOE_TB) + N_EXPERTS
    n_slots = n_blocks * MOE_TB
    slot_token = jnp.zeros((n_slots,), jnp.int32).at[slot].set(
        (jnp.arange(n_tok * TOP_K) // TOP_K).astype(jnp.int32))
    block_expert = jnp.minimum(
        jnp.searchsorted(pad_end, jnp.arange(n_blocks) * MOE_TB, side='right'),
        N_EXPERTS - 1).astype(jnp.int32)
    n_used = (pad_end[-1] // MOE_TB).astype(jnp.int32).reshape(1)
    return slot.astype(jnp.int32), slot_token, block_expert, n_used


def _moe(x2, h2, expert_idx_t, gates_t, w_gate, w_up, w_down):
    n_tok = x2.shape[0]
    slot, slot_token, block_expert, n_used = _routing_plan(expert_idx_t, n_tok)
    xb = _gather_rows(slot_token, h2, x2.shape[1] // LANES)
    yb = _expert_blocks(block_expert, n_used, xb, w_gate, w_up, w_down)
    return _combine(slot, x2, jnp.transpose(gates_t), yb)


def _final_norm_kernel(x_ref, g_ref, o_ref):
    x = x_ref[...]
    o_ref[...] = x * lax.rsqrt(jnp.mean(x * x, axis=-1, keepdims=True) + RMS_EPS) * g_ref[...]


def _final_norm(x2, g):
    n, d = x2.shape
    tm = min(ROW_TILE, n)
    return pl.pallas_call(
        _final_norm_kernel,
        grid=(n // tm,),
        in_specs=[pl.BlockSpec((tm, d), lambda i: (i, 0)), pl.BlockSpec((1, d), lambda i: (0, 0))],
        out_specs=pl.BlockSpec((tm, d), lambda i: (i, 0)),
        out_shape=jax.ShapeDtypeStruct((n, d), F32),
        compiler_params=_cparams("parallel"),
        name="final_norm",
    )(x2, g)


def _layer_params(l, w_in, gm_b, rw_mu, rw_w0, rw_w2, rw_a0, rw_a2, rw_g2, rw_k_k, rw_k_a, rw_r_k,
                  rw_gn_w, rw_gn_b, fx_b_f, w_out, router_group_w, router_group_b,
                  router_expert_w, router_expert_b):
    d = w_in.shape[1]
    n_main = 3 * W_MIX + 2 * W_MIX + RW_COLS + 3 * W_MIX
    w = w_in[l]
    w_main = w[:, :n_main].astype(BF16)
    w_gate_t = jnp.zeros((SUBLANES, d), F32).at[:N_HEADS].set(w[:, n_main:n_main + N_HEADS].T)
    fx_bias = jnp.zeros((SUBLANES, 1), F32).at[:N_HEADS, 0].set(fx_b_f[l])
    wr = jnp.zeros((SUBLANES + N_EXPERTS, d), F32)
    wr = wr.at[:N_EXPERT_GROUPS].set(router_group_w[l].T).at[SUBLANES:].set(router_expert_w[l].T)
    br = jnp.zeros((SUBLANES + N_EXPERTS, 1), F32)
    br = br.at[:N_EXPERT_GROUPS, 0].set(router_group_b[l]).at[SUBLANES:, 0].set(router_expert_b[l])
    row = lambda a: a.reshape(1, -1)
    return dict(
        w_main=w_main, w_gate_t=w_gate_t.astype(BF16), fx_bias=fx_bias,
        gm_b=gm_b[l].T,
        mu=row(rw_mu[l]), w0=row(rw_w0[l]), w2=rw_w2[l].astype(BF16), a0=row(rw_a0[l]),
        a2=rw_a2[l].astype(BF16), g2=rw_g2[l].astype(BF16), k_k=row(rw_k_k[l]), k_a=row(rw_k_a[l]),
        r_k=row(rw_r_k[l]), gn_w=row(rw_gn_w[l]), gn_b=row(rw_gn_b[l]),
        w_out=w_out[l].astype(BF16), wr=wr, br=br,
    )


def kernel(x, norm1_g, w_in, gm_w_s, gm_b, rw_mu, rw_w0, rw_w2, rw_a0, rw_a2, rw_g2, rw_k_k, rw_k_a, rw_r_k, rw_gn_w, rw_gn_b, fx_b_f, w_out, norm2_g, router_group_w, router_group_b, router_expert_w, router_expert_b, exp_w_gate, exp_w_up, exp_w_down, final_norm_g):
    bsz, seq, d = x.shape
    depth = w_in.shape[0]
    assert seq % ATT_TQ == 0 and seq % RW_CHUNK == 0 and seq % (2 * LANES) == 0
    x2 = x.reshape(bsz * seq, d)
    hd = jnp.arange(W_MIX) // HEAD_DIM
    ones_bd = (hd[:, None] == hd[None, :]).astype(F32)
    for l in range(depth):
        lp = _layer_params(l, w_in, gm_b, rw_mu, rw_w0, rw_w2, rw_a0, rw_a2, rw_g2, rw_k_k, rw_k_a,
                           rw_r_k, rw_gn_w, rw_gn_b, fx_b_f, w_out, router_group_w, router_group_b,
                           router_expert_w, router_expert_b)
        p_sb, p_gm, p_rw, p_fx, grow = _norm_inproj(x2, norm1_g[l].reshape(1, d), lp["w_main"],
                                                    lp["w_gate_t"])
        y_sb = _sb_attention(p_sb, bsz, seq)
        y_gm = _gm_mix(p_gm, gm_w_s[l], lp["gm_b"])
        prep = _rw_prep(p_rw, seq, lp["mu"], lp["w0"], lp["w2"], lp["a0"], lp["a2"], lp["g2"],
                        lp["k_k"], lp["k_a"], lp["r_k"], ones_bd)
        y_rw = _rw_chunks(prep, lp["gn_w"], lp["gn_b"], bsz, seq)
        grow3 = jnp.transpose(grow.reshape(SUBLANES, bsz, seq), (1, 0, 2))
        cum_row = _fx_cum(grow3, lp["fx_bias"])
        y_fx = _fx_attention(p_fx, cum_row, bsz, seq)
        x2, h2, eidx_t, gates_t = _outproj_router(x2, (y_sb, y_gm, y_rw, y_fx), lp["w_out"],
                                                  norm2_g[l].reshape(1, d), lp["wr"], lp["br"])
        x2 = _moe(x2, h2, eidx_t, gates_t, exp_w_gate[l].astype(BF16), exp_w_up[l].astype(BF16),
                  exp_w_down[l].astype(BF16))
    return _final_norm(x2, final_norm_g.reshape(1, d)).reshape(bsz, seq, d)
```

```python
import functools

import jax
import jax.numpy as jnp
from jax import lax
from jax.experimental import pallas as pl
from jax.experimental.pallas import tpu as pltpu

F32 = jnp.float32
BF16 = jnp.bfloat16
HIGHEST = lax.Precision.HIGHEST

HEAD_DIM = 64
N_HEADS = 4
W_MIX = N_HEADS * HEAD_DIM
GM_CHUNK = 128
RW_DECAY_LORA = 64
RW_AAA_LORA = 64
RW_GATE_LORA = 128
RW_COLS = 3 * W_MIX + RW_DECAY_LORA + RW_AAA_LORA + RW_GATE_LORA
N_EXPERT_GROUPS = 4
EXPERTS_PER_GROUP = 8
N_EXPERTS = N_EXPERT_GROUPS * EXPERTS_PER_GROUP
TOP_K = 2
RMS_EPS = 1e-6
LN_EPS = 1e-5
GN_EPS = 64e-5
L2_EPS = 1e-12

V7X_VMEM_LIMIT_BYTES = 56 * 1024 * 1024
LANES = 128
SUBLANES = 8

ATT_TK = 128
ATT_TQB = 512
ATT_W = N_HEADS * LANES
RW_CHUNK = 64
RW_CPS = 2
RW_BATCH_GROUP = 4
ROW_TILE = 512
MOE_TB = 256
DMA_ISSUE_UNROLL = 8


def _cparams(*sem):
    return pltpu.CompilerParams(dimension_semantics=sem, vmem_limit_bytes=V7X_VMEM_LIMIT_BYTES)


def _dot(a, b, precision=None):
    return jnp.dot(a, b, preferred_element_type=F32, precision=precision)


def _dot_nt(a, b, precision=None):
    return lax.dot_general(a, b, (((1,), (1,)), ((), ())), preferred_element_type=F32,
                           precision=precision)


def _dot_tn(a, b, precision=None):
    return lax.dot_general(a, b, (((0,), (0,)), ((), ())), preferred_element_type=F32,
                           precision=precision)


def _iota2(shape, dim):
    return lax.broadcasted_iota(jnp.int32, shape, dim)


def _store_heads_padded(ref, col0, src, fill):
    low = _iota2((src.shape[0], LANES), 1) < HEAD_DIM
    for pair in range(N_HEADS // 2):
        slab = src[:, pair * LANES:(pair + 1) * LANES]
        swapped = pltpu.roll(slab, HEAD_DIM, axis=1)
        for odd, val in enumerate((slab, swapped)):
            c = col0 + (2 * pair + odd) * LANES
            ref[:, c:c + LANES] = jnp.where(low, val, fill).astype(ref.dtype)


def _norm_inproj_kernel(x_ref, g_ref, w_ref, wgt_ref, sb_ref, gm_ref, rw_ref, fx_ref, grow_ref):
    x = x_ref[...]
    h = x * lax.rsqrt(jnp.mean(x * x, axis=-1, keepdims=True) + RMS_EPS) * g_ref[...]
    hb = h.astype(BF16)
    p = _dot(hb, w_ref[...])
    scale = HEAD_DIM ** -0.5
    wp = N_HEADS * LANES
    o = 0
    _store_heads_padded(sb_ref, 0, p[:, o:o + W_MIX] * scale, 0.0)
    _store_heads_padded(sb_ref, wp, p[:, o + W_MIX:o + 2 * W_MIX], 0.0)
    _store_heads_padded(sb_ref, 2 * wp, p[:, o + 2 * W_MIX:o + 3 * W_MIX], 0.0)
    o += 3 * W_MIX
    gm_ref[...] = p[:, o:o + 2 * W_MIX]
    o += 2 * W_MIX
    rw_ref[...] = p[:, o:o + RW_COLS]
    o += RW_COLS
    _store_heads_padded(fx_ref, 0, p[:, o:o + W_MIX] * scale, 0.0)
    _store_heads_padded(fx_ref, wp, p[:, o + W_MIX:o + 2 * W_MIX], 0.0)
    _store_heads_padded(fx_ref, 2 * wp, p[:, o + 2 * W_MIX:o + 3 * W_MIX], 1.0)
    grow_ref[...] = _dot_nt(wgt_ref[...], hb)


def _norm_inproj(x2, g, w_main, w_gate_t):
    n, d = x2.shape
    tm = min(ROW_TILE, n)
    cols = w_main.shape[1]
    return pl.pallas_call(
        _norm_inproj_kernel,
        grid=(n // tm,),
        in_specs=[
            pl.BlockSpec((tm, d), lambda i: (i, 0)),
            pl.BlockSpec((1, d), lambda i: (0, 0)),
            pl.BlockSpec((d, cols), lambda i: (0, 0)),
            pl.BlockSpec((SUBLANES, d), lambda i: (0, 0)),
        ],
        out_specs=[
            pl.BlockSpec((tm, 3 * ATT_W), lambda i: (i, 0)),
            pl.BlockSpec((tm, 2 * W_MIX), lambda i: (i, 0)),
            pl.BlockSpec((tm, RW_COLS), lambda i: (i, 0)),
            pl.BlockSpec((tm, 3 * ATT_W), lambda i: (i, 0)),
            pl.BlockSpec((SUBLANES, tm), lambda i: (0, i)),
        ],
        out_shape=[
            jax.ShapeDtypeStruct((n, 3 * ATT_W), BF16),
            jax.ShapeDtypeStruct((n, 2 * W_MIX), F32),
            jax.ShapeDtypeStruct((n, RW_COLS), F32),
            jax.ShapeDtypeStruct((n, 3 * ATT_W), BF16),
            jax.ShapeDtypeStruct((SUBLANES, n), F32),
        ],
        compiler_params=_cparams("parallel"),
        name="norm_inproj",
    )(x2, g, w_main, w_gate_t)


def _sb_kernel(q_ref, k_ref, v_ref, tri_ref, o_ref, c_ref, acc_ref):
    qi = pl.program_id(1)
    tqb = q_ref.shape[1]
    hl = lambda h: slice(h * LANES, (h + 1) * LANES)

    def key_tile(j, off):
        r0 = 0 if off is None else off * ATT_TK
        rows = tqb - r0
        start = pl.multiple_of(j * ATT_TK, ATT_TK)
        if off is not None:
            strict = _iota2((rows, ATT_TK), 1) < _iota2((rows, ATT_TK), 0)
        zs, parts = [], []
        for h in range(N_HEADS):
            z = _dot_nt(q_ref[0, r0:, hl(h)], k_ref[0, pl.ds(start, ATT_TK), hl(h)])
            lom = -(jnp.maximum(z, 0.0) + jnp.log(1.0 + jnp.exp(-jnp.abs(z))))
            if off is not None:
                lom = jnp.where(strict, lom, 0.0)
            hi = lom.astype(BF16)
            lo = (lom - hi.astype(F32)).astype(BF16)
            zs.append(z)
            parts.append(jnp.concatenate([hi, lo], axis=1))
        rr = _dot(jnp.concatenate(parts, axis=0), tri_ref[...])
        for h in range(N_HEADS):
            rh = rr[h * rows:(h + 1) * rows]
            p = jnp.exp(zs[h] + rh[:, :ATT_TK] + c_ref[h, r0:, :])
            if off is not None:
                p = jnp.where(strict, p, 0.0)
            acc_ref[h, r0:, :] += _dot(p.astype(BF16), v_ref[0, pl.ds(start, ATT_TK), hl(h)])
            c_ref[h, r0:, :] += rh[:, ATT_TK:]

    c_ref[...] = jnp.zeros_like(c_ref)
    acc_ref[...] = jnp.zeros_like(acc_ref)
    n_diag = tqb // ATT_TK
    for off in reversed(range(n_diag)):
        key_tile(qi * n_diag + off, off)

    def body(jj, carry):
        key_tile(qi * n_diag - 1 - jj, None)
        return carry

    lax.fori_loop(0, qi * n_diag, body, 0)
    for h in range(N_HEADS):
        o_ref[0, :, h * HEAD_DIM:(h + 1) * HEAD_DIM] = acc_ref[h, :, :HEAD_DIM].astype(o_ref.dtype)


def _sb_attention(qkv, bsz, seq):
    qkv3 = qkv.reshape(bsz, seq, 3 * ATT_W)
    r = jnp.arange(ATT_TK)
    tri = (r[:, None] >= r[None, :]).astype(BF16)
    blk = jnp.concatenate([tri, jnp.ones((ATT_TK, ATT_TK), BF16)], axis=1)
    tri2 = jnp.concatenate([blk, blk], axis=0)
    tqb = min(ATT_TQB, seq)
    out = pl.pallas_call(
        _sb_kernel,
        grid=(bsz, seq // tqb),
        in_specs=[
            pl.BlockSpec((1, tqb, ATT_W), lambda b, i: (b, i, 0)),
            pl.BlockSpec((1, seq, ATT_W), lambda b, i: (b, 0, 1)),
            pl.BlockSpec((1, seq, ATT_W), lambda b, i: (b, 0, 2)),
            pl.BlockSpec((2 * ATT_TK, 2 * ATT_TK), lambda b, i: (0, 0)),
        ],
        out_specs=pl.BlockSpec((1, tqb, W_MIX), lambda b, i: (b, i, 0)),
        out_shape=jax.ShapeDtypeStruct((bsz, seq, W_MIX), BF16),
        scratch_shapes=[pltpu.VMEM((N_HEADS, tqb, ATT_TK), F32),
                        pltpu.VMEM((N_HEADS, tqb, LANES), F32)],
        compiler_params=_cparams("parallel", "parallel"),
        name="sb_attention",
    )(qkv3, qkv3, qkv3, tri2)
    return out.reshape(bsz * seq, W_MIX)


def _fx_cum_kernel(g_ref, b_ref, cum_ref):
    seq = g_ref.shape[2]
    blk = 2 * LANES
    r = _iota2((blk, blk), 0)
    c = _iota2((blk, blk), 1)
    tri = (r <= c).astype(F32)
    carry = jnp.zeros((SUBLANES, 1), F32)
    for s in range(seq // blk):
        x = g_ref[0, :, s * blk:(s + 1) * blk] + b_ref[...]
        lf = -(jnp.maximum(-x, 0.0) + jnp.log(1.0 + jnp.exp(-jnp.abs(x))))
        cs = _dot(lf, tri, precision=HIGHEST) + carry
        cum_ref[0, :, s * blk:(s + 1) * blk] = cs
        carry = cs[:, blk - 1:blk]


def _fx_cum(grow3, bias_col):
    bsz, _, seq = grow3.shape
    return pl.pallas_call(
        _fx_cum_kernel,
        grid=(bsz,),
        in_specs=[
            pl.BlockSpec((1, SUBLANES, seq), lambda b: (b, 0, 0)),
            pl.BlockSpec((SUBLANES, 1), lambda b: (0, 0)),
        ],
        out_specs=pl.BlockSpec((1, SUBLANES, seq), lambda b: (b, 0, 0)),
        out_shape=jax.ShapeDtypeStruct((bsz, SUBLANES, seq), F32),
        compiler_params=_cparams("parallel"),
        name="fx_cum",
    )(grow3, bias_col)


def _fx_kernel(q_ref, k_ref, v_ref, crow_ref, ccol_ref, o_ref, m_ref, acc_ref):
    qi = pl.program_id(1)
    tqb = q_ref.shape[1]
    hl = lambda h: slice(h * LANES, (h + 1) * LANES)

    def key_tile(j, off):
        r0 = 0 if off is None else off * ATT_TK
        rows = tqb - r0
        start = pl.multiple_of(j * ATT_TK, ATT_TK)
        if off is not None:
            causal = _iota2((rows, ATT_TK), 1) <= _iota2((rows, ATT_TK), 0)
        for h in range(N_HEADS):
            x = (_dot_nt(q_ref[0, r0:, hl(h)], k_ref[0, pl.ds(start, ATT_TK), hl(h)])
                 + ccol_ref[0, r0:, h:h + 1] - crow_ref[0, h:h + 1, pl.ds(start, ATT_TK)])
            if off is not None:
                x = jnp.where(causal, x, -jnp.inf)
            m_old = m_ref[h, r0:, :]
            m_new = jnp.maximum(m_old, jnp.max(x, axis=-1, keepdims=True))
            p = jnp.exp(x - m_new)
            acc_ref[h, r0:, :] = (jnp.exp(m_old - m_new) * acc_ref[h, r0:, :]
                                  + _dot(p.astype(BF16), v_ref[0, pl.ds(start, ATT_TK), hl(h)]))
            m_ref[h, r0:, :] = m_new

    m_ref[...] = jnp.full_like(m_ref, -jnp.inf)
    acc_ref[...] = jnp.zeros_like(acc_ref)
    n_diag = tqb // ATT_TK

    def body(j, carry):
        key_tile(j, None)
        return carry

    lax.fori_loop(0, qi * n_diag, body, 0)
    for off in range(n_diag):
        key_tile(qi * n_diag + off, off)
    for h in range(N_HEADS):
        acc = acc_ref[h]
        den = pltpu.roll(acc, HEAD_DIM, axis=1)
        o_ref[0, :, h * HEAD_DIM:(h + 1) * HEAD_DIM] = (acc / den)[:, :HEAD_DIM].astype(o_ref.dtype)


def _fx_attention(qkv, cum_row, bsz, seq):
    qkv3 = qkv.reshape(bsz, seq, 3 * ATT_W)
    cum_col = jnp.transpose(cum_row, (0, 2, 1))
    tqb = min(ATT_TQB, seq)
    out = pl.pallas_call(
        _fx_kernel,
        grid=(bsz, seq // tqb),
        in_specs=[
            pl.BlockSpec((1, tqb, ATT_W), lambda b, i: (b, i, 0)),
            pl.BlockSpec((1, seq, ATT_W), lambda b, i: (b, 0, 1)),
            pl.BlockSpec((1, seq, ATT_W), lambda b, i: (b, 0, 2)),
            pl.BlockSpec((1, SUBLANES, seq), lambda b, i: (b, 0, 0)),
            pl.BlockSpec((1, tqb, SUBLANES), lambda b, i: (b, i, 0)),
        ],
        out_specs=pl.BlockSpec((1, tqb, W_MIX), lambda b, i: (b, i, 0)),
        out_shape=jax.ShapeDtypeStruct((bsz, seq, W_MIX), BF16),
        scratch_shapes=[pltpu.VMEM((N_HEADS, tqb, LANES), F32),
                        pltpu.VMEM((N_HEADS, tqb, LANES), F32)],
        compiler_params=_cparams("parallel", "parallel"),
        name="fx_attention",
    )(qkv3, qkv3, qkv3, cum_row, cum_col)
    return out.reshape(bsz * seq, W_MIX)


def _gm_kernel(p_ref, w_ref, b_ref, o_ref):
    hid = jax.nn.gelu(p_ref[...])
    r = _iota2((GM_CHUNK, GM_CHUNK), 0)
    c = _iota2((GM_CHUNK, GM_CHUNK), 1)
    lower = c <= r
    for g in range(N_HEADS):
        u = hid[:, g * HEAD_DIM:(g + 1) * HEAD_DIM]
        v = hid[:, W_MIX + g * HEAD_DIM:W_MIX + (g + 1) * HEAD_DIM]
        mean = jnp.mean(v, axis=-1, keepdims=True)
        var = jnp.mean(jnp.square(v - mean), axis=-1, keepdims=True)
        vn = (v - mean) * lax.rsqrt(var + LN_EPS)
        w = jnp.where(lower, w_ref[g], 0.0).astype(BF16)
        mixed = _dot(w, vn.astype(BF16)) + b_ref[:, g:g + 1]
        o_ref[:, g * HEAD_DIM:(g + 1) * HEAD_DIM] = (u * mixed).astype(o_ref.dtype)


def _gm_mix(p_gm, w_s, b_col):
    n = p_gm.shape[0]
    return pl.pallas_call(
        _gm_kernel,
        grid=(n // GM_CHUNK,),
        in_specs=[
            pl.BlockSpec((GM_CHUNK, 2 * W_MIX), lambda i: (i, 0)),
            pl.BlockSpec((N_HEADS, GM_CHUNK, GM_CHUNK), lambda i: (0, 0, 0)),
            pl.BlockSpec((GM_CHUNK, N_HEADS), lambda i: (0, 0)),
        ],
        out_specs=pl.BlockSpec((GM_CHUNK, W_MIX), lambda i: (i, 0)),
        out_shape=jax.ShapeDtypeStruct((n, W_MIX), BF16),
        compiler_params=_cparams("parallel"),
        name="gm_mix",
    )(p_gm, w_s, b_col)


def _head_sum(x, ones_bd):
    return _dot(x, ones_bd, precision=HIGHEST)


def _rw_prep_kernel(p_ref, prev_ref, mu_ref, w0_ref, w2_ref, a0_ref, a2_ref, g2_ref,
                    kk_ref, ka_ref, rk_ref, bd_ref,
                    r_out, ld_out, k_out, v_out, kk_out, ba_out, bonus_out, g_out, *, seq):
    i = pl.program_id(0)
    tm = p_ref.shape[0]
    p = p_ref[...]
    first = (i * tm) % seq == 0
    prev_row = jnp.where(first, 0.0, prev_ref[SUBLANES - 1:SUBLANES, :])
    rows = _iota2(p.shape, 0)
    prev = jnp.where(rows == 0, prev_row, pltpu.roll(p, 1, axis=0))
    p = p + (prev - p) * mu_ref[...]
    r = p[:, 0:W_MIX]
    k = p[:, W_MIX:2 * W_MIX]
    v = p[:, 2 * W_MIX:3 * W_MIX]
    o = 3 * W_MIX
    xw = p[:, o:o + RW_DECAY_LORA]
    o += RW_DECAY_LORA
    xa = p[:, o:o + RW_AAA_LORA]
    o += RW_AAA_LORA
    xg = p[:, o:o + RW_GATE_LORA]
    wpre = -(w0_ref[...] + _dot(jnp.tanh(xw).astype(BF16), w2_ref[...]))
    w = -(jnp.maximum(wpre, 0.0) + jnp.log(1.0 + jnp.exp(-jnp.abs(wpre)))) - 0.5
    a = jax.nn.sigmoid(a0_ref[...] + _dot(xa.astype(BF16), a2_ref[...]))
    g = _dot(jax.nn.sigmoid(xg).astype(BF16), g2_ref[...])
    kk = k * kk_ref[...]
    nrm = jnp.maximum(jnp.sqrt(_head_sum(kk * kk, bd_ref[...])), L2_EPS)
    kk = kk / nrm
    k2 = k * (1.0 + (a - 1.0) * ka_ref[...])
    r_out[...] = r
    ld_out[...] = -jnp.exp(w)
    k_out[...] = k2
    v_out[...] = v
    kk_out[...] = kk
    ba_out[...] = kk * a
    bonus_out[...] = _head_sum(r * k2 * rk_ref[...], bd_ref[...]) * v
    g_out[...] = g


def _rw_prep(p_rw, seq, mu, w0, w2, a0, a2, g2, k_k, k_a, r_k, ones_bd):
    n = p_rw.shape[0]
    tm = min(256, seq)
    vec = lambda width: pl.BlockSpec((1, width), lambda i: (0, 0))
    mat = lambda rws: pl.BlockSpec((rws, W_MIX), lambda i: (0, 0))
    out_spec = pl.BlockSpec((tm, W_MIX), lambda i: (i, 0))
    return pl.pallas_call(
        functools.partial(_rw_prep_kernel, seq=seq),
        grid=(n // tm,),
        in_specs=[
            pl.BlockSpec((tm, RW_COLS), lambda i: (i, 0)),
            pl.BlockSpec((SUBLANES, RW_COLS),
                         lambda i: (jnp.maximum(i * (tm // SUBLANES) - 1, 0), 0)),
            vec(RW_COLS), vec(W_MIX), mat(RW_DECAY_LORA), vec(W_MIX), mat(RW_AAA_LORA),
            mat(RW_GATE_LORA), vec(W_MIX), vec(W_MIX), vec(W_MIX), mat(W_MIX),
        ],
        out_specs=[out_spec] * 8,
        out_shape=[jax.ShapeDtypeStruct((n, W_MIX), F32)] * 8,
        compiler_params=_cparams("parallel"),
        name="rw_prep",
    )(p_rw, p_rw, mu, w0, w2, a0, a2, g2, k_k, k_a, r_k, ones_bd)


_NN = (((1,), (0,)), ((), ()))
_NT = (((1,), (1,)), ((), ()))


def _split2(x):
    hi = x.astype(BF16)
    return hi, (x - hi.astype(F32)).astype(BF16)


def _mm_lhs2(a, b, dims):
    m = a.shape[0]
    hi, lo = _split2(a)
    r = lax.dot_general(jnp.concatenate([hi, lo], axis=0), b, dims, preferred_element_type=F32)
    return r[:m] + r[m:]


def _lane_head(shape, width):
    return lax.shift_right_logical(_iota2(shape, 1), width.bit_length() - 1)


def _head_blocks(x, width):
    lh = _lane_head(x.shape, width)
    return jnp.concatenate([jnp.where(lh == h, x, jnp.zeros_like(x)) for h in range(N_HEADS)],
                           axis=0)


def _head_diag(full, width):
    rows = full.shape[0] // N_HEADS
    lh = _lane_head((rows, full.shape[1]), width)
    out = jnp.zeros((rows, full.shape[1]), F32)
    for h in range(N_HEADS):
        out = out + jnp.where(lh == h, full[h * rows:(h + 1) * rows], 0.0)
    return out


def _tn_lhs2(a, b_bf16):
    hi = a.astype(BF16).astype(F32)
    b = b_bf16.astype(F32)
    return _dot_tn(hi, b) + _dot_tn(a - hi, b)


def _rw_intra_kernel(r_ref, ld_ref, k_ref, v_ref, kk_ref, ba_ref,
                     w_out, u0_out, y0_out, rt_out, btp_out, mrb_out, g0_out, pc_out):
    cs = RW_CHUNK
    cw = N_HEADS * cs
    row = _iota2((cs, cw), 0)
    colc = jnp.bitwise_and(_iota2((cs, cw), 1), cs - 1)
    lower = colc <= row
    strict = colc < row
    eye = (colc == row).astype(F32)
    tri3 = (jnp.bitwise_and(_iota2((cs, 3 * cs), 1), cs - 1) <= _iota2((cs, 3 * cs), 0)).astype(BF16)

    for c in range(r_ref.shape[1] // cs):
        rows = slice(c * cs, (c + 1) * cs)
        ld = ld_ref[0, rows, :]
        l1 = ld.astype(BF16)
        rem = ld - l1.astype(F32)
        l2, l3 = _split2(rem)
        cl = _dot(tri3, jnp.concatenate([l1, l2, l3], axis=0))
        p_in = jnp.exp(cl)
        p_inv = jnp.exp(-cl)
        at = -kk_ref[0, rows, :] * jnp.exp(cl - ld)
        bt = ba_ref[0, rows, :] * p_inv
        kt = k_ref[0, rows, :] * p_inv
        rt = r_ref[0, rows, :] * p_in
        v = v_ref[0, rows, :]
        pc = p_in[cs - 1:cs, :]

        lhs = jnp.concatenate([at, rt], axis=0)
        ab_mb = _mm_lhs2(lhs, _head_blocks(bt.astype(BF16), HEAD_DIM), _NT)
        ak_mk = _mm_lhs2(lhs, _head_blocks(kt.astype(BF16), HEAD_DIM), _NT)
        a_ab = jnp.where(strict, ab_mb[:cs], 0.0)
        m_rb = jnp.where(lower, ab_mb[cs:], 0.0)
        a_ak = jnp.where(strict, ak_mk[:cs], 0.0)
        m_rk = jnp.where(lower, ak_mk[cs:], 0.0)

        inv = eye + a_ab
        pw = a_ab
        pw_blocks = _head_blocks(pw.astype(BF16), cs)
        steps = 1
        while steps * 2 < cs:
            pw = _mm_lhs2(pw, pw_blocks, _NN)
            pw_blocks = _head_blocks(pw.astype(BF16), cs)
            inv = inv + _mm_lhs2(inv, pw_blocks, _NN)
            steps *= 2

        akv_mkv = _mm_lhs2(jnp.concatenate([a_ak, m_rk], axis=0),
                           _head_blocks(v.astype(BF16), HEAD_DIM), _NN)
        x2 = akv_mkv[:cs]
        wu = _mm_lhs2(inv, jnp.concatenate([_head_blocks(at.astype(BF16), HEAD_DIM),
                                            _head_blocks(x2.astype(BF16), HEAD_DIM)], axis=1), _NN)
        w_out[0, rows, :] = wu[:, :W_MIX]
        u0_out[0, rows, :] = wu[:, W_MIX:]
        y0_out[0, rows, :] = akv_mkv[cs:]
        rt_out[0, rows, :] = rt
        btp_out[0, rows, :] = bt * pc
        mrb_out[0, rows, :] = m_rb
        g0_out[0, rows, :] = _head_diag(_tn_lhs2(v, kt.astype(BF16)), HEAD_DIM) * pc
        pc_out[0, c * SUBLANES:(c + 1) * SUBLANES, :] = jnp.broadcast_to(pc, (SUBLANES, W_MIX))


def _rw_intra(prep6, bsz, seq):
    arrs = [a.reshape(bsz, seq, W_MIX) for a in prep6]
    rows = RW_CHUNK * RW_CPS
    spec = pl.BlockSpec((1, rows, W_MIX), lambda b, c: (b, c, 0))
    pc_spec = pl.BlockSpec((1, SUBLANES * RW_CPS, W_MIX), lambda b, c: (b, c, 0))
    big = jax.ShapeDtypeStruct((bsz, seq, W_MIX), F32)
    return pl.pallas_call(
        _rw_intra_kernel,
        grid=(bsz, seq // rows),
        in_specs=[spec] * 6,
        out_specs=[spec] * 7 + [pc_spec],
        out_shape=[big] * 7 + [jax.ShapeDtypeStruct((bsz, seq // RW_CHUNK * SUBLANES, W_MIX), F32)],
        compiler_params=_cparams("parallel", "parallel"),
        name="rw_intra",
    )(*arrs)


def _rw_state_kernel(w_ref, u0_ref, y0_ref, rt_ref, btp_ref, mrb_ref, g0_ref, pc_ref,
                     bonus_ref, g_ref, gnw_ref, gnb_ref, ones_ref, o_ref, state_ref):
    cs = RW_CHUNK

    @pl.when(pl.program_id(1) == 0)
    def _():
        state_ref[...] = jnp.zeros_like(state_ref)

    for b in range(w_ref.shape[0]):
        s0 = state_ref[b]
        wr = _mm_lhs2(jnp.concatenate([w_ref[b], rt_ref[b]], axis=0),
                      _head_blocks(s0.astype(BF16), HEAD_DIM), _NT)
        u = wr[:cs] + u0_ref[b]
        y = wr[cs:] + _mm_lhs2(mrb_ref[b], _head_blocks(u.astype(BF16), HEAD_DIM), _NN) + y0_ref[b]
        state_ref[b] = (s0 * pc_ref[b, 0:1, :] + g0_ref[b]
                        + _head_diag(_tn_lhs2(u, btp_ref[b].astype(BF16)), HEAD_DIM))

        inv_n = 1.0 / HEAD_DIM
        mean = _mm_lhs2(y, ones_ref[...], _NN) * inv_n
        yc = y - mean
        var = _mm_lhs2(yc * yc, ones_ref[...], _NN) * inv_n
        yn = yc * lax.rsqrt(var + GN_EPS) * gnw_ref[...] + gnb_ref[...]
        o_ref[b] = ((yn + bonus_ref[b]) * g_ref[b]).astype(o_ref.dtype)


def _rw_state(intra, bonus, g, gn_w, gn_b, ones_bd, bsz, seq):
    gb = RW_BATCH_GROUP if bsz % RW_BATCH_GROUP == 0 else 1
    spec = pl.BlockSpec((gb, RW_CHUNK, W_MIX), lambda b, c: (b, c, 0))
    pc_spec = pl.BlockSpec((gb, SUBLANES, W_MIX), lambda b, c: (b, c, 0))
    vec = pl.BlockSpec((1, W_MIX), lambda b, c: (0, 0))
    out = pl.pallas_call(
        _rw_state_kernel,
        grid=(bsz // gb, seq // RW_CHUNK),
        in_specs=[spec] * 7 + [pc_spec, spec, spec, vec, vec,
                               pl.BlockSpec((W_MIX, W_MIX), lambda b, c: (0, 0))],
        out_specs=spec,
        out_shape=jax.ShapeDtypeStruct((bsz, seq, W_MIX), BF16),
        scratch_shapes=[pltpu.VMEM((gb, HEAD_DIM, W_MIX), F32)],
        compiler_params=_cparams("parallel", "arbitrary"),
        name="rw_state",
    )(*intra, bonus.reshape(bsz, seq, W_MIX), g.reshape(bsz, seq, W_MIX), gn_w, gn_b,
      ones_bd.astype(BF16))
    return out.reshape(bsz * seq, W_MIX)


def _store_token_rows(ref, val):
    tm, d = val.shape
    ch = d // LANES
    for s in range(ch):
        ref[pl.ds(s, tm, stride=ch), :] = val[:, s * LANES:(s + 1) * LANES]


def _load_token_rows(ref, tm, ch):
    return jnp.concatenate([ref[pl.ds(s, tm, stride=ch), :] for s in range(ch)], axis=1)


def _outproj_router_kernel(x_ref, sb_ref, gm_ref, rw_ref, fx_ref, wo_ref, g_ref, wr_ref, br_ref,
                           x_out, h_out, idx_out, gate_out):
    mix = jnp.concatenate([sb_ref[...], gm_ref[...], rw_ref[...], fx_ref[...]], axis=1)
    x = x_ref[...] + _dot(mix, wo_ref[...])
    x_out[...] = x
    h = x * lax.rsqrt(jnp.mean(x * x, axis=-1, keepdims=True) + RMS_EPS) * g_ref[...]
    _store_token_rows(h_out, h)

    lg = _dot_nt(wr_ref[...], h, HIGHEST) + br_ref[...]
    tm = lg.shape[1]
    gl = [lg[g:g + 1, :] for g in range(N_EXPERT_GROUPS)]
    gmax = gl[0]
    gsel = jnp.zeros((1, tm), jnp.int32)
    for g in range(1, N_EXPERT_GROUPS):
        better = gl[g] > gmax
        gsel = jnp.where(better, g, gsel)
        gmax = jnp.where(better, gl[g], gmax)
    denom = gl[0] * 0.0
    for g in range(N_EXPERT_GROUPS):
        denom = denom + jnp.exp(gl[g] - gmax)
    g_gate = 1.0 / denom

    e0 = SUBLANES
    ing = lg[e0:e0 + EXPERTS_PER_GROUP, :]
    for g in range(1, N_EXPERT_GROUPS):
        ing = jnp.where(gsel == g, lg[e0 + g * EXPERTS_PER_GROUP:e0 + (g + 1) * EXPERTS_PER_GROUP, :], ing)
    ridx = _iota2(ing.shape, 0)
    m1 = jnp.max(ing, axis=0, keepdims=True)
    i1 = jnp.min(jnp.where(ing == m1, ridx, EXPERTS_PER_GROUP), axis=0, keepdims=True)
    rest = jnp.where(ridx == i1, -jnp.inf, ing)
    m2 = jnp.max(rest, axis=0, keepdims=True)
    i2 = jnp.min(jnp.where(rest == m2, ridx, EXPERTS_PER_GROUP), axis=0, keepdims=True)
    e2 = jnp.exp(m2 - m1)
    s = 1.0 + e2
    idx_out[0:1, :] = gsel * EXPERTS_PER_GROUP + i1
    idx_out[1:2, :] = gsel * EXPERTS_PER_GROUP + i2
    gate_out[0:1, :] = (1.0 / s) * g_gate
    gate_out[1:2, :] = (e2 / s) * g_gate


def _outproj_router(x2, ys, w_out, g2, w_router_t, b_router):
    n, d = x2.shape
    tm = min(ROW_TILE, n)
    nr = w_router_t.shape[0]
    ymix = pl.BlockSpec((tm, W_MIX), lambda i: (i, 0))
    return pl.pallas_call(
        _outproj_router_kernel,
        grid=(n // tm,),
        in_specs=[
            pl.BlockSpec((tm, d), lambda i: (i, 0)),
            ymix, ymix, ymix, ymix,
            pl.BlockSpec((4 * W_MIX, d), lambda i: (0, 0)),
            pl.BlockSpec((1, d), lambda i: (0, 0)),
            pl.BlockSpec((nr, d), lambda i: (0, 0)),
            pl.BlockSpec((nr, 1), lambda i: (0, 0)),
        ],
        out_specs=[
            pl.BlockSpec((tm, d), lambda i: (i, 0)),
            pl.BlockSpec((tm * (d // LANES), LANES), lambda i: (i, 0)),
            pl.BlockSpec((TOP_K, tm), lambda i: (0, i)),
            pl.BlockSpec((TOP_K, tm), lambda i: (0, i)),
        ],
        out_shape=[
            jax.ShapeDtypeStruct((n, d), F32),
            jax.ShapeDtypeStruct((n * (d // LANES), LANES), F32),
            jax.ShapeDtypeStruct((TOP_K, n), jnp.int32),
            jax.ShapeDtypeStruct((TOP_K, n), F32),
        ],
        compiler_params=_cparams("parallel"),
        name="outproj_router",
    )(x2, *ys, w_out, g2, w_router_t, b_router)


def _token_copy(src_hbm, dst, src_tok, dst_tok, sem, ch):
    return pltpu.make_async_copy(
        src_hbm.at[pl.ds(pl.multiple_of(src_tok * ch, ch), ch)],
        dst.at[pl.ds(pl.multiple_of(dst_tok * ch, ch), ch)], sem)


def _gather_rows_kernel(tok_ref, h_hbm, o_ref, sem, *, ch):
    tb = o_ref.shape[0] // ch

    def start(r, _):
        _token_copy(h_hbm, o_ref, tok_ref[0, 0, r], r, sem, ch).start()
        return 0

    lax.fori_loop(0, tb, start, 0, unroll=DMA_ISSUE_UNROLL)
    pltpu.make_async_copy(h_hbm.at[pl.ds(0, tb * ch)], o_ref, sem).wait()


def _gather_rows(slot_token, h_rows, ch):
    n_slots = slot_token.shape[0]
    n_blocks = n_slots // MOE_TB
    return pl.pallas_call(
        functools.partial(_gather_rows_kernel, ch=ch),
        grid=(n_blocks,),
        in_specs=[
            pl.BlockSpec((1, 1, MOE_TB), lambda i: (i, 0, 0), memory_space=pltpu.SMEM),
            pl.BlockSpec(memory_space=pl.ANY),
        ],
        out_specs=pl.BlockSpec((MOE_TB * ch, LANES), lambda i: (i, 0)),
        scratch_shapes=[pltpu.SemaphoreType.DMA],
        out_shape=jax.ShapeDtypeStruct((n_slots * ch, LANES), h_rows.dtype),
        compiler_params=_cparams("arbitrary"),
        name="moe_gather",
    )(slot_token.reshape(n_blocks, 1, MOE_TB), h_rows)


def _expert_kernel(be_ref, nb_ref, x_ref, wg_ref, wu_ref, wd_ref, o_ref, *, ch):
    i = pl.program_id(0)

    @pl.when(i < nb_ref[0])
    def _():
        x = _load_token_rows(x_ref, MOE_TB, ch).astype(BF16)
        a = _dot(x, wg_ref[0])
        u = _dot(x, wu_ref[0])
        hid = (a * jax.nn.sigmoid(a) * u).astype(BF16)
        _store_token_rows(o_ref, _dot(hid, wd_ref[0]))

    @pl.when(i >= nb_ref[0])
    def _():
        o_ref[...] = jnp.zeros_like(o_ref)


def _expert_blocks(block_expert, n_used, xb_rows, w_gate, w_up, w_down):
    d, hid = w_gate.shape[1], w_gate.shape[2]
    ch = d // LANES
    n_slots = xb_rows.shape[0] // ch
    return pl.pallas_call(
        functools.partial(_expert_kernel, ch=ch),
        grid_spec=pltpu.PrefetchScalarGridSpec(
            num_scalar_prefetch=2,
            grid=(n_slots // MOE_TB,),
            in_specs=[
                pl.BlockSpec((MOE_TB * ch, LANES), lambda i, be, nb: (i, 0)),
                pl.BlockSpec((1, d, hid), lambda i, be, nb: (be[i], 0, 0)),
                pl.BlockSpec((1, d, hid), lambda i, be, nb: (be[i], 0, 0)),
                pl.BlockSpec((1, hid, d), lambda i, be, nb: (be[i], 0, 0)),
            ],
            out_specs=pl.BlockSpec((MOE_TB * ch, LANES), lambda i, be, nb: (i, 0)),
        ),
        out_shape=jax.ShapeDtypeStruct((n_slots * ch, LANES), F32),
        compiler_params=_cparams("arbitrary"),
        name="moe_experts",
    )(block_expert, n_used, xb_rows, w_gate, w_up, w_down)


def _combine_kernel(slot_ref, x_ref, gate_ref, yb_hbm, o_ref, buf, sem, *, ch):
    tm = x_ref.shape[0]

    def start(r, _):
        for k in range(TOP_K):
            _token_copy(yb_hbm, buf.at[k], slot_ref[0, 0, r * TOP_K + k], r, sem, ch).start()
        return 0

    lax.fori_loop(0, tm, start, 0, unroll=DMA_ISSUE_UNROLL)
    for k in range(TOP_K):
        pltpu.make_async_copy(yb_hbm.at[pl.ds(0, tm * ch)], buf.at[k], sem).wait()
    y = (_load_token_rows(buf.at[0], tm, ch) * gate_ref[:, 0:1]
         + _load_token_rows(buf.at[1], tm, ch) * gate_ref[:, 1:2])
    o_ref[...] = x_ref[...] + y


def _combine(slot_flat, x2, gates, yb_rows):
    n, d = x2.shape
    ch = d // LANES
    tm = min(256, n)
    return pl.pallas_call(
        functools.partial(_combine_kernel, ch=ch),
        grid=(n // tm,),
        in_specs=[
            pl.BlockSpec((1, 1, tm * TOP_K), lambda i: (i, 0, 0), memory_space=pltpu.SMEM),
            pl.BlockSpec((tm, d), lambda i: (i, 0)),
            pl.BlockSpec((tm, TOP_K), lambda i: (i, 0)),
            pl.BlockSpec(memory_space=pl.ANY),
        ],
        out_specs=pl.BlockSpec((tm, d), lambda i: (i, 0)),
        scratch_shapes=[pltpu.VMEM((TOP_K, tm * ch, LANES), F32), pltpu.SemaphoreType.DMA],
        out_shape=jax.ShapeDtypeStruct((n, d), F32),
        compiler_params=_cparams("arbitrary"),
        name="moe_combine",
    )(slot_flat.reshape(n // tm, 1, tm * TOP_K), x2, gates, yb_rows)


def _routing_plan(expert_idx_t, n_tok):
    flat_e = jnp.transpose(expert_idx_t).reshape(n_tok * TOP_K)
    onehot = (flat_e[:, None] == jnp.arange(N_EXPERTS)[None, :]).astype(jnp.int32)
    csum = jnp.cumsum(onehot, axis=0)
    counts = csum[-1]
    rank = jnp.take_along_axis(csum, flat_e[:, None], axis=1)[:, 0] - 1
    padded = (counts + MOE_TB - 1) // MOE_TB * MOE_TB
    pad_end = jnp.cumsum(padded)
    pad_start = pad_end - padded
    slot = pad_start[flat_e] + rank
    n_blocks = -(-(n_tok * TOP_K) // MOE_TB) + N_EXPERTS
    n_slots = n_blocks * MOE_TB
    slot_token = jnp.zeros((n_slots,), jnp.int32).at[slot].set(
        (jnp.arange(n_tok * TOP_K) // TOP_K).astype(jnp.int32))
    block_expert = jnp.minimum(
        jnp.searchsorted(pad_end, jnp.arange(n_blocks) * MOE_TB, side='right'),
        N_EXPERTS - 1).astype(jnp.int32)
    n_used = (pad_end[-1] // MOE_TB).astype(jnp.int32).reshape(1)
    return slot.astype(jnp.int32), slot_token, block_expert, n_used


def _moe(x2, h2, expert_idx_t, gates_t, w_gate, w_up, w_down):
    n_tok = x2.shape[0]
    slot, slot_token, block_expert, n_used = _routing_plan(expert_idx_t, n_tok)
    xb = _gather_rows(slot_token, h2, x2.shape[1] // LANES)
    yb = _expert_blocks(block_expert, n_used, xb, w_gate, w_up, w_down)
    return _combine(slot, x2, jnp.transpose(gates_t), yb)


def _final_norm_kernel(x_ref, g_ref, o_ref):
    x = x_ref[...]
    o_ref[...] = x * lax.rsqrt(jnp.mean(x * x, axis=-1, keepdims=True) + RMS_EPS) * g_ref[...]


def _final_norm(x2, g):
    n, d = x2.shape
    tm = min(ROW_TILE, n)
    return pl.pallas_call(
        _final_norm_kernel,
        grid=(n // tm,),
        in_specs=[pl.BlockSpec((tm, d), lambda i: (i, 0)), pl.BlockSpec((1, d), lambda i: (0, 0))],
        out_specs=pl.BlockSpec((tm, d), lambda i: (i, 0)),
        out_shape=jax.ShapeDtypeStruct((n, d), F32),
        compiler_params=_cparams("parallel"),
        name="final_norm",
    )(x2, g)


def _layer_params(l, w_in, gm_b, rw_mu, rw_w0, rw_w2, rw_a0, rw_a2, rw_g2, rw_k_k, rw_k_a, rw_r_k,
                  rw_gn_w, rw_gn_b, fx_b_f, w_out, router_group_w, router_group_b,
                  router_expert_w, router_expert_b):
    d = w_in.shape[1]
    n_main = 3 * W_MIX + 2 * W_MIX + RW_COLS + 3 * W_MIX
    w = w_in[l]
    w_main = w[:, :n_main].astype(BF16)
    w_gate_t = jnp.zeros((SUBLANES, d), F32).at[:N_HEADS].set(w[:, n_main:n_main + N_HEADS].T)
    fx_bias = jnp.zeros((SUBLANES, 1), F32).at[:N_HEADS, 0].set(fx_b_f[l])
    wr = jnp.zeros((SUBLANES + N_EXPERTS, d), F32)
    wr = wr.at[:N_EXPERT_GROUPS].set(router_group_w[l].T).at[SUBLANES:].set(router_expert_w[l].T)
    br = jnp.zeros((SUBLANES + N_EXPERTS, 1), F32)
    br = br.at[:N_EXPERT_GROUPS, 0].set(router_group_b[l]).at[SUBLANES:, 0].set(router_expert_b[l])
    row = lambda a: a.reshape(1, -1)
    return dict(
        w_main=w_main, w_gate_t=w_gate_t.astype(BF16), fx_bias=fx_bias,
        gm_b=gm_b[l].T,
        mu=row(rw_mu[l]), w0=row(rw_w0[l]), w2=rw_w2[l].astype(BF16), a0=row(rw_a0[l]),
        a2=rw_a2[l].astype(BF16), g2=rw_g2[l].astype(BF16), k_k=row(rw_k_k[l]), k_a=row(rw_k_a[l]),
        r_k=row(rw_r_k[l]), gn_w=row(rw_gn_w[l]), gn_b=row(rw_gn_b[l]),
        w_out=w_out[l].astype(BF16), wr=wr, br=br,
    )


def kernel(x, norm1_g, w_in, gm_w_s, gm_b, rw_mu, rw_w0, rw_w2, rw_a0, rw_a2, rw_g2, rw_k_k, rw_k_a, rw_r_k, rw_gn_w, rw_gn_b, fx_b_f, w_out, norm2_g, router_group_w, router_group_b, router_expert_w, router_expert_b, exp_w_gate, exp_w_up, exp_w_down, final_norm_g):
    bsz, seq, d = x.shape
    depth = w_in.shape[0]
    assert seq % min(ATT_TQB, seq) == 0 and seq % RW_CHUNK == 0 and seq % (2 * LANES) == 0
    x2 = x.reshape(bsz * seq, d)
    hd = jnp.arange(W_MIX) // HEAD_DIM
    ones_bd = (hd[:, None] == hd[None, :]).astype(F32)
    for l in range(depth):
        lp = _layer_params(l, w_in, gm_b, rw_mu, rw_w0, rw_w2, rw_a0, rw_a2, rw_g2, rw_k_k, rw_k_a,
                           rw_r_k, rw_gn_w, rw_gn_b, fx_b_f, w_out, router_group_w, router_group_b,
                           router_expert_w, router_expert_b)
        p_sb, p_gm, p_rw, p_fx, grow = _norm_inproj(x2, norm1_g[l].reshape(1, d), lp["w_main"],
                                                    lp["w_gate_t"])
        y_sb = _sb_attention(p_sb, bsz, seq)
        y_gm = _gm_mix(p_gm, gm_w_s[l], lp["gm_b"])
        prep = _rw_prep(p_rw, seq, lp["mu"], lp["w0"], lp["w2"], lp["a0"], lp["a2"], lp["g2"],
                        lp["k_k"], lp["k_a"], lp["r_k"], ones_bd)
        intra = _rw_intra(prep[:6], bsz, seq)
        y_rw = _rw_state(intra, prep[6], prep[7], lp["gn_w"], lp["gn_b"], ones_bd, bsz, seq)
        grow3 = jnp.transpose(grow.reshape(SUBLANES, bsz, seq), (1, 0, 2))
        cum_row = _fx_cum(grow3, lp["fx_bias"])
        y_fx = _fx_attention(p_fx, cum_row, bsz, seq)
        x2, h2, eidx_t, gates_t = _outproj_router(x2, (y_sb, y_gm, y_rw, y_fx), lp["w_out"],
                                                  norm2_g[l].reshape(1, d), lp["wr"], lp["br"])
        x2 = _moe(x2, h2, eidx_t, gates_t, exp_w_gate[l].astype(BF16), exp_w_up[l].astype(BF16),
                  exp_w_down[l].astype(BF16))
    return _final_norm(x2, final_norm_g.reshape(1, d)).reshape(bsz, seq, d)
```

```python
import functools

import jax
import jax.numpy as jnp
from jax import lax
from jax.experimental import pallas as pl
from jax.experimental.pallas import tpu as pltpu

F32 = jnp.float32
BF16 = jnp.bfloat16
HIGHEST = lax.Precision.HIGHEST

HEAD_DIM = 64
N_HEADS = 4
W_MIX = N_HEADS * HEAD_DIM
GM_CHUNK = 128
RW_DECAY_LORA = 64
RW_AAA_LORA = 64
RW_GATE_LORA = 128
RW_COLS = 3 * W_MIX + RW_DECAY_LORA + RW_AAA_LORA + RW_GATE_LORA
N_EXPERT_GROUPS = 4
EXPERTS_PER_GROUP = 8
N_EXPERTS = N_EXPERT_GROUPS * EXPERTS_PER_GROUP
TOP_K = 2
RMS_EPS = 1e-6
LN_EPS = 1e-5
GN_EPS = 64e-5
L2_EPS = 1e-12

V7X_VMEM_LIMIT_BYTES = 56 * 1024 * 1024
LANES = 128
SUBLANES = 8

ATT_TK = 128
ATT_TQB = 512
ATT_W = N_HEADS * LANES
RW_CHUNK = 64
RW_CPS = 4
RW_BATCH_GROUP = 4
ROW_TILE = 512
MOE_TB = 256
DMA_ISSUE_UNROLL = 8


def _cparams(*sem):
    return pltpu.CompilerParams(dimension_semantics=sem, vmem_limit_bytes=V7X_VMEM_LIMIT_BYTES)


def _dot(a, b, precision=None):
    return jnp.dot(a, b, preferred_element_type=F32, precision=precision)


def _dot_nt(a, b, precision=None):
    return lax.dot_general(a, b, (((1,), (1,)), ((), ())), preferred_element_type=F32,
                           precision=precision)


def _dot_tn(a, b, precision=None):
    return lax.dot_general(a, b, (((0,), (0,)), ((), ())), preferred_element_type=F32,
                           precision=precision)


def _iota2(shape, dim):
    return lax.broadcasted_iota(jnp.int32, shape, dim)


def _store_heads_padded(ref, col0, src, fill):
    low = _iota2((src.shape[0], LANES), 1) < HEAD_DIM
    for pair in range(N_HEADS // 2):
        slab = src[:, pair * LANES:(pair + 1) * LANES]
        swapped = pltpu.roll(slab, HEAD_DIM, axis=1)
        for odd, val in enumerate((slab, swapped)):
            c = col0 + (2 * pair + odd) * LANES
            ref[:, c:c + LANES] = jnp.where(low, val, fill).astype(ref.dtype)


def _norm_inproj_kernel(x_ref, g_ref, w_ref, wgt_ref, sb_ref, gm_ref, rw_ref, fx_ref, grow_ref):
    x = x_ref[...]
    h = x * lax.rsqrt(jnp.mean(x * x, axis=-1, keepdims=True) + RMS_EPS) * g_ref[...]
    hb = h.astype(BF16)
    p = _dot(hb, w_ref[...])
    scale = HEAD_DIM ** -0.5
    wp = N_HEADS * LANES
    o = 0
    _store_heads_padded(sb_ref, 0, p[:, o:o + W_MIX] * scale, 0.0)
    _store_heads_padded(sb_ref, wp, p[:, o + W_MIX:o + 2 * W_MIX], 0.0)
    _store_heads_padded(sb_ref, 2 * wp, p[:, o + 2 * W_MIX:o + 3 * W_MIX], 0.0)
    o += 3 * W_MIX
    gm_ref[...] = p[:, o:o + 2 * W_MIX]
    o += 2 * W_MIX
    rw_ref[...] = p[:, o:o + RW_COLS]
    o += RW_COLS
    _store_heads_padded(fx_ref, 0, p[:, o:o + W_MIX] * scale, 0.0)
    _store_heads_padded(fx_ref, wp, p[:, o + W_MIX:o + 2 * W_MIX], 0.0)
    _store_heads_padded(fx_ref, 2 * wp, p[:, o + 2 * W_MIX:o + 3 * W_MIX], 1.0)
    grow_ref[...] = _dot_nt(wgt_ref[...], hb)


def _norm_inproj(x2, g, w_main, w_gate_t):
    n, d = x2.shape
    tm = min(ROW_TILE, n)
    cols = w_main.shape[1]
    return pl.pallas_call(
        _norm_inproj_kernel,
        grid=(n // tm,),
        in_specs=[
            pl.BlockSpec((tm, d), lambda i: (i, 0)),
            pl.BlockSpec((1, d), lambda i: (0, 0)),
            pl.BlockSpec((d, cols), lambda i: (0, 0)),
            pl.BlockSpec((SUBLANES, d), lambda i: (0, 0)),
        ],
        out_specs=[
            pl.BlockSpec((tm, 3 * ATT_W), lambda i: (i, 0)),
            pl.BlockSpec((tm, 2 * W_MIX), lambda i: (i, 0)),
            pl.BlockSpec((tm, RW_COLS), lambda i: (i, 0)),
            pl.BlockSpec((tm, 3 * ATT_W), lambda i: (i, 0)),
            pl.BlockSpec((SUBLANES, tm), lambda i: (0, i)),
        ],
        out_shape=[
            jax.ShapeDtypeStruct((n, 3 * ATT_W), BF16),
            jax.ShapeDtypeStruct((n, 2 * W_MIX), F32),
            jax.ShapeDtypeStruct((n, RW_COLS), F32),
            jax.ShapeDtypeStruct((n, 3 * ATT_W), BF16),
            jax.ShapeDtypeStruct((SUBLANES, n), F32),
        ],
        compiler_params=_cparams("parallel"),
        name="norm_inproj",
    )(x2, g, w_main, w_gate_t)


def _sb_kernel(q_ref, k_ref, v_ref, tri_ref, o_ref, c_ref, acc_ref):
    qi = pl.program_id(1)
    tqb = q_ref.shape[1]
    hl = lambda h: slice(h * LANES, (h + 1) * LANES)

    def key_tile(j, off):
        r0 = 0 if off is None else off * ATT_TK
        rows = tqb - r0
        start = pl.multiple_of(j * ATT_TK, ATT_TK)
        if off is not None:
            strict = _iota2((rows, ATT_TK), 1) < _iota2((rows, ATT_TK), 0)
        zs, parts = [], []
        for h in range(N_HEADS):
            z = _dot_nt(q_ref[0, r0:, hl(h)], k_ref[0, pl.ds(start, ATT_TK), hl(h)])
            lom = -(jnp.maximum(z, 0.0) + jnp.log(1.0 + jnp.exp(-jnp.abs(z))))
            if off is not None:
                lom = jnp.where(strict, lom, 0.0)
            hi = lom.astype(BF16)
            lo = (lom - hi.astype(F32)).astype(BF16)
            zs.append(z)
            parts.append(jnp.concatenate([hi, lo], axis=1))
        rr = _dot(jnp.concatenate(parts, axis=0), tri_ref[...])
        for h in range(N_HEADS):
            rh = rr[h * rows:(h + 1) * rows]
            p = jnp.exp(zs[h] + rh[:, :ATT_TK] + c_ref[h, r0:, :])
            if off is not None:
                p = jnp.where(strict, p, 0.0)
            acc_ref[h, r0:, :] += _dot(p.astype(BF16), v_ref[0, pl.ds(start, ATT_TK), hl(h)])
            c_ref[h, r0:, :] += rh[:, ATT_TK:]

    c_ref[...] = jnp.zeros_like(c_ref)
    acc_ref[...] = jnp.zeros_like(acc_ref)
    n_diag = tqb // ATT_TK
    for off in reversed(range(n_diag)):
        key_tile(qi * n_diag + off, off)

    def body(jj, carry):
        key_tile(qi * n_diag - 1 - jj, None)
        return carry

    lax.fori_loop(0, qi * n_diag, body, 0)
    for h in range(N_HEADS):
        o_ref[0, :, h * HEAD_DIM:(h + 1) * HEAD_DIM] = acc_ref[h, :, :HEAD_DIM].astype(o_ref.dtype)


def _sb_attention(qkv, bsz, seq):
    qkv3 = qkv.reshape(bsz, seq, 3 * ATT_W)
    r = jnp.arange(ATT_TK)
    tri = (r[:, None] >= r[None, :]).astype(BF16)
    blk = jnp.concatenate([tri, jnp.ones((ATT_TK, ATT_TK), BF16)], axis=1)
    tri2 = jnp.concatenate([blk, blk], axis=0)
    tqb = min(ATT_TQB, seq)
    out = pl.pallas_call(
        _sb_kernel,
        grid=(bsz, seq // tqb),
        in_specs=[
            pl.BlockSpec((1, tqb, ATT_W), lambda b, i: (b, i, 0)),
            pl.BlockSpec((1, seq, ATT_W), lambda b, i: (b, 0, 1)),
            pl.BlockSpec((1, seq, ATT_W), lambda b, i: (b, 0, 2)),
            pl.BlockSpec((2 * ATT_TK, 2 * ATT_TK), lambda b, i: (0, 0)),
        ],
        out_specs=pl.BlockSpec((1, tqb, W_MIX), lambda b, i: (b, i, 0)),
        out_shape=jax.ShapeDtypeStruct((bsz, seq, W_MIX), BF16),
        scratch_shapes=[pltpu.VMEM((N_HEADS, tqb, ATT_TK), F32),
                        pltpu.VMEM((N_HEADS, tqb, LANES), F32)],
        compiler_params=_cparams("parallel", "parallel"),
        name="sb_attention",
    )(qkv3, qkv3, qkv3, tri2)
    return out.reshape(bsz * seq, W_MIX)


def _fx_cum_kernel(g_ref, b_ref, cum_ref):
    seq = g_ref.shape[2]
    blk = 2 * LANES
    r = _iota2((blk, blk), 0)
    c = _iota2((blk, blk), 1)
    tri = (r <= c).astype(F32)
    carry = jnp.zeros((SUBLANES, 1), F32)
    for s in range(seq // blk):
        x = g_ref[0, :, s * blk:(s + 1) * blk] + b_ref[...]
        lf = -(jnp.maximum(-x, 0.0) + jnp.log(1.0 + jnp.exp(-jnp.abs(x))))
        cs = _dot(lf, tri, precision=HIGHEST) + carry
        cum_ref[0, :, s * blk:(s + 1) * blk] = cs
        carry = cs[:, blk - 1:blk]


def _fx_cum(grow3, bias_col):
    bsz, _, seq = grow3.shape
    return pl.pallas_call(
        _fx_cum_kernel,
        grid=(bsz,),
        in_specs=[
            pl.BlockSpec((1, SUBLANES, seq), lambda b: (b, 0, 0)),
            pl.BlockSpec((SUBLANES, 1), lambda b: (0, 0)),
        ],
        out_specs=pl.BlockSpec((1, SUBLANES, seq), lambda b: (b, 0, 0)),
        out_shape=jax.ShapeDtypeStruct((bsz, SUBLANES, seq), F32),
        compiler_params=_cparams("parallel"),
        name="fx_cum",
    )(grow3, bias_col)


def _fx_kernel(q_ref, k_ref, v_ref, crow_ref, ccol_ref, o_ref, m_ref, acc_ref):
    qi = pl.program_id(1)
    tqb = q_ref.shape[1]
    hl = lambda h: slice(h * LANES, (h + 1) * LANES)

    def key_tile(j, off):
        r0 = 0 if off is None else off * ATT_TK
        rows = tqb - r0
        start = pl.multiple_of(j * ATT_TK, ATT_TK)
        if off is not None:
            causal = _iota2((rows, ATT_TK), 1) <= _iota2((rows, ATT_TK), 0)
        heads = range(N_HEADS)
        z = [_dot_nt(q_ref[0, r0:, hl(h)], k_ref[0, pl.ds(start, ATT_TK), hl(h)]) for h in heads]
        ps, alphas = [], []
        for h in heads:
            x = z[h] + ccol_ref[0, r0:, h:h + 1] - crow_ref[0, h:h + 1, pl.ds(start, ATT_TK)]
            if off is not None:
                x = jnp.where(causal, x, -jnp.inf)
            m_old = m_ref[h, r0:, :]
            m_new = jnp.maximum(m_old, jnp.max(x, axis=-1, keepdims=True))
            m_ref[h, r0:, :] = m_new
            ps.append(jnp.exp(x - m_new).astype(BF16))
            alphas.append(jnp.exp(m_old - m_new))
        pv = [_dot(ps[h], v_ref[0, pl.ds(start, ATT_TK), hl(h)]) for h in heads]
        for h in heads:
            acc_ref[h, r0:, :] = alphas[h] * acc_ref[h, r0:, :] + pv[h]

    m_ref[...] = jnp.full_like(m_ref, -jnp.inf)
    acc_ref[...] = jnp.zeros_like(acc_ref)
    n_diag = tqb // ATT_TK

    def body(j, carry):
        key_tile(j, None)
        return carry

    lax.fori_loop(0, qi * n_diag, body, 0)
    for off in range(n_diag):
        key_tile(qi * n_diag + off, off)
    for h in range(N_HEADS):
        acc = acc_ref[h]
        den = pltpu.roll(acc, HEAD_DIM, axis=1)
        o_ref[0, :, h * HEAD_DIM:(h + 1) * HEAD_DIM] = (acc / den)[:, :HEAD_DIM].astype(o_ref.dtype)


def _fx_attention(qkv, cum_row, bsz, seq):
    qkv3 = qkv.reshape(bsz, seq, 3 * ATT_W)
    cum_col = jnp.transpose(cum_row, (0, 2, 1))
    tqb = min(ATT_TQB, seq)
    out = pl.pallas_call(
        _fx_kernel,
        grid=(bsz, seq // tqb),
        in_specs=[
            pl.BlockSpec((1, tqb, ATT_W), lambda b, i: (b, i, 0)),
            pl.BlockSpec((1, seq, ATT_W), lambda b, i: (b, 0, 1)),
            pl.BlockSpec((1, seq, ATT_W), lambda b, i: (b, 0, 2)),
            pl.BlockSpec((1, SUBLANES, seq), lambda b, i: (b, 0, 0)),
            pl.BlockSpec((1, tqb, SUBLANES), lambda b, i: (b, i, 0)),
        ],
        out_specs=pl.BlockSpec((1, tqb, W_MIX), lambda b, i: (b, i, 0)),
        out_shape=jax.ShapeDtypeStruct((bsz, seq, W_MIX), BF16),
        scratch_shapes=[pltpu.VMEM((N_HEADS, tqb, LANES), F32),
                        pltpu.VMEM((N_HEADS, tqb, LANES), F32)],
        compiler_params=_cparams("parallel", "parallel"),
        name="fx_attention",
    )(qkv3, qkv3, qkv3, cum_row, cum_col)
    return out.reshape(bsz * seq, W_MIX)


def _gm_kernel(p_ref, w_ref, b_ref, o_ref):
    hid = jax.nn.gelu(p_ref[...])
    r = _iota2((GM_CHUNK, GM_CHUNK), 0)
    c = _iota2((GM_CHUNK, GM_CHUNK), 1)
    lower = c <= r
    for g in range(N_HEADS):
        u = hid[:, g * HEAD_DIM:(g + 1) * HEAD_DIM]
        v = hid[:, W_MIX + g * HEAD_DIM:W_MIX + (g + 1) * HEAD_DIM]
        mean = jnp.mean(v, axis=-1, keepdims=True)
        var = jnp.mean(jnp.square(v - mean), axis=-1, keepdims=True)
        vn = (v - mean) * lax.rsqrt(var + LN_EPS)
        w = jnp.where(lower, w_ref[g], 0.0).astype(BF16)
        mixed = _dot(w, vn.astype(BF16)) + b_ref[:, g:g + 1]
        o_ref[:, g * HEAD_DIM:(g + 1) * HEAD_DIM] = (u * mixed).astype(o_ref.dtype)


def _gm_mix(p_gm, w_s, b_col):
    n = p_gm.shape[0]
    return pl.pallas_call(
        _gm_kernel,
        grid=(n // GM_CHUNK,),
        in_specs=[
            pl.BlockSpec((GM_CHUNK, 2 * W_MIX), lambda i: (i, 0)),
            pl.BlockSpec((N_HEADS, GM_CHUNK, GM_CHUNK), lambda i: (0, 0, 0)),
            pl.BlockSpec((GM_CHUNK, N_HEADS), lambda i: (0, 0)),
        ],
        out_specs=pl.BlockSpec((GM_CHUNK, W_MIX), lambda i: (i, 0)),
        out_shape=jax.ShapeDtypeStruct((n, W_MIX), BF16),
        compiler_params=_cparams("parallel"),
        name="gm_mix",
    )(p_gm, w_s, b_col)


def _head_sum(x, ones_bd):
    return _dot(x, ones_bd, precision=HIGHEST)


def _rw_prep_kernel(p_ref, prev_ref, mu_ref, w0_ref, w2_ref, a0_ref, a2_ref, g2_ref,
                    kk_ref, ka_ref, rk_ref, bd_ref,
                    r_out, ld_out, k_out, v_out, kk_out, ba_out, bonus_out, g_out, *, seq):
    i = pl.program_id(0)
    tm = p_ref.shape[0]
    p = p_ref[...]
    first = (i * tm) % seq == 0
    prev_row = jnp.where(first, 0.0, prev_ref[SUBLANES - 1:SUBLANES, :])
    rows = _iota2(p.shape, 0)
    prev = jnp.where(rows == 0, prev_row, pltpu.roll(p, 1, axis=0))
    p = p + (prev - p) * mu_ref[...]
    r = p[:, 0:W_MIX]
    k = p[:, W_MIX:2 * W_MIX]
    v = p[:, 2 * W_MIX:3 * W_MIX]
    o = 3 * W_MIX
    xw = p[:, o:o + RW_DECAY_LORA]
    o += RW_DECAY_LORA
    xa = p[:, o:o + RW_AAA_LORA]
    o += RW_AAA_LORA
    xg = p[:, o:o + RW_GATE_LORA]
    wpre = -(w0_ref[...] + _dot(jnp.tanh(xw).astype(BF16), w2_ref[...]))
    w = -(jnp.maximum(wpre, 0.0) + jnp.log(1.0 + jnp.exp(-jnp.abs(wpre)))) - 0.5
    a = jax.nn.sigmoid(a0_ref[...] + _dot(xa.astype(BF16), a2_ref[...]))
    g = _dot(jax.nn.sigmoid(xg).astype(BF16), g2_ref[...])
    kk = k * kk_ref[...]
    nrm = jnp.maximum(jnp.sqrt(_head_sum(kk * kk, bd_ref[...])), L2_EPS)
    kk = kk / nrm
    k2 = k * (1.0 + (a - 1.0) * ka_ref[...])
    r_out[...] = r
    ld_out[...] = -jnp.exp(w)
    k_out[...] = k2
    v_out[...] = v
    kk_out[...] = kk
    ba_out[...] = kk * a
    bonus_out[...] = _head_sum(r * k2 * rk_ref[...], bd_ref[...]) * v
    g_out[...] = g


def _rw_prep(p_rw, seq, mu, w0, w2, a0, a2, g2, k_k, k_a, r_k, ones_bd):
    n = p_rw.shape[0]
    tm = min(256, seq)
    vec = lambda width: pl.BlockSpec((1, width), lambda i: (0, 0))
    mat = lambda rws: pl.BlockSpec((rws, W_MIX), lambda i: (0, 0))
    out_spec = pl.BlockSpec((tm, W_MIX), lambda i: (i, 0))
    return pl.pallas_call(
        functools.partial(_rw_prep_kernel, seq=seq),
        grid=(n // tm,),
        in_specs=[
            pl.BlockSpec((tm, RW_COLS), lambda i: (i, 0)),
            pl.BlockSpec((SUBLANES, RW_COLS),
                         lambda i: (jnp.maximum(i * (tm // SUBLANES) - 1, 0), 0)),
            vec(RW_COLS), vec(W_MIX), mat(RW_DECAY_LORA), vec(W_MIX), mat(RW_AAA_LORA),
            mat(RW_GATE_LORA), vec(W_MIX), vec(W_MIX), vec(W_MIX), mat(W_MIX),
        ],
        out_specs=[out_spec] * 8,
        out_shape=[jax.ShapeDtypeStruct((n, W_MIX), F32)] * 8,
        compiler_params=_cparams("parallel"),
        name="rw_prep",
    )(p_rw, p_rw, mu, w0, w2, a0, a2, g2, k_k, k_a, r_k, ones_bd)


_NN = (((1,), (0,)), ((), ()))
_NT = (((1,), (1,)), ((), ()))


def _split2(x):
    hi = x.astype(BF16)
    return hi, (x - hi.astype(F32)).astype(BF16)


def _mm_lhs2(a, b, dims):
    m = a.shape[0]
    hi, lo = _split2(a)
    r = lax.dot_general(jnp.concatenate([hi, lo], axis=0), b, dims, preferred_element_type=F32)
    return r[:m] + r[m:]


def _lane_head(shape, width):
    return lax.shift_right_logical(_iota2(shape, 1), width.bit_length() - 1)


def _head_blocks(x, width):
    lh = _lane_head(x.shape, width)
    return jnp.concatenate([jnp.where(lh == h, x, jnp.zeros_like(x)) for h in range(N_HEADS)],
                           axis=0)


def _head_diag(full, width):
    rows = full.shape[0] // N_HEADS
    lh = _lane_head((rows, full.shape[1]), width)
    out = jnp.zeros((rows, full.shape[1]), F32)
    for h in range(N_HEADS):
        out = out + jnp.where(lh == h, full[h * rows:(h + 1) * rows], 0.0)
    return out


def _tn_lhs2(a, b_bf16):
    hi = a.astype(BF16).astype(F32)
    b = b_bf16.astype(F32)
    return _dot_tn(hi, b) + _dot_tn(a - hi, b)


def _rw_intra_kernel(r_ref, ld_ref, k_ref, v_ref, kk_ref, ba_ref,
                     w_out, u0_out, y0_out, rt_out, btp_out, mrb_out, g0_out, pc_out):
    cs = RW_CHUNK
    cw = N_HEADS * cs
    row = _iota2((cs, cw), 0)
    colc = jnp.bitwise_and(_iota2((cs, cw), 1), cs - 1)
    lower = colc <= row
    strict = colc < row
    eye = (colc == row).astype(F32)
    tri3 = (jnp.bitwise_and(_iota2((cs, 3 * cs), 1), cs - 1) <= _iota2((cs, 3 * cs), 0)).astype(BF16)

    chunks = range(r_ref.shape[1] // cs)
    rows = [slice(c * cs, (c + 1) * cs) for c in chunks]
    each = lambda f: [f(c) for c in chunks]
    hb = lambda x: _head_blocks(x.astype(BF16), HEAD_DIM)

    def running_log_decay(c):
        ld = ld_ref[0, rows[c], :]
        l1 = ld.astype(BF16)
        l2, l3 = _split2(ld - l1.astype(F32))
        return _dot(tri3, jnp.concatenate([l1, l2, l3], axis=0))

    cl = each(running_log_decay)
    p_in = each(lambda c: jnp.exp(cl[c]))
    p_inv = each(lambda c: jnp.exp(-cl[c]))
    at = each(lambda c: -kk_ref[0, rows[c], :] * jnp.exp(cl[c] - ld_ref[0, rows[c], :]))
    bt = each(lambda c: ba_ref[0, rows[c], :] * p_inv[c])
    kt = each(lambda c: k_ref[0, rows[c], :] * p_inv[c])
    rt = each(lambda c: r_ref[0, rows[c], :] * p_in[c])
    pc = each(lambda c: p_in[c][cs - 1:cs, :])

    lhs = each(lambda c: jnp.concatenate([at[c], rt[c]], axis=0))
    ab_mb = each(lambda c: _mm_lhs2(lhs[c], hb(bt[c]), _NT))
    ak_mk = each(lambda c: _mm_lhs2(lhs[c], hb(kt[c]), _NT))
    a_ab = each(lambda c: jnp.where(strict, ab_mb[c][:cs], 0.0))
    m_rb = each(lambda c: jnp.where(lower, ab_mb[c][cs:], 0.0))
    a_ak = each(lambda c: jnp.where(strict, ak_mk[c][:cs], 0.0))
    m_rk = each(lambda c: jnp.where(lower, ak_mk[c][cs:], 0.0))

    inv = each(lambda c: eye + a_ab[c])
    pw = a_ab
    pw_blocks = each(lambda c: _head_blocks(pw[c].astype(BF16), cs))
    steps = 1
    while steps * 2 < cs:
        pw = each(lambda c: _mm_lhs2(pw[c], pw_blocks[c], _NN))
        pw_blocks = each(lambda c: _head_blocks(pw[c].astype(BF16), cs))
        inv = each(lambda c: inv[c] + _mm_lhs2(inv[c], pw_blocks[c], _NN))
        steps *= 2

    akv_mkv = each(lambda c: _mm_lhs2(jnp.concatenate([a_ak[c], m_rk[c]], axis=0),
                                      hb(v_ref[0, rows[c], :]), _NN))
    wu = each(lambda c: _mm_lhs2(
        inv[c], jnp.concatenate([hb(at[c]), hb(akv_mkv[c][:cs])], axis=1), _NN))
    g0 = each(lambda c: _head_diag(_tn_lhs2(v_ref[0, rows[c], :], kt[c].astype(BF16)), HEAD_DIM))
    for c in chunks:
        w_out[0, rows[c], :] = wu[c][:, :W_MIX]
        u0_out[0, rows[c], :] = wu[c][:, W_MIX:]
        y0_out[0, rows[c], :] = akv_mkv[c][cs:]
        rt_out[0, rows[c], :] = rt[c]
        btp_out[0, rows[c], :] = bt[c] * pc[c]
        mrb_out[0, rows[c], :] = m_rb[c]
        g0_out[0, rows[c], :] = g0[c] * pc[c]
        pc_out[0, c * SUBLANES:(c + 1) * SUBLANES, :] = jnp.broadcast_to(pc[c], (SUBLANES, W_MIX))


def _rw_intra(prep6, bsz, seq):
    arrs = [a.reshape(bsz, seq, W_MIX) for a in prep6]
    rows = RW_CHUNK * RW_CPS
    spec = pl.BlockSpec((1, rows, W_MIX), lambda b, c: (b, c, 0))
    pc_spec = pl.BlockSpec((1, SUBLANES * RW_CPS, W_MIX), lambda b, c: (b, c, 0))
    big = jax.ShapeDtypeStruct((bsz, seq, W_MIX), F32)
    return pl.pallas_call(
        _rw_intra_kernel,
        grid=(bsz, seq // rows),
        in_specs=[spec] * 6,
        out_specs=[spec] * 7 + [pc_spec],
        out_shape=[big] * 7 + [jax.ShapeDtypeStruct((bsz, seq // RW_CHUNK * SUBLANES, W_MIX), F32)],
        compiler_params=_cparams("parallel", "parallel"),
        name="rw_intra",
    )(*arrs)


def _rw_state_kernel(w_ref, u0_ref, y0_ref, rt_ref, btp_ref, mrb_ref, g0_ref, pc_ref,
                     bonus_ref, g_ref, gnw_ref, gnb_ref, ones_ref, o_ref, state_ref):
    cs = RW_CHUNK

    @pl.when(pl.program_id(1) == 0)
    def _():
        state_ref[...] = jnp.zeros_like(state_ref)

    batch = range(w_ref.shape[0])
    each = lambda f: [f(b) for b in batch]
    hb = lambda x: _head_blocks(x.astype(BF16), HEAD_DIM)
    s0 = each(lambda b: state_ref[b])
    wr = each(lambda b: _mm_lhs2(jnp.concatenate([w_ref[b], rt_ref[b]], axis=0), hb(s0[b]), _NT))
    u = each(lambda b: wr[b][:cs] + u0_ref[b])
    y = each(lambda b: wr[b][cs:] + _mm_lhs2(mrb_ref[b], hb(u[b]), _NN) + y0_ref[b])
    su = each(lambda b: _head_diag(_tn_lhs2(u[b], btp_ref[b].astype(BF16)), HEAD_DIM))
    for b in batch:
        state_ref[b] = s0[b] * pc_ref[b, 0:1, :] + g0_ref[b] + su[b]

    inv_n = 1.0 / HEAD_DIM
    mean = each(lambda b: _mm_lhs2(y[b], ones_ref[...], _NN) * inv_n)
    yc = each(lambda b: y[b] - mean[b])
    var = each(lambda b: _mm_lhs2(yc[b] * yc[b], ones_ref[...], _NN) * inv_n)
    for b in batch:
        yn = yc[b] * lax.rsqrt(var[b] + GN_EPS) * gnw_ref[...] + gnb_ref[...]
        o_ref[b] = ((yn + bonus_ref[b]) * g_ref[b]).astype(o_ref.dtype)


def _rw_state(intra, bonus, g, gn_w, gn_b, ones_bd, bsz, seq):
    gb = RW_BATCH_GROUP if bsz % RW_BATCH_GROUP == 0 else 1
    spec = pl.BlockSpec((gb, RW_CHUNK, W_MIX), lambda b, c: (b, c, 0))
    pc_spec = pl.BlockSpec((gb, SUBLANES, W_MIX), lambda b, c: (b, c, 0))
    vec = pl.BlockSpec((1, W_MIX), lambda b, c: (0, 0))
    out = pl.pallas_call(
        _rw_state_kernel,
        grid=(bsz // gb, seq // RW_CHUNK),
        in_specs=[spec] * 7 + [pc_spec, spec, spec, vec, vec,
                               pl.BlockSpec((W_MIX, W_MIX), lambda b, c: (0, 0))],
        out_specs=spec,
        out_shape=jax.ShapeDtypeStruct((bsz, seq, W_MIX), BF16),
        scratch_shapes=[pltpu.VMEM((gb, HEAD_DIM, W_MIX), F32)],
        compiler_params=_cparams("parallel", "arbitrary"),
        name="rw_state",
    )(*intra, bonus.reshape(bsz, seq, W_MIX), g.reshape(bsz, seq, W_MIX), gn_w, gn_b,
      ones_bd.astype(BF16))
    return out.reshape(bsz * seq, W_MIX)


def _store_token_rows(ref, val):
    tm, d = val.shape
    ch = d // LANES
    for s in range(ch):
        ref[pl.ds(s, tm, stride=ch), :] = val[:, s * LANES:(s + 1) * LANES]


def _load_token_rows(ref, tm, ch):
    return jnp.concatenate([ref[pl.ds(s, tm, stride=ch), :] for s in range(ch)], axis=1)


def _outproj_router_kernel(x_ref, sb_ref, gm_ref, rw_ref, fx_ref, wo_ref, g_ref, wr_ref, br_ref,
                           su_ref, x_out, h_out, idx_out, gate_out, rank_out, cnt_out, cnt_ref):
    mix = jnp.concatenate([sb_ref[...], gm_ref[...], rw_ref[...], fx_ref[...]], axis=1)
    x = x_ref[...] + _dot(mix, wo_ref[...])
    x_out[...] = x
    h = x * lax.rsqrt(jnp.mean(x * x, axis=-1, keepdims=True) + RMS_EPS) * g_ref[...]
    _store_token_rows(h_out, h)

    lg = _dot_nt(wr_ref[...], h, HIGHEST) + br_ref[...]
    tm = lg.shape[1]
    gl = [lg[g:g + 1, :] for g in range(N_EXPERT_GROUPS)]
    gmax = gl[0]
    gsel = jnp.zeros((1, tm), jnp.int32)
    for g in range(1, N_EXPERT_GROUPS):
        better = gl[g] > gmax
        gsel = jnp.where(better, g, gsel)
        gmax = jnp.where(better, gl[g], gmax)
    denom = gl[0] * 0.0
    for g in range(N_EXPERT_GROUPS):
        denom = denom + jnp.exp(gl[g] - gmax)
    g_gate = 1.0 / denom

    e0 = SUBLANES
    ing = lg[e0:e0 + EXPERTS_PER_GROUP, :]
    for g in range(1, N_EXPERT_GROUPS):
        ing = jnp.where(gsel == g, lg[e0 + g * EXPERTS_PER_GROUP:e0 + (g + 1) * EXPERTS_PER_GROUP, :], ing)
    ridx = _iota2(ing.shape, 0)
    m1 = jnp.max(ing, axis=0, keepdims=True)
    i1 = jnp.min(jnp.where(ing == m1, ridx, EXPERTS_PER_GROUP), axis=0, keepdims=True)
    rest = jnp.where(ridx == i1, -jnp.inf, ing)
    m2 = jnp.max(rest, axis=0, keepdims=True)
    i2 = jnp.min(jnp.where(rest == m2, ridx, EXPERTS_PER_GROUP), axis=0, keepdims=True)
    e2 = jnp.exp(m2 - m1)
    s = 1.0 + e2
    e_sel = (gsel * EXPERTS_PER_GROUP + i1, gsel * EXPERTS_PER_GROUP + i2)
    idx_out[0:1, :] = e_sel[0]
    idx_out[1:2, :] = e_sel[1]
    gate_out[0:1, :] = (1.0 / s) * g_gate
    gate_out[1:2, :] = (e2 / s) * g_gate

    @pl.when(pl.program_id(0) == 0)
    def _():
        cnt_ref[...] = jnp.zeros_like(cnt_ref)

    erow = _iota2((N_EXPERTS, tm), 0)
    onehot = [(erow == e).astype(F32) for e in e_sel]
    both = onehot[0] + onehot[1]
    before = cnt_ref[...] + _dot(both.astype(BF16), su_ref[...])
    for k in range(TOP_K):
        rank_out[k:k + 1, :] = jnp.sum(onehot[k] * before, axis=0, keepdims=True).astype(jnp.int32)
    cnt_ref[...] = cnt_ref[...] + jnp.sum(both, axis=1, keepdims=True)
    cnt_out[...] = jnp.broadcast_to(cnt_ref[...], cnt_out.shape).astype(jnp.int32)


def _outproj_router(x2, ys, w_out, g2, w_router_t, b_router):
    n, d = x2.shape
    tm = min(ROW_TILE, n)
    nr = w_router_t.shape[0]
    ymix = pl.BlockSpec((tm, W_MIX), lambda i: (i, 0))
    r = jnp.arange(tm)
    strict_upper = (r[:, None] < r[None, :]).astype(BF16)
    return pl.pallas_call(
        _outproj_router_kernel,
        grid=(n // tm,),
        in_specs=[
            pl.BlockSpec((tm, d), lambda i: (i, 0)),
            ymix, ymix, ymix, ymix,
            pl.BlockSpec((4 * W_MIX, d), lambda i: (0, 0)),
            pl.BlockSpec((1, d), lambda i: (0, 0)),
            pl.BlockSpec((nr, d), lambda i: (0, 0)),
            pl.BlockSpec((nr, 1), lambda i: (0, 0)),
            pl.BlockSpec((tm, tm), lambda i: (0, 0)),
        ],
        out_specs=[
            pl.BlockSpec((tm, d), lambda i: (i, 0)),
            pl.BlockSpec((tm * (d // LANES), LANES), lambda i: (i, 0)),
            pl.BlockSpec((TOP_K, tm), lambda i: (0, i)),
            pl.BlockSpec((TOP_K, tm), lambda i: (0, i)),
            pl.BlockSpec((TOP_K, tm), lambda i: (0, i)),
            pl.BlockSpec((N_EXPERTS, LANES), lambda i: (0, 0)),
        ],
        out_shape=[
            jax.ShapeDtypeStruct((n, d), F32),
            jax.ShapeDtypeStruct((n * (d // LANES), LANES), F32),
            jax.ShapeDtypeStruct((TOP_K, n), jnp.int32),
            jax.ShapeDtypeStruct((TOP_K, n), F32),
            jax.ShapeDtypeStruct((TOP_K, n), jnp.int32),
            jax.ShapeDtypeStruct((N_EXPERTS, LANES), jnp.int32),
        ],
        scratch_shapes=[pltpu.VMEM((N_EXPERTS, 1), F32)],
        compiler_params=_cparams("arbitrary"),
        name="outproj_router",
    )(x2, *ys, w_out, g2, w_router_t, b_router, strict_upper)


def _token_copy(src_hbm, dst, src_tok, dst_tok, sem, ch):
    return pltpu.make_async_copy(
        src_hbm.at[pl.ds(pl.multiple_of(src_tok * ch, ch), ch)],
        dst.at[pl.ds(pl.multiple_of(dst_tok * ch, ch), ch)], sem)


def _dispatch_kernel(pad_end_ref, slot_ref, h_hbm, o_hbm, zbuf, zsem, sem, *, ch, tm):
    i = pl.program_id(0)
    last = pl.num_programs(0) - 1

    def zero_copy(first_slot):
        return pltpu.make_async_copy(
            zbuf, o_hbm.at[pl.ds(pl.multiple_of(first_slot * ch, ch), MOE_TB * ch)], zsem)

    @pl.when(i == 0)
    def _():
        zbuf[...] = jnp.zeros_like(zbuf)
        tails = [jnp.maximum(pad_end_ref[e] - MOE_TB, 0) for e in range(N_EXPERTS)]
        for first in tails:
            zero_copy(first).start()
        for first in tails:
            zero_copy(first).wait()

        def unused_block(b, _):
            zero_copy(b * MOE_TB).start()
            zero_copy(b * MOE_TB).wait()
            return 0

        lax.fori_loop(pad_end_ref[N_EXPERTS - 1] // MOE_TB, o_hbm.shape[0] // (MOE_TB * ch),
                      unused_block, 0)

    def start(r, _):
        for k in range(TOP_K):
            _token_copy(h_hbm, o_hbm, i * tm + r, slot_ref[0, 0, r * TOP_K + k], sem.at[i % 2],
                        ch).start()
        return 0

    lax.fori_loop(0, tm, start, 0, unroll=DMA_ISSUE_UNROLL)

    def wait_step(parity):
        block = pl.ds(0, tm * TOP_K * ch)
        pltpu.make_async_copy(h_hbm.at[block], o_hbm.at[block], sem.at[parity]).wait()

    @pl.when(i > 0)
    def _():
        wait_step((i + 1) % 2)

    @pl.when(i == last)
    def _():
        wait_step(i % 2)


def _dispatch(pad_end, slot_flat, h_rows, n_slots, ch):
    n_tok = h_rows.shape[0] // ch
    tm = min(256, n_tok)
    n_steps = n_tok // tm
    return pl.pallas_call(
        functools.partial(_dispatch_kernel, ch=ch, tm=tm),
        grid_spec=pltpu.PrefetchScalarGridSpec(
            num_scalar_prefetch=1,
            grid=(n_steps,),
            in_specs=[
                pl.BlockSpec((1, 1, tm * TOP_K), lambda i, pe: (i, 0, 0), memory_space=pltpu.SMEM),
                pl.BlockSpec(memory_space=pl.ANY),
            ],
            out_specs=pl.BlockSpec(memory_space=pl.ANY),
            scratch_shapes=[pltpu.VMEM((MOE_TB * ch, LANES), F32), pltpu.SemaphoreType.DMA,
                            pltpu.SemaphoreType.DMA((2,))],
        ),
        out_shape=jax.ShapeDtypeStruct((n_slots * ch, LANES), h_rows.dtype),
        compiler_params=_cparams("arbitrary"),
        name="moe_dispatch",
    )(pad_end, slot_flat.reshape(n_steps, 1, tm * TOP_K), h_rows)


def _expert_kernel(be_ref, nb_ref, x_ref, wg_ref, wu_ref, wd_ref, o_ref, *, ch):
    i = pl.program_id(0)

    @pl.when(i < nb_ref[0])
    def _():
        x = _load_token_rows(x_ref, MOE_TB, ch).astype(BF16)
        a = _dot(x, wg_ref[0, 0].astype(BF16))
        u = _dot(x, wu_ref[0, 0].astype(BF16))
        hid = (a * jax.nn.sigmoid(a) * u).astype(BF16)
        _store_token_rows(o_ref, _dot(hid, wd_ref[0, 0].astype(BF16)))

    @pl.when(i >= nb_ref[0])
    def _():
        o_ref[...] = jnp.zeros_like(o_ref)


def _expert_blocks(block_expert, n_used, xb_rows, w_gate, w_up, w_down, layer):
    d, hid = w_gate.shape[2], w_gate.shape[3]
    ch = d // LANES
    n_slots = xb_rows.shape[0] // ch
    return pl.pallas_call(
        functools.partial(_expert_kernel, ch=ch),
        grid_spec=pltpu.PrefetchScalarGridSpec(
            num_scalar_prefetch=2,
            grid=(n_slots // MOE_TB,),
            in_specs=[
                pl.BlockSpec((MOE_TB * ch, LANES), lambda i, be, nb: (i, 0)),
                pl.BlockSpec((1, 1, d, hid), lambda i, be, nb: (layer, be[i], 0, 0)),
                pl.BlockSpec((1, 1, d, hid), lambda i, be, nb: (layer, be[i], 0, 0)),
                pl.BlockSpec((1, 1, hid, d), lambda i, be, nb: (layer, be[i], 0, 0)),
            ],
            out_specs=pl.BlockSpec((MOE_TB * ch, LANES), lambda i, be, nb: (i, 0)),
        ),
        out_shape=jax.ShapeDtypeStruct((n_slots * ch, LANES), F32),
        compiler_params=_cparams("arbitrary"),
        name="moe_experts",
    )(block_expert, n_used, xb_rows, w_gate, w_up, w_down)


def _combine_kernel(slot_ref, slot_next_ref, x_ref, gate_ref, yb_hbm, o_ref, buf, sem, *, ch):
    i = pl.program_id(0)
    last = pl.num_programs(0) - 1
    tm = x_ref.shape[0]

    def issue(idx_ref, parity):
        def start(r, _):
            for k in range(TOP_K):
                _token_copy(yb_hbm, buf.at[parity, k], idx_ref[0, 0, r * TOP_K + k], r,
                            sem.at[parity], ch).start()
            return 0

        lax.fori_loop(0, tm, start, 0, unroll=DMA_ISSUE_UNROLL)

    @pl.when(i == 0)
    def _():
        issue(slot_ref, 0)

    @pl.when(i < last)
    def _():
        issue(slot_next_ref, (i + 1) % 2)

    par = i % 2
    for k in range(TOP_K):
        pltpu.make_async_copy(yb_hbm.at[pl.ds(0, tm * ch)], buf.at[par, k], sem.at[par]).wait()
    y = (_load_token_rows(buf.at[par, 0], tm, ch) * gate_ref[:, 0:1]
         + _load_token_rows(buf.at[par, 1], tm, ch) * gate_ref[:, 1:2])
    o_ref[...] = x_ref[...] + y


def _combine(slot_flat, x2, gates, yb_rows):
    n, d = x2.shape
    ch = d // LANES
    tm = min(256, n)
    n_steps = n // tm
    slots3 = slot_flat.reshape(n_steps, 1, tm * TOP_K)
    return pl.pallas_call(
        functools.partial(_combine_kernel, ch=ch),
        grid=(n_steps,),
        in_specs=[
            pl.BlockSpec((1, 1, tm * TOP_K), lambda i: (i, 0, 0), memory_space=pltpu.SMEM),
            pl.BlockSpec((1, 1, tm * TOP_K), lambda i: (jnp.minimum(i + 1, n_steps - 1), 0, 0),
                         memory_space=pltpu.SMEM),
            pl.BlockSpec((tm, d), lambda i: (i, 0)),
            pl.BlockSpec((tm, TOP_K), lambda i: (i, 0)),
            pl.BlockSpec(memory_space=pl.ANY),
        ],
        out_specs=pl.BlockSpec((tm, d), lambda i: (i, 0)),
        scratch_shapes=[pltpu.VMEM((2, TOP_K, tm * ch, LANES), F32),
                        pltpu.SemaphoreType.DMA((2,))],
        out_shape=jax.ShapeDtypeStruct((n, d), F32),
        compiler_params=_cparams("arbitrary"),
        name="moe_combine",
    )(slots3, slots3, x2, gates, yb_rows)


def _routing_plan(expert_idx_t, rank_t, counts, n_tok):
    padded = (counts + MOE_TB - 1) // MOE_TB * MOE_TB
    pad_end = jnp.cumsum(padded)
    pad_start = pad_end - padded
    start_t = jnp.sum(jnp.where(expert_idx_t[:, :, None] == jnp.arange(N_EXPERTS), pad_start, 0),
                      axis=-1)
    slot = jnp.transpose(start_t + rank_t).reshape(n_tok * TOP_K)
    n_blocks = -(-(n_tok * TOP_K) // MOE_TB) + N_EXPERTS
    block_expert = jnp.minimum(
        jnp.searchsorted(pad_end, jnp.arange(n_blocks) * MOE_TB, side='right'),
        N_EXPERTS - 1).astype(jnp.int32)
    n_used = (pad_end[-1] // MOE_TB).astype(jnp.int32).reshape(1)
    return slot.astype(jnp.int32), pad_end.astype(jnp.int32), block_expert, n_used, n_blocks * MOE_TB


def _moe(x2, h2, expert_idx_t, gates_t, rank_t, counts, w_gate, w_up, w_down, layer):
    n_tok = x2.shape[0]
    slot, pad_end, block_expert, n_used, n_slots = _routing_plan(expert_idx_t, rank_t, counts, n_tok)
    xb = _dispatch(pad_end, slot, h2, n_slots, x2.shape[1] // LANES)
    yb = _expert_blocks(block_expert, n_used, xb, w_gate, w_up, w_down, layer)
    return _combine(slot, x2, jnp.transpose(gates_t), yb)


def _final_norm_kernel(x_ref, g_ref, o_ref):
    x = x_ref[...]
    o_ref[...] = x * lax.rsqrt(jnp.mean(x * x, axis=-1, keepdims=True) + RMS_EPS) * g_ref[...]


def _final_norm(x2, g):
    n, d = x2.shape
    tm = min(ROW_TILE, n)
    return pl.pallas_call(
        _final_norm_kernel,
        grid=(n // tm,),
        in_specs=[pl.BlockSpec((tm, d), lambda i: (i, 0)), pl.BlockSpec((1, d), lambda i: (0, 0))],
        out_specs=pl.BlockSpec((tm, d), lambda i: (i, 0)),
        out_shape=jax.ShapeDtypeStruct((n, d), F32),
        compiler_params=_cparams("parallel"),
        name="final_norm",
    )(x2, g)


def _layer_params(l, w_in, gm_b, rw_mu, rw_w0, rw_w2, rw_a0, rw_a2, rw_g2, rw_k_k, rw_k_a, rw_r_k,
                  rw_gn_w, rw_gn_b, fx_b_f, w_out, router_group_w, router_group_b,
                  router_expert_w, router_expert_b):
    d = w_in.shape[1]
    n_main = 3 * W_MIX + 2 * W_MIX + RW_COLS + 3 * W_MIX
    w = w_in[l]
    w_main = w[:, :n_main].astype(BF16)
    w_gate_t = jnp.zeros((SUBLANES, d), F32).at[:N_HEADS].set(w[:, n_main:n_main + N_HEADS].T)
    fx_bias = jnp.zeros((SUBLANES, 1), F32).at[:N_HEADS, 0].set(fx_b_f[l])
    wr = jnp.zeros((SUBLANES + N_EXPERTS, d), F32)
    wr = wr.at[:N_EXPERT_GROUPS].set(router_group_w[l].T).at[SUBLANES:].set(router_expert_w[l].T)
    br = jnp.zeros((SUBLANES + N_EXPERTS, 1), F32)
    br = br.at[:N_EXPERT_GROUPS, 0].set(router_group_b[l]).at[SUBLANES:, 0].set(router_expert_b[l])
    row = lambda a: a.reshape(1, -1)
    return dict(
        w_main=w_main, w_gate_t=w_gate_t.astype(BF16), fx_bias=fx_bias,
        gm_b=gm_b[l].T,
        mu=row(rw_mu[l]), w0=row(rw_w0[l]), w2=rw_w2[l].astype(BF16), a0=row(rw_a0[l]),
        a2=rw_a2[l].astype(BF16), g2=rw_g2[l].astype(BF16), k_k=row(rw_k_k[l]), k_a=row(rw_k_a[l]),
        r_k=row(rw_r_k[l]), gn_w=row(rw_gn_w[l]), gn_b=row(rw_gn_b[l]),
        w_out=w_out[l].astype(BF16), wr=wr, br=br,
    )


def kernel(x, norm1_g, w_in, gm_w_s, gm_b, rw_mu, rw_w0, rw_w2, rw_a0, rw_a2, rw_g2, rw_k_k, rw_k_a, rw_r_k, rw_gn_w, rw_gn_b, fx_b_f, w_out, norm2_g, router_group_w, router_group_b, router_expert_w, router_expert_b, exp_w_gate, exp_w_up, exp_w_down, final_norm_g):
    bsz, seq, d = x.shape
    depth = w_in.shape[0]
    assert seq % min(ATT_TQB, seq) == 0 and seq % RW_CHUNK == 0 and seq % (2 * LANES) == 0
    x2 = x.reshape(bsz * seq, d)
    hd = jnp.arange(W_MIX) // HEAD_DIM
    ones_bd = (hd[:, None] == hd[None, :]).astype(F32)
    for l in range(depth):
        lp = _layer_params(l, w_in, gm_b, rw_mu, rw_w0, rw_w2, rw_a0, rw_a2, rw_g2, rw_k_k, rw_k_a,
                           rw_r_k, rw_gn_w, rw_gn_b, fx_b_f, w_out, router_group_w, router_group_b,
                           router_expert_w, router_expert_b)
        p_sb, p_gm, p_rw, p_fx, grow = _norm_inproj(x2, norm1_g[l].reshape(1, d), lp["w_main"],
                                                    lp["w_gate_t"])
        y_sb = _sb_attention(p_sb, bsz, seq)
        y_gm = _gm_mix(p_gm, gm_w_s[l], lp["gm_b"])
        prep = _rw_prep(p_rw, seq, lp["mu"], lp["w0"], lp["w2"], lp["a0"], lp["a2"], lp["g2"],
                        lp["k_k"], lp["k_a"], lp["r_k"], ones_bd)
        intra = _rw_intra(prep[:6], bsz, seq)
        y_rw = _rw_state(intra, prep[6], prep[7], lp["gn_w"], lp["gn_b"], ones_bd, bsz, seq)
        grow3 = jnp.transpose(grow.reshape(SUBLANES, bsz, seq), (1, 0, 2))
        cum_row = _fx_cum(grow3, lp["fx_bias"])
        y_fx = _fx_attention(p_fx, cum_row, bsz, seq)
        x2, h2, eidx_t, gates_t, rank_t, cnt = _outproj_router(
            x2, (y_sb, y_gm, y_rw, y_fx), lp["w_out"], norm2_g[l].reshape(1, d), lp["wr"], lp["br"])
        x2 = _moe(x2, h2, eidx_t, gates_t, rank_t, cnt[:, 0], exp_w_gate, exp_w_up, exp_w_down, l)
    return _final_norm(x2, final_norm_g.reshape(1, d)).reshape(bsz, seq, d)
```

```python
import functools

import jax
import jax.numpy as jnp
from jax import lax
from jax.experimental import pallas as pl
from jax.experimental.pallas import tpu as pltpu

F32 = jnp.float32
BF16 = jnp.bfloat16
HIGHEST = lax.Precision.HIGHEST

HEAD_DIM = 64
N_HEADS = 4
W_MIX = N_HEADS * HEAD_DIM
GM_CHUNK = 128
RW_DECAY_LORA = 64
RW_AAA_LORA = 64
RW_GATE_LORA = 128
RW_COLS = 3 * W_MIX + RW_DECAY_LORA + RW_AAA_LORA + RW_GATE_LORA
N_EXPERT_GROUPS = 4
EXPERTS_PER_GROUP = 8
N_EXPERTS = N_EXPERT_GROUPS * EXPERTS_PER_GROUP
TOP_K = 2
RMS_EPS = 1e-6
LN_EPS = 1e-5
GN_EPS = 64e-5
L2_EPS = 1e-12

V7X_VMEM_LIMIT_BYTES = 56 * 1024 * 1024
LANES = 128
SUBLANES = 8

ATT_TK = 128
ATT_TQB = 512
ATT_W = N_HEADS * LANES
RW_CHUNK = 64
RW_CPS = 4
RW_BATCH_GROUP = 4
ROW_TILE = 512
MOE_TB = 256
DMA_ISSUE_UNROLL = 8


def _cparams(*sem):
    return pltpu.CompilerParams(dimension_semantics=sem, vmem_limit_bytes=V7X_VMEM_LIMIT_BYTES)


def _dot(a, b, precision=None):
    return jnp.dot(a, b, preferred_element_type=F32, precision=precision)


def _dot_nt(a, b, precision=None):
    return lax.dot_general(a, b, (((1,), (1,)), ((), ())), preferred_element_type=F32,
                           precision=precision)


def _dot_tn(a, b, precision=None):
    return lax.dot_general(a, b, (((0,), (0,)), ((), ())), preferred_element_type=F32,
                           precision=precision)


def _iota2(shape, dim):
    return lax.broadcasted_iota(jnp.int32, shape, dim)


def _store_heads_padded(ref, col0, src, fill):
    low = _iota2((src.shape[0], LANES), 1) < HEAD_DIM
    for pair in range(N_HEADS // 2):
        slab = src[:, pair * LANES:(pair + 1) * LANES]
        swapped = pltpu.roll(slab, HEAD_DIM, axis=1)
        for odd, val in enumerate((slab, swapped)):
            c = col0 + (2 * pair + odd) * LANES
            ref[:, c:c + LANES] = jnp.where(low, val, fill).astype(ref.dtype)


def _norm_inproj_kernel(x_ref, g_ref, w_ref, wgt_ref, sb_ref, gm_ref, rw_ref, fx_ref, grow_ref):
    x = x_ref[...]
    h = x * lax.rsqrt(jnp.mean(x * x, axis=-1, keepdims=True) + RMS_EPS) * g_ref[...]
    hb = h.astype(BF16)
    p = _dot(hb, w_ref[...])
    scale = HEAD_DIM ** -0.5
    wp = N_HEADS * LANES
    o = 0
    _store_heads_padded(sb_ref, 0, p[:, o:o + W_MIX] * scale, 0.0)
    _store_heads_padded(sb_ref, wp, p[:, o + W_MIX:o + 2 * W_MIX], 0.0)
    _store_heads_padded(sb_ref, 2 * wp, p[:, o + 2 * W_MIX:o + 3 * W_MIX], 0.0)
    o += 3 * W_MIX
    gm_ref[...] = p[:, o:o + 2 * W_MIX]
    o += 2 * W_MIX
    rw_ref[...] = p[:, o:o + RW_COLS]
    o += RW_COLS
    _store_heads_padded(fx_ref, 0, p[:, o:o + W_MIX] * scale, 0.0)
    _store_heads_padded(fx_ref, wp, p[:, o + W_MIX:o + 2 * W_MIX], 0.0)
    _store_heads_padded(fx_ref, 2 * wp, p[:, o + 2 * W_MIX:o + 3 * W_MIX], 1.0)
    grow_ref[...] = _dot_nt(wgt_ref[...], hb)


def _norm_inproj(x2, g, w_main, w_gate_t):
    n, d = x2.shape
    tm = min(ROW_TILE, n)
    cols = w_main.shape[1]
    return pl.pallas_call(
        _norm_inproj_kernel,
        grid=(n // tm,),
        in_specs=[
            pl.BlockSpec((tm, d), lambda i: (i, 0)),
            pl.BlockSpec((1, d), lambda i: (0, 0)),
            pl.BlockSpec((d, cols), lambda i: (0, 0)),
            pl.BlockSpec((SUBLANES, d), lambda i: (0, 0)),
        ],
        out_specs=[
            pl.BlockSpec((tm, 3 * ATT_W), lambda i: (i, 0)),
            pl.BlockSpec((tm, 2 * W_MIX), lambda i: (i, 0)),
            pl.BlockSpec((tm, RW_COLS), lambda i: (i, 0)),
            pl.BlockSpec((tm, 3 * ATT_W), lambda i: (i, 0)),
            pl.BlockSpec((SUBLANES, tm), lambda i: (0, i)),
        ],
        out_shape=[
            jax.ShapeDtypeStruct((n, 3 * ATT_W), BF16),
            jax.ShapeDtypeStruct((n, 2 * W_MIX), F32),
            jax.ShapeDtypeStruct((n, RW_COLS), F32),
            jax.ShapeDtypeStruct((n, 3 * ATT_W), BF16),
            jax.ShapeDtypeStruct((SUBLANES, n), F32),
        ],
        compiler_params=_cparams("parallel"),
        name="norm_inproj",
    )(x2, g, w_main, w_gate_t)


def _sb_kernel(q_ref, k_ref, v_ref, tri_ref, o_ref, c_ref, acc_ref):
    qi = pl.program_id(1)
    tqb = q_ref.shape[1]
    hl = lambda h: slice(h * LANES, (h + 1) * LANES)

    def key_tile(j, off):
        r0 = 0 if off is None else off * ATT_TK
        rows = tqb - r0
        start = pl.multiple_of(j * ATT_TK, ATT_TK)
        if off is not None:
            strict = _iota2((rows, ATT_TK), 1) < _iota2((rows, ATT_TK), 0)
        zs, parts = [], []
        for h in range(N_HEADS):
            z = _dot_nt(q_ref[0, r0:, hl(h)], k_ref[0, pl.ds(start, ATT_TK), hl(h)])
            lom = -(jnp.maximum(z, 0.0) + jnp.log(1.0 + jnp.exp(-jnp.abs(z))))
            if off is not None:
                lom = jnp.where(strict, lom, 0.0)
            hi = lom.astype(BF16)
            lo = (lom - hi.astype(F32)).astype(BF16)
            zs.append(z)
            parts.append(jnp.concatenate([hi, lo], axis=1))
        rr = _dot(jnp.concatenate(parts, axis=0), tri_ref[...])
        for h in range(N_HEADS):
            rh = rr[h * rows:(h + 1) * rows]
            p = jnp.exp(zs[h] + rh[:, :ATT_TK] + c_ref[h, r0:, :])
            if off is not None:
                p = jnp.where(strict, p, 0.0)
            acc_ref[h, r0:, :] += _dot(p.astype(BF16), v_ref[0, pl.ds(start, ATT_TK), hl(h)])
            c_ref[h, r0:, :] += rh[:, ATT_TK:]

    c_ref[...] = jnp.zeros_like(c_ref)
    acc_ref[...] = jnp.zeros_like(acc_ref)
    n_diag = tqb // ATT_TK
    for off in reversed(range(n_diag)):
        key_tile(qi * n_diag + off, off)

    def body(jj, carry):
        key_tile(qi * n_diag - 1 - jj, None)
        return carry

    lax.fori_loop(0, qi * n_diag, body, 0)
    for h in range(N_HEADS):
        o_ref[0, :, h * HEAD_DIM:(h + 1) * HEAD_DIM] = acc_ref[h, :, :HEAD_DIM].astype(o_ref.dtype)


def _sb_attention(qkv, bsz, seq):
    qkv3 = qkv.reshape(bsz, seq, 3 * ATT_W)
    r = jnp.arange(ATT_TK)
    tri = (r[:, None] >= r[None, :]).astype(BF16)
    blk = jnp.concatenate([tri, jnp.ones((ATT_TK, ATT_TK), BF16)], axis=1)
    tri2 = jnp.concatenate([blk, blk], axis=0)
    tqb = min(ATT_TQB, seq)
    out = pl.pallas_call(
        _sb_kernel,
        grid=(bsz, seq // tqb),
        in_specs=[
            pl.BlockSpec((1, tqb, ATT_W), lambda b, i: (b, i, 0)),
            pl.BlockSpec((1, seq, ATT_W), lambda b, i: (b, 0, 1)),
            pl.BlockSpec((1, seq, ATT_W), lambda b, i: (b, 0, 2)),
            pl.BlockSpec((2 * ATT_TK, 2 * ATT_TK), lambda b, i: (0, 0)),
        ],
        out_specs=pl.BlockSpec((1, tqb, W_MIX), lambda b, i: (b, i, 0)),
        out_shape=jax.ShapeDtypeStruct((bsz, seq, W_MIX), BF16),
        scratch_shapes=[pltpu.VMEM((N_HEADS, tqb, ATT_TK), F32),
                        pltpu.VMEM((N_HEADS, tqb, LANES), F32)],
        compiler_params=_cparams("parallel", "parallel"),
        name="sb_attention",
    )(qkv3, qkv3, qkv3, tri2)
    return out.reshape(bsz * seq, W_MIX)


def _fx_cum_kernel(g_ref, b_ref, cum_ref):
    seq = g_ref.shape[2]
    blk = 2 * LANES
    r = _iota2((blk, blk), 0)
    c = _iota2((blk, blk), 1)
    tri = (r <= c).astype(F32)
    carry = jnp.zeros((SUBLANES, 1), F32)
    for s in range(seq // blk):
        x = g_ref[0, :, s * blk:(s + 1) * blk] + b_ref[...]
        lf = -(jnp.maximum(-x, 0.0) + jnp.log(1.0 + jnp.exp(-jnp.abs(x))))
        cs = _dot(lf, tri, precision=HIGHEST) + carry
        cum_ref[0, :, s * blk:(s + 1) * blk] = cs
        carry = cs[:, blk - 1:blk]


def _fx_cum(grow3, bias_col):
    bsz, _, seq = grow3.shape
    return pl.pallas_call(
        _fx_cum_kernel,
        grid=(bsz,),
        in_specs=[
            pl.BlockSpec((1, SUBLANES, seq), lambda b: (b, 0, 0)),
            pl.BlockSpec((SUBLANES, 1), lambda b: (0, 0)),
        ],
        out_specs=pl.BlockSpec((1, SUBLANES, seq), lambda b: (b, 0, 0)),
        out_shape=jax.ShapeDtypeStruct((bsz, SUBLANES, seq), F32),
        compiler_params=_cparams("parallel"),
        name="fx_cum",
    )(grow3, bias_col)


def _fx_kernel(q_ref, k_ref, v_ref, crow_ref, ccol_ref, o_ref, m_ref, acc_ref):
    qi = pl.program_id(1)
    tqb = q_ref.shape[1]
    hl = lambda h: slice(h * LANES, (h + 1) * LANES)

    def key_tile(j, off):
        r0 = 0 if off is None else off * ATT_TK
        rows = tqb - r0
        start = pl.multiple_of(j * ATT_TK, ATT_TK)
        if off is not None:
            causal = _iota2((rows, ATT_TK), 1) <= _iota2((rows, ATT_TK), 0)
        heads = range(N_HEADS)
        z = [_dot_nt(q_ref[0, r0:, hl(h)], k_ref[0, pl.ds(start, ATT_TK), hl(h)]) for h in heads]
        ps, alphas = [], []
        for h in heads:
            x = z[h] + ccol_ref[0, r0:, h:h + 1] - crow_ref[0, h:h + 1, pl.ds(start, ATT_TK)]
            if off is not None:
                x = jnp.where(causal, x, -jnp.inf)
            m_old = m_ref[h, r0:, :]
            m_new = jnp.maximum(m_old, jnp.max(x, axis=-1, keepdims=True))
            m_ref[h, r0:, :] = m_new
            ps.append(jnp.exp(x - m_new).astype(BF16))
            alphas.append(jnp.exp(m_old - m_new))
        pv = [_dot(ps[h], v_ref[0, pl.ds(start, ATT_TK), hl(h)]) for h in heads]
        for h in heads:
            acc_ref[h, r0:, :] = alphas[h] * acc_ref[h, r0:, :] + pv[h]

    m_ref[...] = jnp.full_like(m_ref, -jnp.inf)
    acc_ref[...] = jnp.zeros_like(acc_ref)
    n_diag = tqb // ATT_TK

    def body(j, carry):
        key_tile(j, None)
        return carry

    lax.fori_loop(0, qi * n_diag, body, 0)
    for off in range(n_diag):
        key_tile(qi * n_diag + off, off)
    for h in range(N_HEADS):
        acc = acc_ref[h]
        den = pltpu.roll(acc, HEAD_DIM, axis=1)
        o_ref[0, :, h * HEAD_DIM:(h + 1) * HEAD_DIM] = (acc / den)[:, :HEAD_DIM].astype(o_ref.dtype)


def _fx_attention(qkv, cum_row, bsz, seq):
    qkv3 = qkv.reshape(bsz, seq, 3 * ATT_W)
    cum_col = jnp.transpose(cum_row, (0, 2, 1))
    tqb = min(ATT_TQB, seq)
    out = pl.pallas_call(
        _fx_kernel,
        grid=(bsz, seq // tqb),
        in_specs=[
            pl.BlockSpec((1, tqb, ATT_W), lambda b, i: (b, i, 0)),
            pl.BlockSpec((1, seq, ATT_W), lambda b, i: (b, 0, 1)),
            pl.BlockSpec((1, seq, ATT_W), lambda b, i: (b, 0, 2)),
            pl.BlockSpec((1, SUBLANES, seq), lambda b, i: (b, 0, 0)),
            pl.BlockSpec((1, tqb, SUBLANES), lambda b, i: (b, i, 0)),
        ],
        out_specs=pl.BlockSpec((1, tqb, W_MIX), lambda b, i: (b, i, 0)),
        out_shape=jax.ShapeDtypeStruct((bsz, seq, W_MIX), BF16),
        scratch_shapes=[pltpu.VMEM((N_HEADS, tqb, LANES), F32),
                        pltpu.VMEM((N_HEADS, tqb, LANES), F32)],
        compiler_params=_cparams("parallel", "parallel"),
        name="fx_attention",
    )(qkv3, qkv3, qkv3, cum_row, cum_col)
    return out.reshape(bsz * seq, W_MIX)


def _gm_kernel(p_ref, w_ref, b_ref, o_ref):
    hid = jax.nn.gelu(p_ref[...])
    r = _iota2((GM_CHUNK, GM_CHUNK), 0)
    c = _iota2((GM_CHUNK, GM_CHUNK), 1)
    lower = c <= r
    for g in range(N_HEADS):
        u = hid[:, g * HEAD_DIM:(g + 1) * HEAD_DIM]
        v = hid[:, W_MIX + g * HEAD_DIM:W_MIX + (g + 1) * HEAD_DIM]
        mean = jnp.mean(v, axis=-1, keepdims=True)
        var = jnp.mean(jnp.square(v - mean), axis=-1, keepdims=True)
        vn = (v - mean) * lax.rsqrt(var + LN_EPS)
        w = jnp.where(lower, w_ref[g], 0.0).astype(BF16)
        mixed = _dot(w, vn.astype(BF16)) + b_ref[:, g:g + 1]
        o_ref[:, g * HEAD_DIM:(g + 1) * HEAD_DIM] = (u * mixed).astype(o_ref.dtype)


def _gm_mix(p_gm, w_s, b_col):
    n = p_gm.shape[0]
    return pl.pallas_call(
        _gm_kernel,
        grid=(n // GM_CHUNK,),
        in_specs=[
            pl.BlockSpec((GM_CHUNK, 2 * W_MIX), lambda i: (i, 0)),
            pl.BlockSpec((N_HEADS, GM_CHUNK, GM_CHUNK), lambda i: (0, 0, 0)),
            pl.BlockSpec((GM_CHUNK, N_HEADS), lambda i: (0, 0)),
        ],
        out_specs=pl.BlockSpec((GM_CHUNK, W_MIX), lambda i: (i, 0)),
        out_shape=jax.ShapeDtypeStruct((n, W_MIX), BF16),
        compiler_params=_cparams("parallel"),
        name="gm_mix",
    )(p_gm, w_s, b_col)


def _head_sum(x, ones_bd):
    return _dot(x, ones_bd, precision=HIGHEST)


def _rw_prep_kernel(p_ref, prev_ref, mu_ref, w0_ref, w2_ref, a0_ref, a2_ref, g2_ref,
                    kk_ref, ka_ref, rk_ref, bd_ref,
                    r_out, ld_out, k_out, v_out, kk_out, ba_out, bonus_out, g_out, *, seq):
    i = pl.program_id(0)
    tm = p_ref.shape[0]
    p = p_ref[...]
    first = (i * tm) % seq == 0
    prev_row = jnp.where(first, 0.0, prev_ref[SUBLANES - 1:SUBLANES, :])
    rows = _iota2(p.shape, 0)
    prev = jnp.where(rows == 0, prev_row, pltpu.roll(p, 1, axis=0))
    p = p + (prev - p) * mu_ref[...]
    r = p[:, 0:W_MIX]
    k = p[:, W_MIX:2 * W_MIX]
    v = p[:, 2 * W_MIX:3 * W_MIX]
    o = 3 * W_MIX
    xw = p[:, o:o + RW_DECAY_LORA]
    o += RW_DECAY_LORA
    xa = p[:, o:o + RW_AAA_LORA]
    o += RW_AAA_LORA
    xg = p[:, o:o + RW_GATE_LORA]
    wpre = -(w0_ref[...] + _dot(jnp.tanh(xw).astype(BF16), w2_ref[...]))
    w = -(jnp.maximum(wpre, 0.0) + jnp.log(1.0 + jnp.exp(-jnp.abs(wpre)))) - 0.5
    a = jax.nn.sigmoid(a0_ref[...] + _dot(xa.astype(BF16), a2_ref[...]))
    g = _dot(jax.nn.sigmoid(xg).astype(BF16), g2_ref[...])
    kk = k * kk_ref[...]
    nrm = jnp.maximum(jnp.sqrt(_head_sum(kk * kk, bd_ref[...])), L2_EPS)
    kk = kk / nrm
    k2 = k * (1.0 + (a - 1.0) * ka_ref[...])
    r_out[...] = r
    ld_out[...] = -jnp.exp(w)
    k_out[...] = k2
    v_out[...] = v
    kk_out[...] = kk
    ba_out[...] = kk * a
    bonus_out[...] = _head_sum(r * k2 * rk_ref[...], bd_ref[...]) * v
    g_out[...] = g


def _rw_prep(p_rw, seq, mu, w0, w2, a0, a2, g2, k_k, k_a, r_k, ones_bd):
    n = p_rw.shape[0]
    tm = min(256, seq)
    vec = lambda width: pl.BlockSpec((1, width), lambda i: (0, 0))
    mat = lambda rws: pl.BlockSpec((rws, W_MIX), lambda i: (0, 0))
    out_spec = pl.BlockSpec((tm, W_MIX), lambda i: (i, 0))
    return pl.pallas_call(
        functools.partial(_rw_prep_kernel, seq=seq),
        grid=(n // tm,),
        in_specs=[
            pl.BlockSpec((tm, RW_COLS), lambda i: (i, 0)),
            pl.BlockSpec((SUBLANES, RW_COLS),
                         lambda i: (jnp.maximum(i * (tm // SUBLANES) - 1, 0), 0)),
            vec(RW_COLS), vec(W_MIX), mat(RW_DECAY_LORA), vec(W_MIX), mat(RW_AAA_LORA),
            mat(RW_GATE_LORA), vec(W_MIX), vec(W_MIX), vec(W_MIX), mat(W_MIX),
        ],
        out_specs=[out_spec] * 8,
        out_shape=[jax.ShapeDtypeStruct((n, W_MIX), F32)] * 8,
        compiler_params=_cparams("parallel"),
        name="rw_prep",
    )(p_rw, p_rw, mu, w0, w2, a0, a2, g2, k_k, k_a, r_k, ones_bd)


_NN = (((1,), (0,)), ((), ()))
_NT = (((1,), (1,)), ((), ()))


def _split2(x):
    hi = x.astype(BF16)
    return hi, (x - hi.astype(F32)).astype(BF16)


def _mm_lhs2(a, b, dims):
    m = a.shape[0]
    hi, lo = _split2(a)
    r = lax.dot_general(jnp.concatenate([hi, lo], axis=0), b, dims, preferred_element_type=F32)
    return r[:m] + r[m:]


def _lane_head(shape, width):
    return lax.shift_right_logical(_iota2(shape, 1), width.bit_length() - 1)


def _head_blocks(x, width):
    lh = _lane_head(x.shape, width)
    return jnp.concatenate([jnp.where(lh == h, x, jnp.zeros_like(x)) for h in range(N_HEADS)],
                           axis=0)


def _head_diag(full, width):
    rows = full.shape[0] // N_HEADS
    lh = _lane_head((rows, full.shape[1]), width)
    out = jnp.zeros((rows, full.shape[1]), F32)
    for h in range(N_HEADS):
        out = out + jnp.where(lh == h, full[h * rows:(h + 1) * rows], 0.0)
    return out


def _tn_lhs2(a, b_bf16):
    hi = a.astype(BF16).astype(F32)
    b = b_bf16.astype(F32)
    return _dot_tn(hi, b) + _dot_tn(a - hi, b)


def _rw_intra_kernel(r_ref, ld_ref, k_ref, v_ref, kk_ref, ba_ref,
                     w_out, u0_out, y0_out, rt_out, btp_out, mrb_out, g0_out, pc_out):
    cs = RW_CHUNK
    cw = N_HEADS * cs
    row = _iota2((cs, cw), 0)
    colc = jnp.bitwise_and(_iota2((cs, cw), 1), cs - 1)
    lower = colc <= row
    strict = colc < row
    eye = (colc == row).astype(F32)
    tri3 = (jnp.bitwise_and(_iota2((cs, 3 * cs), 1), cs - 1) <= _iota2((cs, 3 * cs), 0)).astype(BF16)

    chunks = range(r_ref.shape[1] // cs)
    rows = [slice(c * cs, (c + 1) * cs) for c in chunks]
    each = lambda f: [f(c) for c in chunks]
    hb = lambda x: _head_blocks(x.astype(BF16), HEAD_DIM)

    def running_log_decay(c):
        ld = ld_ref[0, rows[c], :]
        l1 = ld.astype(BF16)
        l2, l3 = _split2(ld - l1.astype(F32))
        return _dot(tri3, jnp.concatenate([l1, l2, l3], axis=0))

    cl = each(running_log_decay)
    p_in = each(lambda c: jnp.exp(cl[c]))
    p_inv = each(lambda c: jnp.exp(-cl[c]))
    at = each(lambda c: -kk_ref[0, rows[c], :] * jnp.exp(cl[c] - ld_ref[0, rows[c], :]))
    bt = each(lambda c: ba_ref[0, rows[c], :] * p_inv[c])
    kt = each(lambda c: k_ref[0, rows[c], :] * p_inv[c])
    rt = each(lambda c: r_ref[0, rows[c], :] * p_in[c])
    pc = each(lambda c: p_in[c][cs - 1:cs, :])

    lhs = each(lambda c: jnp.concatenate([at[c], rt[c]], axis=0))
    ab_mb = each(lambda c: _mm_lhs2(lhs[c], hb(bt[c]), _NT))
    ak_mk = each(lambda c: _mm_lhs2(lhs[c], hb(kt[c]), _NT))
    a_ab = each(lambda c: jnp.where(strict, ab_mb[c][:cs], 0.0))
    m_rb = each(lambda c: jnp.where(lower, ab_mb[c][cs:], 0.0))
    a_ak = each(lambda c: jnp.where(strict, ak_mk[c][:cs], 0.0))
    m_rk = each(lambda c: jnp.where(lower, ak_mk[c][cs:], 0.0))

    inv = each(lambda c: eye + a_ab[c])
    pw = a_ab
    pw_blocks = each(lambda c: _head_blocks(pw[c].astype(BF16), cs))
    steps = 1
    while steps * 2 < cs:
        pw = each(lambda c: _mm_lhs2(pw[c], pw_blocks[c], _NN))
        pw_blocks = each(lambda c: _head_blocks(pw[c].astype(BF16), cs))
        inv = each(lambda c: inv[c] + _mm_lhs2(inv[c], pw_blocks[c], _NN))
        steps *= 2

    akv_mkv = each(lambda c: _mm_lhs2(jnp.concatenate([a_ak[c], m_rk[c]], axis=0),
                                      hb(v_ref[0, rows[c], :]), _NN))
    wu = each(lambda c: _mm_lhs2(
        inv[c], jnp.concatenate([hb(at[c]), hb(akv_mkv[c][:cs])], axis=1), _NN))
    g0 = each(lambda c: _head_diag(_tn_lhs2(v_ref[0, rows[c], :], kt[c].astype(BF16)), HEAD_DIM))
    for c in chunks:
        w_out[0, rows[c], :] = wu[c][:, :W_MIX]
        u0_out[0, rows[c], :] = wu[c][:, W_MIX:]
        y0_out[0, rows[c], :] = akv_mkv[c][cs:]
        rt_out[0, rows[c], :] = rt[c]
        btp_out[0, rows[c], :] = bt[c] * pc[c]
        mrb_out[0, rows[c], :] = m_rb[c]
        g0_out[0, rows[c], :] = g0[c] * pc[c]
        pc_out[0, c * SUBLANES:(c + 1) * SUBLANES, :] = jnp.broadcast_to(pc[c], (SUBLANES, W_MIX))


def _rw_intra(prep6, bsz, seq):
    arrs = [a.reshape(bsz, seq, W_MIX) for a in prep6]
    rows = RW_CHUNK * RW_CPS
    spec = pl.BlockSpec((1, rows, W_MIX), lambda b, c: (b, c, 0))
    pc_spec = pl.BlockSpec((1, SUBLANES * RW_CPS, W_MIX), lambda b, c: (b, c, 0))
    big = jax.ShapeDtypeStruct((bsz, seq, W_MIX), F32)
    return pl.pallas_call(
        _rw_intra_kernel,
        grid=(bsz, seq // rows),
        in_specs=[spec] * 6,
        out_specs=[spec] * 7 + [pc_spec],
        out_shape=[big] * 7 + [jax.ShapeDtypeStruct((bsz, seq // RW_CHUNK * SUBLANES, W_MIX), F32)],
        compiler_params=_cparams("parallel", "parallel"),
        name="rw_intra",
    )(*arrs)


def _rw_state_kernel(w_ref, u0_ref, y0_ref, rt_ref, btp_ref, mrb_ref, g0_ref, pc_ref,
                     bonus_ref, g_ref, gnw_ref, gnb_ref, ones_ref, o_ref, state_ref):
    cs = RW_CHUNK

    @pl.when(pl.program_id(1) == 0)
    def _():
        state_ref[...] = jnp.zeros_like(state_ref)

    batch = range(w_ref.shape[0])
    each = lambda f: [f(b) for b in batch]
    hb = lambda x: _head_blocks(x.astype(BF16), HEAD_DIM)
    s0 = each(lambda b: state_ref[b])
    wr = each(lambda b: _mm_lhs2(jnp.concatenate([w_ref[b], rt_ref[b]], axis=0), hb(s0[b]), _NT))
    u = each(lambda b: wr[b][:cs] + u0_ref[b])
    y = each(lambda b: wr[b][cs:] + _mm_lhs2(mrb_ref[b], hb(u[b]), _NN) + y0_ref[b])
    su = each(lambda b: _head_diag(_tn_lhs2(u[b], btp_ref[b].astype(BF16)), HEAD_DIM))
    for b in batch:
        state_ref[b] = s0[b] * pc_ref[b, 0:1, :] + g0_ref[b] + su[b]

    inv_n = 1.0 / HEAD_DIM
    mean = each(lambda b: _mm_lhs2(y[b], ones_ref[...], _NN) * inv_n)
    yc = each(lambda b: y[b] - mean[b])
    var = each(lambda b: _mm_lhs2(yc[b] * yc[b], ones_ref[...], _NN) * inv_n)
    for b in batch:
        yn = yc[b] * lax.rsqrt(var[b] + GN_EPS) * gnw_ref[...] + gnb_ref[...]
        o_ref[b] = ((yn + bonus_ref[b]) * g_ref[b]).astype(o_ref.dtype)


def _rw_state(intra, bonus, g, gn_w, gn_b, ones_bd, bsz, seq):
    gb = RW_BATCH_GROUP if bsz % RW_BATCH_GROUP == 0 else 1
    spec = pl.BlockSpec((gb, RW_CHUNK, W_MIX), lambda b, c: (b, c, 0))
    pc_spec = pl.BlockSpec((gb, SUBLANES, W_MIX), lambda b, c: (b, c, 0))
    vec = pl.BlockSpec((1, W_MIX), lambda b, c: (0, 0))
    out = pl.pallas_call(
        _rw_state_kernel,
        grid=(bsz // gb, seq // RW_CHUNK),
        in_specs=[spec] * 7 + [pc_spec, spec, spec, vec, vec,
                               pl.BlockSpec((W_MIX, W_MIX), lambda b, c: (0, 0))],
        out_specs=spec,
        out_shape=jax.ShapeDtypeStruct((bsz, seq, W_MIX), BF16),
        scratch_shapes=[pltpu.VMEM((gb, HEAD_DIM, W_MIX), F32)],
        compiler_params=_cparams("parallel", "arbitrary"),
        name="rw_state",
    )(*intra, bonus.reshape(bsz, seq, W_MIX), g.reshape(bsz, seq, W_MIX), gn_w, gn_b,
      ones_bd.astype(BF16))
    return out.reshape(bsz * seq, W_MIX)


def _store_token_rows(ref, val):
    tm, d = val.shape
    ch = d // LANES
    for s in range(ch):
        ref[pl.ds(s, tm, stride=ch), :] = val[:, s * LANES:(s + 1) * LANES]


def _load_token_rows(ref, tm, ch):
    return jnp.concatenate([ref[pl.ds(s, tm, stride=ch), :] for s in range(ch)], axis=1)


def _outproj_router_kernel(x_ref, sb_ref, gm_ref, rw_ref, fx_ref, wo_ref, g_ref, wr_ref, br_ref,
                           su_ref, x_out, h_out, idx_out, gate_out, rank_out, cnt_out, cnt_ref):
    mix = jnp.concatenate([sb_ref[...], gm_ref[...], rw_ref[...], fx_ref[...]], axis=1)
    x = x_ref[...] + _dot(mix, wo_ref[...])
    x_out[...] = x
    h = x * lax.rsqrt(jnp.mean(x * x, axis=-1, keepdims=True) + RMS_EPS) * g_ref[...]
    _store_token_rows(h_out, h)

    lg = _dot_nt(wr_ref[...], h, HIGHEST) + br_ref[...]
    tm = lg.shape[1]
    gl = [lg[g:g + 1, :] for g in range(N_EXPERT_GROUPS)]
    gmax = gl[0]
    gsel = jnp.zeros((1, tm), jnp.int32)
    for g in range(1, N_EXPERT_GROUPS):
        better = gl[g] > gmax
        gsel = jnp.where(better, g, gsel)
        gmax = jnp.where(better, gl[g], gmax)
    denom = gl[0] * 0.0
    for g in range(N_EXPERT_GROUPS):
        denom = denom + jnp.exp(gl[g] - gmax)
    g_gate = 1.0 / denom

    e0 = SUBLANES
    ing = lg[e0:e0 + EXPERTS_PER_GROUP, :]
    for g in range(1, N_EXPERT_GROUPS):
        ing = jnp.where(gsel == g, lg[e0 + g * EXPERTS_PER_GROUP:e0 + (g + 1) * EXPERTS_PER_GROUP, :], ing)
    ridx = _iota2(ing.shape, 0)
    m1 = jnp.max(ing, axis=0, keepdims=True)
    i1 = jnp.min(jnp.where(ing == m1, ridx, EXPERTS_PER_GROUP), axis=0, keepdims=True)
    rest = jnp.where(ridx == i1, -jnp.inf, ing)
    m2 = jnp.max(rest, axis=0, keepdims=True)
    i2 = jnp.min(jnp.where(rest == m2, ridx, EXPERTS_PER_GROUP), axis=0, keepdims=True)
    e2 = jnp.exp(m2 - m1)
    s = 1.0 + e2
    e_sel = (gsel * EXPERTS_PER_GROUP + i1, gsel * EXPERTS_PER_GROUP + i2)
    idx_out[0:1, :] = e_sel[0]
    idx_out[1:2, :] = e_sel[1]
    gate_out[0:1, :] = (1.0 / s) * g_gate
    gate_out[1:2, :] = (e2 / s) * g_gate

    @pl.when(pl.program_id(0) == 0)
    def _():
        cnt_ref[...] = jnp.zeros_like(cnt_ref)

    erow = _iota2((N_EXPERTS, tm), 0)
    onehot = [(erow == e).astype(F32) for e in e_sel]
    both = onehot[0] + onehot[1]
    before = cnt_ref[...] + _dot(both.astype(BF16), su_ref[...])
    for k in range(TOP_K):
        rank_out[k:k + 1, :] = jnp.sum(onehot[k] * before, axis=0, keepdims=True).astype(jnp.int32)
    cnt_ref[...] = cnt_ref[...] + jnp.sum(both, axis=1, keepdims=True)
    cnt_out[...] = jnp.broadcast_to(cnt_ref[...], cnt_out.shape).astype(jnp.int32)


def _outproj_router(x2, ys, w_out, g2, w_router_t, b_router):
    n, d = x2.shape
    tm = min(ROW_TILE, n)
    nr = w_router_t.shape[0]
    ymix = pl.BlockSpec((tm, W_MIX), lambda i: (i, 0))
    r = jnp.arange(tm)
    strict_upper = (r[:, None] < r[None, :]).astype(BF16)
    return pl.pallas_call(
        _outproj_router_kernel,
        grid=(n // tm,),
        in_specs=[
            pl.BlockSpec((tm, d), lambda i: (i, 0)),
            ymix, ymix, ymix, ymix,
            pl.BlockSpec((4 * W_MIX, d), lambda i: (0, 0)),
            pl.BlockSpec((1, d), lambda i: (0, 0)),
            pl.BlockSpec((nr, d), lambda i: (0, 0)),
            pl.BlockSpec((nr, 1), lambda i: (0, 0)),
            pl.BlockSpec((tm, tm), lambda i: (0, 0)),
        ],
        out_specs=[
            pl.BlockSpec((tm, d), lambda i: (i, 0)),
            pl.BlockSpec((tm * (d // LANES), LANES), lambda i: (i, 0)),
            pl.BlockSpec((TOP_K, tm), lambda i: (0, i)),
            pl.BlockSpec((TOP_K, tm), lambda i: (0, i)),
            pl.BlockSpec((TOP_K, tm), lambda i: (0, i)),
            pl.BlockSpec((N_EXPERTS, LANES), lambda i: (0, 0)),
        ],
        out_shape=[
            jax.ShapeDtypeStruct((n, d), F32),
            jax.ShapeDtypeStruct((n * (d // LANES), LANES), F32),
            jax.ShapeDtypeStruct((TOP_K, n), jnp.int32),
            jax.ShapeDtypeStruct((TOP_K, n), F32),
            jax.ShapeDtypeStruct((TOP_K, n), jnp.int32),
            jax.ShapeDtypeStruct((N_EXPERTS, LANES), jnp.int32),
        ],
        scratch_shapes=[pltpu.VMEM((N_EXPERTS, 1), F32)],
        compiler_params=_cparams("arbitrary"),
        name="outproj_router",
    )(x2, *ys, w_out, g2, w_router_t, b_router, strict_upper)


def _token_copy(src_hbm, dst, src_tok, dst_tok, sem, ch):
    return pltpu.make_async_copy(
        src_hbm.at[pl.ds(pl.multiple_of(src_tok * ch, ch), ch)],
        dst.at[pl.ds(pl.multiple_of(dst_tok * ch, ch), ch)], sem)


def _dispatch_kernel(pad_end_ref, slot_ref, h_ref, o_hbm, zbuf, zsem, sem, *, ch, tm):
    i = pl.program_id(0)

    def zero_copy(first_slot):
        return pltpu.make_async_copy(
            zbuf, o_hbm.at[pl.ds(pl.multiple_of(first_slot * ch, ch), MOE_TB * ch)], zsem)

    @pl.when(i == 0)
    def _():
        zbuf[...] = jnp.zeros_like(zbuf)
        tails = [jnp.maximum(pad_end_ref[e] - MOE_TB, 0) for e in range(N_EXPERTS)]
        for first in tails:
            zero_copy(first).start()
        for first in tails:
            zero_copy(first).wait()

        def unused_block(b, _):
            zero_copy(b * MOE_TB).start()
            zero_copy(b * MOE_TB).wait()
            return 0

        lax.fori_loop(pad_end_ref[N_EXPERTS - 1] // MOE_TB, o_hbm.shape[0] // (MOE_TB * ch),
                      unused_block, 0)

    def start(r, _):
        for k in range(TOP_K):
            _token_copy(h_ref, o_hbm, r, slot_ref[0, 0, r * TOP_K + k], sem, ch).start()
        return 0

    lax.fori_loop(0, tm, start, 0, unroll=DMA_ISSUE_UNROLL)
    for k in range(TOP_K):
        pltpu.make_async_copy(h_ref, o_hbm.at[pl.ds(0, tm * ch)], sem).wait()


def _dispatch(pad_end, slot_flat, h_rows, n_slots, ch):
    n_tok = h_rows.shape[0] // ch
    tm = min(256, n_tok)
    n_steps = n_tok // tm
    return pl.pallas_call(
        functools.partial(_dispatch_kernel, ch=ch, tm=tm),
        grid_spec=pltpu.PrefetchScalarGridSpec(
            num_scalar_prefetch=1,
            grid=(n_steps,),
            in_specs=[
                pl.BlockSpec((1, 1, tm * TOP_K), lambda i, pe: (i, 0, 0), memory_space=pltpu.SMEM),
                pl.BlockSpec((tm * ch, LANES), lambda i, pe: (i, 0)),
            ],
            out_specs=pl.BlockSpec(memory_space=pl.ANY),
            scratch_shapes=[pltpu.VMEM((MOE_TB * ch, LANES), F32), pltpu.SemaphoreType.DMA,
                            pltpu.SemaphoreType.DMA],
        ),
        out_shape=jax.ShapeDtypeStruct((n_slots * ch, LANES), h_rows.dtype),
        compiler_params=_cparams("arbitrary"),
        name="moe_dispatch",
    )(pad_end, slot_flat.reshape(n_steps, 1, tm * TOP_K), h_rows)


def _expert_kernel(be_ref, nb_ref, x_ref, wg_ref, wu_ref, wd_ref, o_ref, *, ch):
    i = pl.program_id(0)

    @pl.when(i < nb_ref[0])
    def _():
        x = _load_token_rows(x_ref, MOE_TB, ch).astype(BF16)
        a = _dot(x, wg_ref[0, 0].astype(BF16))
        u = _dot(x, wu_ref[0, 0].astype(BF16))
        hid = (a * jax.nn.sigmoid(a) * u).astype(BF16)
        _store_token_rows(o_ref, _dot(hid, wd_ref[0, 0].astype(BF16)))

    @pl.when(i >= nb_ref[0])
    def _():
        o_ref[...] = jnp.zeros_like(o_ref)


def _expert_blocks(block_expert, n_used, xb_rows, w_gate, w_up, w_down, layer):
    d, hid = w_gate.shape[2], w_gate.shape[3]
    ch = d // LANES
    n_slots = xb_rows.shape[0] // ch
    return pl.pallas_call(
        functools.partial(_expert_kernel, ch=ch),
        grid_spec=pltpu.PrefetchScalarGridSpec(
            num_scalar_prefetch=2,
            grid=(n_slots // MOE_TB,),
            in_specs=[
                pl.BlockSpec((MOE_TB * ch, LANES), lambda i, be, nb: (i, 0)),
                pl.BlockSpec((1, 1, d, hid), lambda i, be, nb: (layer, be[i], 0, 0)),
                pl.BlockSpec((1, 1, d, hid), lambda i, be, nb: (layer, be[i], 0, 0)),
                pl.BlockSpec((1, 1, hid, d), lambda i, be, nb: (layer, be[i], 0, 0)),
            ],
            out_specs=pl.BlockSpec((MOE_TB * ch, LANES), lambda i, be, nb: (i, 0)),
        ),
        out_shape=jax.ShapeDtypeStruct((n_slots * ch, LANES), F32),
        compiler_params=_cparams("arbitrary"),
        name="moe_experts",
    )(block_expert, n_used, xb_rows, w_gate, w_up, w_down)


def _combine_kernel(slot_ref, slot_next_ref, x_ref, gate_ref, yb_hbm, o_ref, buf, sem, *, ch):
    i = pl.program_id(0)
    last = pl.num_programs(0) - 1
    tm = x_ref.shape[0]

    def issue(idx_ref, parity):
        def start(r, _):
            for k in range(TOP_K):
                _token_copy(yb_hbm, buf.at[parity, k], idx_ref[0, 0, r * TOP_K + k], r,
                            sem.at[parity], ch).start()
            return 0

        lax.fori_loop(0, tm, start, 0, unroll=DMA_ISSUE_UNROLL)

    @pl.when(i == 0)
    def _():
        issue(slot_ref, 0)

    @pl.when(i < last)
    def _():
        issue(slot_next_ref, (i + 1) % 2)

    par = i % 2
    for k in range(TOP_K):
        pltpu.make_async_copy(yb_hbm.at[pl.ds(0, tm * ch)], buf.at[par, k], sem.at[par]).wait()
    y = (_load_token_rows(buf.at[par, 0], tm, ch) * gate_ref[:, 0:1]
         + _load_token_rows(buf.at[par, 1], tm, ch) * gate_ref[:, 1:2])
    o_ref[...] = x_ref[...] + y


def _combine(slot_flat, x2, gates, yb_rows):
    n, d = x2.shape
    ch = d // LANES
    tm = min(256, n)
    n_steps = n // tm
    slots3 = slot_flat.reshape(n_steps, 1, tm * TOP_K)
    return pl.pallas_call(
        functools.partial(_combine_kernel, ch=ch),
        grid=(n_steps,),
        in_specs=[
            pl.BlockSpec((1, 1, tm * TOP_K), lambda i: (i, 0, 0), memory_space=pltpu.SMEM),
            pl.BlockSpec((1, 1, tm * TOP_K), lambda i: (jnp.minimum(i + 1, n_steps - 1), 0, 0),
                         memory_space=pltpu.SMEM),
            pl.BlockSpec((tm, d), lambda i: (i, 0)),
            pl.BlockSpec((tm, TOP_K), lambda i: (i, 0)),
            pl.BlockSpec(memory_space=pl.ANY),
        ],
        out_specs=pl.BlockSpec((tm, d), lambda i: (i, 0)),
        scratch_shapes=[pltpu.VMEM((2, TOP_K, tm * ch, LANES), F32),
                        pltpu.SemaphoreType.DMA((2,))],
        out_shape=jax.ShapeDtypeStruct((n, d), F32),
        compiler_params=_cparams("arbitrary"),
        name="moe_combine",
    )(slots3, slots3, x2, gates, yb_rows)


def _routing_plan(expert_idx_t, rank_t, counts, n_tok):
    padded = (counts + MOE_TB - 1) // MOE_TB * MOE_TB
    pad_end = jnp.cumsum(padded)
    pad_start = pad_end - padded
    start_t = jnp.sum(jnp.where(expert_idx_t[:, :, None] == jnp.arange(N_EXPERTS), pad_start, 0),
                      axis=-1)
    slot = jnp.transpose(start_t + rank_t).reshape(n_tok * TOP_K)
    n_blocks = -(-(n_tok * TOP_K) // MOE_TB) + N_EXPERTS
    block_expert = jnp.minimum(
        jnp.searchsorted(pad_end, jnp.arange(n_blocks) * MOE_TB, side='right'),
        N_EXPERTS - 1).astype(jnp.int32)
    n_used = (pad_end[-1] // MOE_TB).astype(jnp.int32).reshape(1)
    return slot.astype(jnp.int32), pad_end.astype(jnp.int32), block_expert, n_used, n_blocks * MOE_TB


def _moe(x2, h2, expert_idx_t, gates_t, rank_t, counts, w_gate, w_up, w_down, layer):
    n_tok = x2.shape[0]
    slot, pad_end, block_expert, n_used, n_slots = _routing_plan(expert_idx_t, rank_t, counts, n_tok)
    xb = _dispatch(pad_end, slot, h2, n_slots, x2.shape[1] // LANES)
    yb = _expert_blocks(block_expert, n_used, xb, w_gate, w_up, w_down, layer)
    return _combine(slot, x2, jnp.transpose(gates_t), yb)


def _final_norm_kernel(x_ref, g_ref, o_ref):
    x = x_ref[...]
    o_ref[...] = x * lax.rsqrt(jnp.mean(x * x, axis=-1, keepdims=True) + RMS_EPS) * g_ref[...]


def _final_norm(x2, g):
    n, d = x2.shape
    tm = min(ROW_TILE, n)
    return pl.pallas_call(
        _final_norm_kernel,
        grid=(n // tm,),
        in_specs=[pl.BlockSpec((tm, d), lambda i: (i, 0)), pl.BlockSpec((1, d), lambda i: (0, 0))],
        out_specs=pl.BlockSpec((tm, d), lambda i: (i, 0)),
        out_shape=jax.ShapeDtypeStruct((n, d), F32),
        compiler_params=_cparams("parallel"),
        name="final_norm",
    )(x2, g)


def _layer_params(l, w_in, gm_b, rw_mu, rw_w0, rw_w2, rw_a0, rw_a2, rw_g2, rw_k_k, rw_k_a, rw_r_k,
                  rw_gn_w, rw_gn_b, fx_b_f, w_out, router_group_w, router_group_b,
                  router_expert_w, router_expert_b):
    d = w_in.shape[1]
    n_main = 3 * W_MIX + 2 * W_MIX + RW_COLS + 3 * W_MIX
    w = w_in[l]
    w_main = w[:, :n_main].astype(BF16)
    w_gate_t = jnp.zeros((SUBLANES, d), F32).at[:N_HEADS].set(w[:, n_main:n_main + N_HEADS].T)
    fx_bias = jnp.zeros((SUBLANES, 1), F32).at[:N_HEADS, 0].set(fx_b_f[l])
    wr = jnp.zeros((SUBLANES + N_EXPERTS, d), F32)
    wr = wr.at[:N_EXPERT_GROUPS].set(router_group_w[l].T).at[SUBLANES:].set(router_expert_w[l].T)
    br = jnp.zeros((SUBLANES + N_EXPERTS, 1), F32)
    br = br.at[:N_EXPERT_GROUPS, 0].set(router_group_b[l]).at[SUBLANES:, 0].set(router_expert_b[l])
    row = lambda a: a.reshape(1, -1)
    return dict(
        w_main=w_main, w_gate_t=w_gate_t.astype(BF16), fx_bias=fx_bias,
        gm_b=gm_b[l].T,
        mu=row(rw_mu[l]), w0=row(rw_w0[l]), w2=rw_w2[l].astype(BF16), a0=row(rw_a0[l]),
        a2=rw_a2[l].astype(BF16), g2=rw_g2[l].astype(BF16), k_k=row(rw_k_k[l]), k_a=row(rw_k_a[l]),
        r_k=row(rw_r_k[l]), gn_w=row(rw_gn_w[l]), gn_b=row(rw_gn_b[l]),
        w_out=w_out[l].astype(BF16), wr=wr, br=br,
    )


def kernel(x, norm1_g, w_in, gm_w_s, gm_b, rw_mu, rw_w0, rw_w2, rw_a0, rw_a2, rw_g2, rw_k_k, rw_k_a, rw_r_k, rw_gn_w, rw_gn_b, fx_b_f, w_out, norm2_g, router_group_w, router_group_b, router_expert_w, router_expert_b, exp_w_gate, exp_w_up, exp_w_down, final_norm_g):
    bsz, seq, d = x.shape
    depth = w_in.shape[0]
    assert seq % min(ATT_TQB, seq) == 0 and seq % RW_CHUNK == 0 and seq % (2 * LANES) == 0
    x2 = x.reshape(bsz * seq, d)
    hd = jnp.arange(W_MIX) // HEAD_DIM
    ones_bd = (hd[:, None] == hd[None, :]).astype(F32)
    for l in range(depth):
        lp = _layer_params(l, w_in, gm_b, rw_mu, rw_w0, rw_w2, rw_a0, rw_a2, rw_g2, rw_k_k, rw_k_a,
                           rw_r_k, rw_gn_w, rw_gn_b, fx_b_f, w_out, router_group_w, router_group_b,
                           router_expert_w, router_expert_b)
        p_sb, p_gm, p_rw, p_fx, grow = _norm_inproj(x2, norm1_g[l].reshape(1, d), lp["w_main"],
                                                    lp["w_gate_t"])
        y_sb = _sb_attention(p_sb, bsz, seq)
        y_gm = _gm_mix(p_gm, gm_w_s[l], lp["gm_b"])
        prep = _rw_prep(p_rw, seq, lp["mu"], lp["w0"], lp["w2"], lp["a0"], lp["a2"], lp["g2"],
                        lp["k_k"], lp["k_a"], lp["r_k"], ones_bd)
        intra = _rw_intra(prep[:6], bsz, seq)
        y_rw = _rw_state(intra, prep[6], prep[7], lp["gn_w"], lp["gn_b"], ones_bd, bsz, seq)
        grow3 = jnp.transpose(grow.reshape(SUBLANES, bsz, seq), (1, 0, 2))
        cum_row = _fx_cum(grow3, lp["fx_bias"])
        y_fx = _fx_attention(p_fx, cum_row, bsz, seq)
        x2, h2, eidx_t, gates_t, rank_t, cnt = _outproj_router(
            x2, (y_sb, y_gm, y_rw, y_fx), lp["w_out"], norm2_g[l].reshape(1, d), lp["wr"], lp["br"])
        x2 = _moe(x2, h2, eidx_t, gates_t, rank_t, cnt[:, 0], exp_w_gate, exp_w_up, exp_w_down, l)
    return _final_norm(x2, final_norm_g.reshape(1, d)).reshape(bsz, seq, d)
```

```python
import functools

import jax
import jax.numpy as jnp
from jax import lax
from jax.experimental import pallas as pl
from jax.experimental.pallas import tpu as pltpu

F32 = jnp.float32
BF16 = jnp.bfloat16
HIGHEST = lax.Precision.HIGHEST

HEAD_DIM = 64
N_HEADS = 4
W_MIX = N_HEADS * HEAD_DIM
GM_CHUNK = 128
GM_ROWS = 1024
RW_DECAY_LORA = 64
RW_AAA_LORA = 64
RW_GATE_LORA = 128
RW_COLS = 3 * W_MIX + RW_DECAY_LORA + RW_AAA_LORA + RW_GATE_LORA
N_EXPERT_GROUPS = 4
EXPERTS_PER_GROUP = 8
N_EXPERTS = N_EXPERT_GROUPS * EXPERTS_PER_GROUP
TOP_K = 2
RMS_EPS = 1e-6
LN_EPS = 1e-5
GN_EPS = 64e-5
L2_EPS = 1e-12
LOG2_E = 1.4426950408889634

V7X_VMEM_LIMIT_BYTES = 56 * 1024 * 1024
LANES = 128
SUBLANES = 8

ATT_TK = 128
ATT_TQB = 512
ATT_W = N_HEADS * LANES
RW_CHUNK = 64
RW_CPS = 4
RW_BATCH_GROUP = 4
ROW_TILE = 512
MOE_TB = 256
DMA_ISSUE_UNROLL = 8


def _cparams(*sem):
    return pltpu.CompilerParams(dimension_semantics=sem, vmem_limit_bytes=V7X_VMEM_LIMIT_BYTES)


def _dot(a, b, precision=None):
    return jnp.dot(a, b, preferred_element_type=F32, precision=precision)


def _dot_nt(a, b, precision=None):
    return lax.dot_general(a, b, (((1,), (1,)), ((), ())), preferred_element_type=F32,
                           precision=precision)


def _dot_tn(a, b, precision=None):
    return lax.dot_general(a, b, (((0,), (0,)), ((), ())), preferred_element_type=F32,
                           precision=precision)


def _iota2(shape, dim):
    return lax.broadcasted_iota(jnp.int32, shape, dim)


def _store_heads_padded(ref, col0, src, fill):
    low = _iota2((src.shape[0], LANES), 1) < HEAD_DIM
    for pair in range(N_HEADS // 2):
        slab = src[:, pair * LANES:(pair + 1) * LANES]
        swapped = pltpu.roll(slab, HEAD_DIM, axis=1)
        for odd, val in enumerate((slab, swapped)):
            c = col0 + (2 * pair + odd) * LANES
            ref[:, c:c + LANES] = jnp.where(low, val, fill).astype(ref.dtype)


def _norm_inproj_kernel(x_ref, g_ref, w_ref, wgt_ref, sb_ref, gm_ref, rw_ref, fx_ref, grow_ref):
    x = x_ref[...]
    h = x * lax.rsqrt(jnp.mean(x * x, axis=-1, keepdims=True) + RMS_EPS) * g_ref[...]
    hb = h.astype(BF16)
    p = _dot(hb, w_ref[...])
    scale = HEAD_DIM ** -0.5
    wp = N_HEADS * LANES
    o = 0
    _store_heads_padded(sb_ref, 0, p[:, o:o + W_MIX] * scale, 0.0)
    _store_heads_padded(sb_ref, wp, p[:, o + W_MIX:o + 2 * W_MIX], 0.0)
    _store_heads_padded(sb_ref, 2 * wp, p[:, o + 2 * W_MIX:o + 3 * W_MIX], 0.0)
    o += 3 * W_MIX
    gm_ref[...] = p[:, o:o + 2 * W_MIX]
    o += 2 * W_MIX
    rw_ref[...] = p[:, o:o + RW_COLS]
    o += RW_COLS
    _store_heads_padded(fx_ref, 0, p[:, o:o + W_MIX] * scale, 0.0)
    _store_heads_padded(fx_ref, wp, p[:, o + W_MIX:o + 2 * W_MIX], 0.0)
    _store_heads_padded(fx_ref, 2 * wp, p[:, o + 2 * W_MIX:o + 3 * W_MIX], 1.0)
    grow_ref[...] = _dot_nt(wgt_ref[...], hb)


def _norm_inproj(x2, g, w_main, w_gate_t):
    n, d = x2.shape
    tm = min(ROW_TILE, n)
    cols = w_main.shape[1]
    return pl.pallas_call(
        _norm_inproj_kernel,
        grid=(n // tm,),
        in_specs=[
            pl.BlockSpec((tm, d), lambda i: (i, 0)),
            pl.BlockSpec((1, d), lambda i: (0, 0)),
            pl.BlockSpec((d, cols), lambda i: (0, 0)),
            pl.BlockSpec((SUBLANES, d), lambda i: (0, 0)),
        ],
        out_specs=[
            pl.BlockSpec((tm, 3 * ATT_W), lambda i: (i, 0)),
            pl.BlockSpec((tm, 2 * W_MIX), lambda i: (i, 0)),
            pl.BlockSpec((tm, RW_COLS), lambda i: (i, 0)),
            pl.BlockSpec((tm, 3 * ATT_W), lambda i: (i, 0)),
            pl.BlockSpec((SUBLANES, tm), lambda i: (0, i)),
        ],
        out_shape=[
            jax.ShapeDtypeStruct((n, 3 * ATT_W), BF16),
            jax.ShapeDtypeStruct((n, 2 * W_MIX), F32),
            jax.ShapeDtypeStruct((n, RW_COLS), F32),
            jax.ShapeDtypeStruct((n, 3 * ATT_W), BF16),
            jax.ShapeDtypeStruct((SUBLANES, n), F32),
        ],
        compiler_params=_cparams("parallel"),
        name="norm_inproj",
    )(x2, g, w_main, w_gate_t)


def _sb_kernel(q_ref, k_ref, v_ref, tri_ref, o_ref, c_ref, acc_ref):
    qi = pl.program_id(1)
    tqb = q_ref.shape[1]
    hl = lambda h: slice(h * LANES, (h + 1) * LANES)

    def key_tile(j, off):
        r0 = 0 if off is None else off * ATT_TK
        rows = tqb - r0
        start = pl.multiple_of(j * ATT_TK, ATT_TK)
        if off is not None:
            strict = _iota2((rows, ATT_TK), 1) < _iota2((rows, ATT_TK), 0)
        zs, parts = [], []
        for h in range(N_HEADS):
            z = _dot_nt(q_ref[0, r0:, hl(h)], k_ref[0, pl.ds(start, ATT_TK), hl(h)])
            sp = jnp.maximum(z, 0.0) + jnp.log(1.0 + jnp.exp2(jnp.abs(z) * -LOG2_E))
            if off is not None:
                sp = jnp.where(strict, sp, 0.0)
            hi = sp.astype(BF16)
            lo = (sp - hi.astype(F32)).astype(BF16)
            zs.append(z)
            parts.append(jnp.concatenate([hi, lo], axis=1))
        rr = _dot(jnp.concatenate(parts, axis=0), tri_ref[...])
        for h in range(N_HEADS):
            rh = rr[h * rows:(h + 1) * rows]
            p = jnp.exp(zs[h] - rh[:, :ATT_TK] - c_ref[h, r0:, :])
            if off is not None:
                p = jnp.where(strict, p, 0.0)
            acc_ref[h, r0:, :] += _dot(p.astype(BF16), v_ref[0, pl.ds(start, ATT_TK), hl(h)])
            c_ref[h, r0:, :] += rh[:, ATT_TK:]

    c_ref[...] = jnp.zeros_like(c_ref)
    acc_ref[...] = jnp.zeros_like(acc_ref)
    n_diag = tqb // ATT_TK
    for off in reversed(range(n_diag)):
        key_tile(qi * n_diag + off, off)

    def body(jj, carry):
        key_tile(qi * n_diag - 1 - jj, None)
        return carry

    lax.fori_loop(0, qi * n_diag, body, 0)
    for h in range(N_HEADS):
        o_ref[0, :, h * HEAD_DIM:(h + 1) * HEAD_DIM] = acc_ref[h, :, :HEAD_DIM].astype(o_ref.dtype)


def _sb_attention(qkv, bsz, seq):
    qkv3 = qkv.reshape(bsz, seq, 3 * ATT_W)
    r = jnp.arange(ATT_TK)
    tri = (r[:, None] >= r[None, :]).astype(BF16)
    blk = jnp.concatenate([tri, jnp.ones((ATT_TK, ATT_TK), BF16)], axis=1)
    tri2 = jnp.concatenate([blk, blk], axis=0)
    tqb = min(ATT_TQB, seq)
    out = pl.pallas_call(
        _sb_kernel,
        grid=(bsz, seq // tqb),
        in_specs=[
            pl.BlockSpec((1, tqb, ATT_W), lambda b, i: (b, i, 0)),
            pl.BlockSpec((1, seq, ATT_W), lambda b, i: (b, 0, 1)),
            pl.BlockSpec((1, seq, ATT_W), lambda b, i: (b, 0, 2)),
            pl.BlockSpec((2 * ATT_TK, 2 * ATT_TK), lambda b, i: (0, 0)),
        ],
        out_specs=pl.BlockSpec((1, tqb, W_MIX), lambda b, i: (b, i, 0)),
        out_shape=jax.ShapeDtypeStruct((bsz, seq, W_MIX), BF16),
        scratch_shapes=[pltpu.VMEM((N_HEADS, tqb, ATT_TK), F32),
                        pltpu.VMEM((N_HEADS, tqb, LANES), F32)],
        compiler_params=_cparams("parallel", "parallel"),
        name="sb_attention",
    )(qkv3, qkv3, qkv3, tri2)
    return out.reshape(bsz * seq, W_MIX)


def _fx_cum_kernel(g_ref, b_ref, cum_ref):
    seq = g_ref.shape[2]
    blk = 2 * LANES
    r = _iota2((blk, blk), 0)
    c = _iota2((blk, blk), 1)
    tri = (r <= c).astype(F32)
    carry = jnp.zeros((SUBLANES, 1), F32)
    for s in range(seq // blk):
        x = g_ref[0, :, s * blk:(s + 1) * blk] + b_ref[...]
        lf = -(jnp.maximum(-x, 0.0) + jnp.log(1.0 + jnp.exp(-jnp.abs(x))))
        cs = _dot(lf, tri, precision=HIGHEST) + carry
        cum_ref[0, :, s * blk:(s + 1) * blk] = cs
        carry = cs[:, blk - 1:blk]


def _fx_cum(grow3, bias_col):
    bsz, _, seq = grow3.shape
    return pl.pallas_call(
        _fx_cum_kernel,
        grid=(bsz,),
        in_specs=[
            pl.BlockSpec((1, SUBLANES, seq), lambda b: (b, 0, 0)),
            pl.BlockSpec((SUBLANES, 1), lambda b: (0, 0)),
        ],
        out_specs=pl.BlockSpec((1, SUBLANES, seq), lambda b: (b, 0, 0)),
        out_shape=jax.ShapeDtypeStruct((bsz, SUBLANES, seq), F32),
        compiler_params=_cparams("parallel"),
        name="fx_cum",
    )(grow3, bias_col)


def _fx_kernel(q_ref, k_ref, v_ref, crow_ref, ccol_ref, o_ref, m_ref, acc_ref, ct_ref):
    qi = pl.program_id(1)
    tqb = q_ref.shape[1]
    hl = lambda h: slice(h * LANES, (h + 1) * LANES)

    def key_tile(j, off):
        r0 = 0 if off is None else off * ATT_TK
        rows = tqb - r0
        start = pl.multiple_of(j * ATT_TK, ATT_TK)
        if off is not None:
            causal = _iota2((rows, ATT_TK), 1) <= _iota2((rows, ATT_TK), 0)
        heads = range(N_HEADS)
        z = [_dot_nt(q_ref[0, r0:, hl(h)], k_ref[0, pl.ds(start, ATT_TK), hl(h)]) for h in heads]
        ps, alphas = [], []
        for h in heads:
            x = z[h] + ct_ref[h, r0:, :] - crow_ref[0, h:h + 1, pl.ds(start, ATT_TK)]
            if off is not None:
                x = jnp.where(causal, x, -jnp.inf)
            m_old = m_ref[h, r0:, :]
            m_new = jnp.maximum(m_old, jnp.max(x, axis=-1, keepdims=True))
            m_ref[h, r0:, :] = m_new
            ps.append(jnp.exp(x - m_new).astype(BF16))
            alphas.append(jnp.exp(m_old - m_new))
        pv = [_dot(ps[h], v_ref[0, pl.ds(start, ATT_TK), hl(h)]) for h in heads]
        for h in heads:
            acc_ref[h, r0:, :] = alphas[h] * acc_ref[h, r0:, :] + pv[h]

    m_ref[...] = jnp.full_like(m_ref, -jnp.inf)
    acc_ref[...] = jnp.zeros_like(acc_ref)
    for h in range(N_HEADS):
        ct_ref[h] = jnp.broadcast_to(ccol_ref[0, :, h:h + 1], (tqb, LANES))
    n_diag = tqb // ATT_TK

    def body(j, carry):
        key_tile(j, None)
        return carry

    lax.fori_loop(0, qi * n_diag, body, 0)
    for off in range(n_diag):
        key_tile(qi * n_diag + off, off)
    for h in range(N_HEADS):
        acc = acc_ref[h]
        den = pltpu.roll(acc, HEAD_DIM, axis=1)
        o_ref[0, :, h * HEAD_DIM:(h + 1) * HEAD_DIM] = (acc / den)[:, :HEAD_DIM].astype(o_ref.dtype)


def _fx_attention(qkv, cum_row, bsz, seq):
    qkv3 = qkv.reshape(bsz, seq, 3 * ATT_W)
    cum_col = jnp.transpose(cum_row, (0, 2, 1))
    tqb = min(ATT_TQB, seq)
    out = pl.pallas_call(
        _fx_kernel,
        grid=(bsz, seq // tqb),
        in_specs=[
            pl.BlockSpec((1, tqb, ATT_W), lambda b, i: (b, i, 0)),
            pl.BlockSpec((1, seq, ATT_W), lambda b, i: (b, 0, 1)),
            pl.BlockSpec((1, seq, ATT_W), lambda b, i: (b, 0, 2)),
            pl.BlockSpec((1, SUBLANES, seq), lambda b, i: (b, 0, 0)),
            pl.BlockSpec((1, tqb, SUBLANES), lambda b, i: (b, i, 0)),
        ],
        out_specs=pl.BlockSpec((1, tqb, W_MIX), lambda b, i: (b, i, 0)),
        out_shape=jax.ShapeDtypeStruct((bsz, seq, W_MIX), BF16),
        scratch_shapes=[pltpu.VMEM((N_HEADS, tqb, LANES), F32)] * 3,
        compiler_params=_cparams("parallel", "parallel"),
        name="fx_attention",
    )(qkv3, qkv3, qkv3, cum_row, cum_col)
    return out.reshape(bsz * seq, W_MIX)


def _gm_kernel(p_ref, w_ref, b_ref, ones_ref, o_ref):
    hid = jax.nn.gelu(p_ref[...])
    u = hid[:, :W_MIX]
    v = hid[:, W_MIX:]
    inv_n = 1.0 / HEAD_DIM
    mean = _mm_lhs2(v, ones_ref[...], _NN) * inv_n
    vc = v - mean
    var = _mm_lhs2(vc * vc, ones_ref[...], _NN) * inv_n
    vn = (vc * lax.rsqrt(var + LN_EPS)).astype(BF16)
    wshape = (GM_CHUNK, N_HEADS * GM_CHUNK)
    lower = jnp.bitwise_and(_iota2(wshape, 1), GM_CHUNK - 1) <= _iota2(wshape, 0)
    w = jnp.where(lower, w_ref[...], 0.0).astype(BF16)
    chunks = range(p_ref.shape[0] // GM_CHUNK)
    rows = [slice(c * GM_CHUNK, (c + 1) * GM_CHUNK) for c in chunks]
    mixed = [_dot(w, _head_blocks(vn[rows[c]], HEAD_DIM)) for c in chunks]
    for c in chunks:
        o_ref[rows[c], :] = (u[rows[c]] * (mixed[c] + b_ref[...])).astype(o_ref.dtype)


def _gm_mix(p_gm, w_cat, b_full, ones_bd):
    n = p_gm.shape[0]
    tm = min(GM_ROWS, n)
    return pl.pallas_call(
        _gm_kernel,
        grid=(n // tm,),
        in_specs=[
            pl.BlockSpec((tm, 2 * W_MIX), lambda i: (i, 0)),
            pl.BlockSpec((GM_CHUNK, N_HEADS * GM_CHUNK), lambda i: (0, 0)),
            pl.BlockSpec((GM_CHUNK, W_MIX), lambda i: (0, 0)),
            pl.BlockSpec((W_MIX, W_MIX), lambda i: (0, 0)),
        ],
        out_specs=pl.BlockSpec((tm, W_MIX), lambda i: (i, 0)),
        out_shape=jax.ShapeDtypeStruct((n, W_MIX), BF16),
        compiler_params=_cparams("parallel"),
        name="gm_mix",
    )(p_gm, w_cat, b_full, ones_bd)


def _head_sum(x, ones_bd):
    return _mm_lhs2(x, ones_bd, _NN)


def _rw_prep_kernel(p_ref, prev_ref, mu_ref, w0_ref, w2_ref, a0_ref, a2_ref, g2_ref,
                    kk_ref, ka_ref, rk_ref, bd_ref,
                    r_out, ld_out, k_out, v_out, kk_out, ba_out, bonus_out, g_out, *, seq):
    i = pl.program_id(0)
    tm = p_ref.shape[0]
    p = p_ref[...]
    first = (i * tm) % seq == 0
    prev_row = jnp.where(first, 0.0, prev_ref[SUBLANES - 1:SUBLANES, :])
    rows = _iota2(p.shape, 0)
    prev = jnp.where(rows == 0, prev_row, pltpu.roll(p, 1, axis=0))
    p = p + (prev - p) * mu_ref[...]
    r = p[:, 0:W_MIX]
    k = p[:, W_MIX:2 * W_MIX]
    v = p[:, 2 * W_MIX:3 * W_MIX]
    o = 3 * W_MIX
    xw = p[:, o:o + RW_DECAY_LORA]
    o += RW_DECAY_LORA
    xa = p[:, o:o + RW_AAA_LORA]
    o += RW_AAA_LORA
    xg = p[:, o:o + RW_GATE_LORA]
    wpre = -(w0_ref[...] + _dot(jnp.tanh(xw).astype(BF16), w2_ref[...]))
    w = -(jnp.maximum(wpre, 0.0) + jnp.log(1.0 + jnp.exp(-jnp.abs(wpre)))) - 0.5
    a = jax.nn.sigmoid(a0_ref[...] + _dot(xa.astype(BF16), a2_ref[...]))
    g = _dot(jax.nn.sigmoid(xg).astype(BF16), g2_ref[...])
    kk = k * kk_ref[...]
    nrm = jnp.maximum(jnp.sqrt(_head_sum(kk * kk, bd_ref[...])), L2_EPS)
    kk = kk / nrm
    k2 = k * (1.0 + (a - 1.0) * ka_ref[...])
    r_out[...] = r
    ld_out[...] = -jnp.exp(w)
    k_out[...] = k2
    v_out[...] = v
    kk_out[...] = kk
    ba_out[...] = kk * a
    bonus_out[...] = _head_sum(r * k2 * rk_ref[...], bd_ref[...]) * v
    g_out[...] = g


def _rw_prep(p_rw, seq, mu, w0, w2, a0, a2, g2, k_k, k_a, r_k, ones_bd):
    n = p_rw.shape[0]
    tm = min(256, seq)
    vec = lambda width: pl.BlockSpec((1, width), lambda i: (0, 0))
    mat = lambda rws: pl.BlockSpec((rws, W_MIX), lambda i: (0, 0))
    out_spec = pl.BlockSpec((tm, W_MIX), lambda i: (i, 0))
    return pl.pallas_call(
        functools.partial(_rw_prep_kernel, seq=seq),
        grid=(n // tm,),
        in_specs=[
            pl.BlockSpec((tm, RW_COLS), lambda i: (i, 0)),
            pl.BlockSpec((SUBLANES, RW_COLS),
                         lambda i: (jnp.maximum(i * (tm // SUBLANES) - 1, 0), 0)),
            vec(RW_COLS), vec(W_MIX), mat(RW_DECAY_LORA), vec(W_MIX), mat(RW_AAA_LORA),
            mat(RW_GATE_LORA), vec(W_MIX), vec(W_MIX), vec(W_MIX), mat(W_MIX),
        ],
        out_specs=[out_spec] * 8,
        out_shape=[jax.ShapeDtypeStruct((n, W_MIX), F32)] * 8,
        compiler_params=_cparams("parallel"),
        name="rw_prep",
    )(p_rw, p_rw, mu, w0, w2, a0, a2, g2, k_k, k_a, r_k, ones_bd)


_NN = (((1,), (0,)), ((), ()))
_NT = (((1,), (1,)), ((), ()))


def _split2(x):
    hi = x.astype(BF16)
    return hi, (x - hi.astype(F32)).astype(BF16)


def _mm_lhs2(a, b, dims):
    m = a.shape[0]
    hi, lo = _split2(a)
    r = lax.dot_general(jnp.concatenate([hi, lo], axis=0), b, dims, preferred_element_type=F32)
    return r[:m] + r[m:]


def _lane_head(shape, width):
    return lax.shift_right_logical(_iota2(shape, 1), width.bit_length() - 1)


def _head_blocks(x, width):
    lh = _lane_head(x.shape, width)
    return jnp.concatenate([jnp.where(lh == h, x, jnp.zeros_like(x)) for h in range(N_HEADS)],
                           axis=0)


def _head_diag(full, width):
    rows = full.shape[0] // N_HEADS
    lh = _lane_head((rows, full.shape[1]), width)
    out = jnp.zeros((rows, full.shape[1]), F32)
    for h in range(N_HEADS):
        out = out + jnp.where(lh == h, full[h * rows:(h + 1) * rows], 0.0)
    return out


def _tn_lhs2(a, b_bf16):
    hi = a.astype(BF16).astype(F32)
    b = b_bf16.astype(F32)
    return _dot_tn(hi, b) + _dot_tn(a - hi, b)


def _rw_intra_kernel(r_ref, ld_ref, k_ref, v_ref, kk_ref, ba_ref,
                     w_out, u0_out, y0_out, rt_out, btp_out, mrb_out, g0_out, pc_out):
    cs = RW_CHUNK
    cw = N_HEADS * cs
    row = _iota2((cs, cw), 0)
    colc = jnp.bitwise_and(_iota2((cs, cw), 1), cs - 1)
    lower = colc <= row
    strict = colc < row
    eye = (colc == row).astype(F32)
    tri3 = (jnp.bitwise_and(_iota2((cs, 3 * cs), 1), cs - 1) <= _iota2((cs, 3 * cs), 0)).astype(BF16)

    chunks = range(r_ref.shape[1] // cs)
    rows = [slice(c * cs, (c + 1) * cs) for c in chunks]
    each = lambda f: [f(c) for c in chunks]
    hb = lambda x: _head_blocks(x.astype(BF16), HEAD_DIM)

    def running_log_decay(c):
        ld = ld_ref[0, rows[c], :]
        l1 = ld.astype(BF16)
        l2, l3 = _split2(ld - l1.astype(F32))
        return _dot(tri3, jnp.concatenate([l1, l2, l3], axis=0))

    cl = each(running_log_decay)
    p_in = each(lambda c: jnp.exp(cl[c]))
    p_inv = each(lambda c: jnp.exp(-cl[c]))
    at = each(lambda c: -kk_ref[0, rows[c], :] * jnp.exp(cl[c] - ld_ref[0, rows[c], :]))
    bt = each(lambda c: ba_ref[0, rows[c], :] * p_inv[c])
    kt = each(lambda c: k_ref[0, rows[c], :] * p_inv[c])
    rt = each(lambda c: r_ref[0, rows[c], :] * p_in[c])
    pc = each(lambda c: p_in[c][cs - 1:cs, :])

    lhs = each(lambda c: jnp.concatenate([at[c], rt[c]], axis=0))
    ab_mb = each(lambda c: _mm_lhs2(lhs[c], hb(bt[c]), _NT))
    ak_mk = each(lambda c: _mm_lhs2(lhs[c], hb(kt[c]), _NT))
    a_ab = each(lambda c: jnp.where(strict, ab_mb[c][:cs], 0.0))
    m_rb = each(lambda c: jnp.where(lower, ab_mb[c][cs:], 0.0))
    a_ak = each(lambda c: jnp.where(strict, ak_mk[c][:cs], 0.0))
    m_rk = each(lambda c: jnp.where(lower, ak_mk[c][cs:], 0.0))

    inv = each(lambda c: eye + a_ab[c])
    pw = a_ab
    pw_blocks = each(lambda c: _head_blocks(pw[c].astype(BF16), cs))
    steps = 1
    while steps * 2 < cs:
        pw = each(lambda c: _mm_lhs2(pw[c], pw_blocks[c], _NN))
        pw_blocks = each(lambda c: _head_blocks(pw[c].astype(BF16), cs))
        inv = each(lambda c: inv[c] + _mm_lhs2(inv[c], pw_blocks[c], _NN))
        steps *= 2

    akv_mkv = each(lambda c: _mm_lhs2(jnp.concatenate([a_ak[c], m_rk[c]], axis=0),
                                      hb(v_ref[0, rows[c], :]), _NN))
    wu = each(lambda c: _mm_lhs2(
        inv[c], jnp.concatenate([hb(at[c]), hb(akv_mkv[c][:cs])], axis=1), _NN))
    g0 = each(lambda c: _head_diag(_tn_lhs2(v_ref[0, rows[c], :], kt[c].astype(BF16)), HEAD_DIM))
    for c in chunks:
        w_out[0, rows[c], :] = wu[c][:, :W_MIX]
        u0_out[0, rows[c], :] = wu[c][:, W_MIX:]
        y0_out[0, rows[c], :] = akv_mkv[c][cs:]
        rt_out[0, rows[c], :] = rt[c]
        btp_out[0, rows[c], :] = bt[c] * pc[c]
        mrb_out[0, rows[c], :] = m_rb[c]
        g0_out[0, rows[c], :] = g0[c] * pc[c]
        pc_out[0, c * SUBLANES:(c + 1) * SUBLANES, :] = jnp.broadcast_to(pc[c], (SUBLANES, W_MIX))


def _rw_intra(prep6, bsz, seq):
    arrs = [a.reshape(bsz, seq, W_MIX) for a in prep6]
    rows = RW_CHUNK * RW_CPS
    spec = pl.BlockSpec((1, rows, W_MIX), lambda b, c: (b, c, 0))
    pc_spec = pl.BlockSpec((1, SUBLANES * RW_CPS, W_MIX), lambda b, c: (b, c, 0))
    big = jax.ShapeDtypeStruct((bsz, seq, W_MIX), F32)
    return pl.pallas_call(
        _rw_intra_kernel,
        grid=(bsz, seq // rows),
        in_specs=[spec] * 6,
        out_specs=[spec] * 7 + [pc_spec],
        out_shape=[big] * 7 + [jax.ShapeDtypeStruct((bsz, seq // RW_CHUNK * SUBLANES, W_MIX), F32)],
        compiler_params=_cparams("parallel", "parallel"),
        name="rw_intra",
    )(*arrs)


def _rw_state_kernel(w_ref, u0_ref, y0_ref, rt_ref, btp_ref, mrb_ref, g0_ref, pc_ref,
                     bonus_ref, g_ref, gnw_ref, gnb_ref, ones_ref, o_ref, state_ref):
    cs = RW_CHUNK

    @pl.when(pl.program_id(1) == 0)
    def _():
        state_ref[...] = jnp.zeros_like(state_ref)

    batch = range(w_ref.shape[0])
    each = lambda f: [f(b) for b in batch]
    hb = lambda x: _head_blocks(x.astype(BF16), HEAD_DIM)
    s0 = each(lambda b: state_ref[b])
    wr = each(lambda b: _mm_lhs2(jnp.concatenate([w_ref[b], rt_ref[b]], axis=0), hb(s0[b]), _NT))
    u = each(lambda b: wr[b][:cs] + u0_ref[b])
    y = each(lambda b: wr[b][cs:] + _mm_lhs2(mrb_ref[b], hb(u[b]), _NN) + y0_ref[b])
    su = each(lambda b: _head_diag(_tn_lhs2(u[b], btp_ref[b].astype(BF16)), HEAD_DIM))
    for b in batch:
        state_ref[b] = s0[b] * pc_ref[b, 0:1, :] + g0_ref[b] + su[b]

    inv_n = 1.0 / HEAD_DIM
    mean = each(lambda b: _mm_lhs2(y[b], ones_ref[...], _NN) * inv_n)
    yc = each(lambda b: y[b] - mean[b])
    var = each(lambda b: _mm_lhs2(yc[b] * yc[b], ones_ref[...], _NN) * inv_n)
    for b in batch:
        yn = yc[b] * lax.rsqrt(var[b] + GN_EPS) * gnw_ref[...] + gnb_ref[...]
        o_ref[b] = ((yn + bonus_ref[b]) * g_ref[b]).astype(o_ref.dtype)


def _rw_state(intra, bonus, g, gn_w, gn_b, ones_bd, bsz, seq):
    gb = RW_BATCH_GROUP if bsz % RW_BATCH_GROUP == 0 else 1
    spec = pl.BlockSpec((gb, RW_CHUNK, W_MIX), lambda b, c: (b, c, 0))
    pc_spec = pl.BlockSpec((gb, SUBLANES, W_MIX), lambda b, c: (b, c, 0))
    vec = pl.BlockSpec((1, W_MIX), lambda b, c: (0, 0))
    out = pl.pallas_call(
        _rw_state_kernel,
        grid=(bsz // gb, seq // RW_CHUNK),
        in_specs=[spec] * 7 + [pc_spec, spec, spec, vec, vec,
                               pl.BlockSpec((W_MIX, W_MIX), lambda b, c: (0, 0))],
        out_specs=spec,
        out_shape=jax.ShapeDtypeStruct((bsz, seq, W_MIX), BF16),
        scratch_shapes=[pltpu.VMEM((gb, HEAD_DIM, W_MIX), F32)],
        compiler_params=_cparams("parallel", "arbitrary"),
        name="rw_state",
    )(*intra, bonus.reshape(bsz, seq, W_MIX), g.reshape(bsz, seq, W_MIX), gn_w, gn_b,
      ones_bd.astype(BF16))
    return out.reshape(bsz * seq, W_MIX)


def _store_token_rows(ref, val):
    tm, d = val.shape
    ch = d // LANES
    for s in range(ch):
        ref[pl.ds(s, tm, stride=ch), :] = val[:, s * LANES:(s + 1) * LANES]


def _load_token_rows(ref, tm, ch):
    return jnp.concatenate([ref[pl.ds(s, tm, stride=ch), :] for s in range(ch)], axis=1)


def _outproj_router_kernel(x_ref, sb_ref, gm_ref, rw_ref, fx_ref, wo_ref, g_ref, wr_ref, br_ref,
                           su_ref, x_out, h_out, idx_out, gate_out, rank_out, cnt_out, cnt_ref):
    mix = jnp.concatenate([sb_ref[...], gm_ref[...], rw_ref[...], fx_ref[...]], axis=1)
    x = x_ref[...] + _dot(mix, wo_ref[...])
    x_out[...] = x
    h = x * lax.rsqrt(jnp.mean(x * x, axis=-1, keepdims=True) + RMS_EPS) * g_ref[...]
    _store_token_rows(h_out, h)

    nr = wr_ref.shape[0]
    h_hi, h_lo = _split2(h)
    w_hi, w_lo = _split2(wr_ref[...])
    part = _dot_nt(jnp.concatenate([w_hi, w_lo], axis=0), h_hi)
    lg = part[:nr] + part[nr:] + _dot_nt(w_hi, h_lo) + br_ref[...]
    tm = lg.shape[1]
    gl = [lg[g:g + 1, :] for g in range(N_EXPERT_GROUPS)]
    gmax = gl[0]
    gsel = jnp.zeros((1, tm), jnp.int32)
    for g in range(1, N_EXPERT_GROUPS):
        better = gl[g] > gmax
        gsel = jnp.where(better, g, gsel)
        gmax = jnp.where(better, gl[g], gmax)
    denom = gl[0] * 0.0
    for g in range(N_EXPERT_GROUPS):
        denom = denom + jnp.exp(gl[g] - gmax)
    g_gate = 1.0 / denom

    e0 = SUBLANES
    ing = lg[e0:e0 + EXPERTS_PER_GROUP, :]
    for g in range(1, N_EXPERT_GROUPS):
        ing = jnp.where(gsel == g, lg[e0 + g * EXPERTS_PER_GROUP:e0 + (g + 1) * EXPERTS_PER_GROUP, :], ing)
    ridx = _iota2(ing.shape, 0)
    m1 = jnp.max(ing, axis=0, keepdims=True)
    i1 = jnp.min(jnp.where(ing == m1, ridx, EXPERTS_PER_GROUP), axis=0, keepdims=True)
    rest = jnp.where(ridx == i1, -jnp.inf, ing)
    m2 = jnp.max(rest, axis=0, keepdims=True)
    i2 = jnp.min(jnp.where(rest == m2, ridx, EXPERTS_PER_GROUP), axis=0, keepdims=True)
    e2 = jnp.exp(m2 - m1)
    s = 1.0 + e2
    e_sel = (gsel * EXPERTS_PER_GROUP + i1, gsel * EXPERTS_PER_GROUP + i2)
    idx_out[0:1, :] = e_sel[0]
    idx_out[1:2, :] = e_sel[1]
    gate_out[0:1, :] = (1.0 / s) * g_gate
    gate_out[1:2, :] = (e2 / s) * g_gate

    @pl.when(pl.program_id(0) == 0)
    def _():
        cnt_ref[...] = jnp.zeros_like(cnt_ref)

    erow = _iota2((N_EXPERTS, tm), 0)
    onehot = [(erow == e).astype(F32) for e in e_sel]
    both = onehot[0] + onehot[1]
    before = cnt_ref[...] + _dot(both.astype(BF16), su_ref[...])
    for k in range(TOP_K):
        rank_out[k:k + 1, :] = jnp.sum(onehot[k] * before, axis=0, keepdims=True).astype(jnp.int32)
    cnt_ref[...] = cnt_ref[...] + jnp.sum(both, axis=1, keepdims=True)
    cnt_out[...] = jnp.broadcast_to(cnt_ref[...], cnt_out.shape).astype(jnp.int32)


def _outproj_router(x2, ys, w_out, g2, w_router_t, b_router):
    n, d = x2.shape
    tm = min(ROW_TILE, n)
    nr = w_router_t.shape[0]
    ymix = pl.BlockSpec((tm, W_MIX), lambda i: (i, 0))
    r = jnp.arange(tm)
    strict_upper = (r[:, None] < r[None, :]).astype(BF16)
    return pl.pallas_call(
        _outproj_router_kernel,
        grid=(n // tm,),
        in_specs=[
            pl.BlockSpec((tm, d), lambda i: (i, 0)),
            ymix, ymix, ymix, ymix,
            pl.BlockSpec((4 * W_MIX, d), lambda i: (0, 0)),
            pl.BlockSpec((1, d), lambda i: (0, 0)),
            pl.BlockSpec((nr, d), lambda i: (0, 0)),
            pl.BlockSpec((nr, 1), lambda i: (0, 0)),
            pl.BlockSpec((tm, tm), lambda i: (0, 0)),
        ],
        out_specs=[
            pl.BlockSpec((tm, d), lambda i: (i, 0)),
            pl.BlockSpec((tm * (d // LANES), LANES), lambda i: (i, 0)),
            pl.BlockSpec((TOP_K, tm), lambda i: (0, i)),
            pl.BlockSpec((TOP_K, tm), lambda i: (0, i)),
            pl.BlockSpec((TOP_K, tm), lambda i: (0, i)),
            pl.BlockSpec((N_EXPERTS, LANES), lambda i: (0, 0)),
        ],
        out_shape=[
            jax.ShapeDtypeStruct((n, d), F32),
            jax.ShapeDtypeStruct((n * (d // LANES), LANES), F32),
            jax.ShapeDtypeStruct((TOP_K, n), jnp.int32),
            jax.ShapeDtypeStruct((TOP_K, n), F32),
            jax.ShapeDtypeStruct((TOP_K, n), jnp.int32),
            jax.ShapeDtypeStruct((N_EXPERTS, LANES), jnp.int32),
        ],
        scratch_shapes=[pltpu.VMEM((N_EXPERTS, 1), F32)],
        compiler_params=_cparams("arbitrary"),
        name="outproj_router",
    )(x2, *ys, w_out, g2, w_router_t, b_router, strict_upper)


def _token_copy(src_hbm, dst, src_tok, dst_tok, sem, ch):
    return pltpu.make_async_copy(
        src_hbm.at[pl.ds(pl.multiple_of(src_tok * ch, ch), ch)],
        dst.at[pl.ds(pl.multiple_of(dst_tok * ch, ch), ch)], sem)


def _dispatch_kernel(pad_end_ref, slot_ref, h_ref, o_hbm, zbuf, zsem, sem, *, ch, tm):
    i = pl.program_id(0)

    def zero_copy(first_slot):
        return pltpu.make_async_copy(
            zbuf, o_hbm.at[pl.ds(pl.multiple_of(first_slot * ch, ch), MOE_TB * ch)], zsem)

    @pl.when(i == 0)
    def _():
        zbuf[...] = jnp.zeros_like(zbuf)
        tails = [jnp.maximum(pad_end_ref[e] - MOE_TB, 0) for e in range(N_EXPERTS)]
        for first in tails:
            zero_copy(first).start()
        for first in tails:
            zero_copy(first).wait()

        def unused_block(b, _):
            zero_copy(b * MOE_TB).start()
            zero_copy(b * MOE_TB).wait()
            return 0

        lax.fori_loop(pad_end_ref[N_EXPERTS - 1] // MOE_TB, o_hbm.shape[0] // (MOE_TB * ch),
                      unused_block, 0)

    def start(r, _):
        for k in range(TOP_K):
            _token_copy(h_ref, o_hbm, r, slot_ref[0, 0, r * TOP_K + k], sem, ch).start(
                priority=k % 2)
        return 0

    lax.fori_loop(0, tm, start, 0, unroll=DMA_ISSUE_UNROLL)
    for k in range(TOP_K):
        pltpu.make_async_copy(h_ref, o_hbm.at[pl.ds(0, tm * ch)], sem).wait()


def _dispatch(pad_end, slot_flat, h_rows, n_slots, ch):
    n_tok = h_rows.shape[0] // ch
    tm = min(256, n_tok)
    n_steps = n_tok // tm
    return pl.pallas_call(
        functools.partial(_dispatch_kernel, ch=ch, tm=tm),
        grid_spec=pltpu.PrefetchScalarGridSpec(
            num_scalar_prefetch=1,
            grid=(n_steps,),
            in_specs=[
                pl.BlockSpec((1, 1, tm * TOP_K), lambda i, pe: (i, 0, 0), memory_space=pltpu.SMEM),
                pl.BlockSpec((tm * ch, LANES), lambda i, pe: (i, 0)),
            ],
            out_specs=pl.BlockSpec(memory_space=pl.ANY),
            scratch_shapes=[pltpu.VMEM((MOE_TB * ch, LANES), F32), pltpu.SemaphoreType.DMA,
                            pltpu.SemaphoreType.DMA],
        ),
        out_shape=jax.ShapeDtypeStruct((n_slots * ch, LANES), h_rows.dtype),
        compiler_params=_cparams("arbitrary"),
        name="moe_dispatch",
    )(pad_end, slot_flat.reshape(n_steps, 1, tm * TOP_K), h_rows)


def _expert_kernel(be_ref, nb_ref, x_ref, wg_ref, wu_ref, wd_ref, o_ref, *, ch):
    i = pl.program_id(0)

    @pl.when(i < nb_ref[0])
    def _():
        x = _load_token_rows(x_ref, MOE_TB, ch).astype(BF16)
        a = _dot(x, wg_ref[0, 0].astype(BF16))
        u = _dot(x, wu_ref[0, 0].astype(BF16))
        hid = (a * jax.nn.sigmoid(a) * u).astype(BF16)
        _store_token_rows(o_ref, _dot(hid, wd_ref[0, 0].astype(BF16)))

    @pl.when(i >= nb_ref[0])
    def _():
        o_ref[...] = jnp.zeros_like(o_ref)


def _expert_blocks(block_expert, n_used, xb_rows, w_gate, w_up, w_down, layer):
    d, hid = w_gate.shape[2], w_gate.shape[3]
    ch = d // LANES
    n_slots = xb_rows.shape[0] // ch
    return pl.pallas_call(
        functools.partial(_expert_kernel, ch=ch),
        grid_spec=pltpu.PrefetchScalarGridSpec(
            num_scalar_prefetch=2,
            grid=(n_slots // MOE_TB,),
            in_specs=[
                pl.BlockSpec((MOE_TB * ch, LANES), lambda i, be, nb: (i, 0)),
                pl.BlockSpec((1, 1, d, hid), lambda i, be, nb: (layer, be[i], 0, 0)),
                pl.BlockSpec((1, 1, d, hid), lambda i, be, nb: (layer, be[i], 0, 0)),
                pl.BlockSpec((1, 1, hid, d), lambda i, be, nb: (layer, be[i], 0, 0)),
            ],
            out_specs=pl.BlockSpec((MOE_TB * ch, LANES), lambda i, be, nb: (i, 0)),
        ),
        out_shape=jax.ShapeDtypeStruct((n_slots * ch, LANES), F32),
        compiler_params=_cparams("arbitrary"),
        name="moe_experts",
    )(block_expert, n_used, xb_rows, w_gate, w_up, w_down)


def _combine_kernel(slot_ref, slot_next_ref, x_ref, gate_ref, yb_hbm, o_ref, buf, sem, *, ch):
    i = pl.program_id(0)
    last = pl.num_programs(0) - 1
    tm = x_ref.shape[0]

    def issue(idx_ref, parity):
        def start(r, _):
            for k in range(TOP_K):
                _token_copy(yb_hbm, buf.at[parity, k], idx_ref[0, 0, r * TOP_K + k], r,
                            sem.at[parity], ch).start(priority=k % 2)
            return 0

        lax.fori_loop(0, tm, start, 0, unroll=DMA_ISSUE_UNROLL)

    @pl.when(i == 0)
    def _():
        issue(slot_ref, 0)

    @pl.when(i < last)
    def _():
        issue(slot_next_ref, (i + 1) % 2)

    par = i % 2
    for k in range(TOP_K):
        pltpu.make_async_copy(yb_hbm.at[pl.ds(0, tm * ch)], buf.at[par, k], sem.at[par]).wait()
    y = (_load_token_rows(buf.at[par, 0], tm, ch) * gate_ref[:, 0:1]
         + _load_token_rows(buf.at[par, 1], tm, ch) * gate_ref[:, 1:2])
    o_ref[...] = x_ref[...] + y


def _combine(slot_flat, x2, gates, yb_rows):
    n, d = x2.shape
    ch = d // LANES
    tm = min(256, n)
    n_steps = n // tm
    slots3 = slot_flat.reshape(n_steps, 1, tm * TOP_K)
    return pl.pallas_call(
        functools.partial(_combine_kernel, ch=ch),
        grid=(n_steps,),
        in_specs=[
            pl.BlockSpec((1, 1, tm * TOP_K), lambda i: (i, 0, 0), memory_space=pltpu.SMEM),
            pl.BlockSpec((1, 1, tm * TOP_K), lambda i: (jnp.minimum(i + 1, n_steps - 1), 0, 0),
                         memory_space=pltpu.SMEM),
            pl.BlockSpec((tm, d), lambda i: (i, 0)),
            pl.BlockSpec((tm, TOP_K), lambda i: (i, 0)),
            pl.BlockSpec(memory_space=pl.ANY),
        ],
        out_specs=pl.BlockSpec((tm, d), lambda i: (i, 0)),
        scratch_shapes=[pltpu.VMEM((2, TOP_K, tm * ch, LANES), F32),
                        pltpu.SemaphoreType.DMA((2,))],
        out_shape=jax.ShapeDtypeStruct((n, d), F32),
        compiler_params=_cparams("arbitrary"),
        name="moe_combine",
    )(slots3, slots3, x2, gates, yb_rows)


def _routing_plan(expert_idx_t, rank_t, counts, n_tok):
    padded = (counts + MOE_TB - 1) // MOE_TB * MOE_TB
    pad_end = jnp.cumsum(padded)
    pad_start = pad_end - padded
    start_t = jnp.sum(jnp.where(expert_idx_t[:, :, None] == jnp.arange(N_EXPERTS), pad_start, 0),
                      axis=-1)
    slot = jnp.transpose(start_t + rank_t).reshape(n_tok * TOP_K)
    n_blocks = -(-(n_tok * TOP_K) // MOE_TB) + N_EXPERTS
    block_expert = jnp.minimum(
        jnp.sum(pad_end[None, :] <= (jnp.arange(n_blocks) * MOE_TB)[:, None], axis=1),
        N_EXPERTS - 1).astype(jnp.int32)
    n_used = (pad_end[-1] // MOE_TB).astype(jnp.int32).reshape(1)
    return slot.astype(jnp.int32), pad_end.astype(jnp.int32), block_expert, n_used, n_blocks * MOE_TB


def _moe(x2, h2, expert_idx_t, gates_t, rank_t, counts, w_gate, w_up, w_down, layer):
    n_tok = x2.shape[0]
    slot, pad_end, block_expert, n_used, n_slots = _routing_plan(expert_idx_t, rank_t, counts, n_tok)
    xb = _dispatch(pad_end, slot, h2, n_slots, x2.shape[1] // LANES)
    yb = _expert_blocks(block_expert, n_used, xb, w_gate, w_up, w_down, layer)
    return _combine(slot, x2, jnp.transpose(gates_t), yb)


def _final_norm_kernel(x_ref, g_ref, o_ref):
    x = x_ref[...]
    o_ref[...] = x * lax.rsqrt(jnp.mean(x * x, axis=-1, keepdims=True) + RMS_EPS) * g_ref[...]


def _final_norm(x2, g):
    n, d = x2.shape
    tm = min(ROW_TILE, n)
    return pl.pallas_call(
        _final_norm_kernel,
        grid=(n // tm,),
        in_specs=[pl.BlockSpec((tm, d), lambda i: (i, 0)), pl.BlockSpec((1, d), lambda i: (0, 0))],
        out_specs=pl.BlockSpec((tm, d), lambda i: (i, 0)),
        out_shape=jax.ShapeDtypeStruct((n, d), F32),
        compiler_params=_cparams("parallel"),
        name="final_norm",
    )(x2, g)


def _prepare_params(w_in, gm_w_s, gm_b, rw_mu, rw_w0, rw_w2, rw_a0, rw_a2, rw_g2, rw_k_k, rw_k_a,
                    rw_r_k, rw_gn_w, rw_gn_b, fx_b_f, w_out, router_group_w, router_group_b,
                    router_expert_w, router_expert_b):
    depth, d = w_in.shape[0], w_in.shape[1]
    n_main = 3 * W_MIX + 2 * W_MIX + RW_COLS + 3 * W_MIX
    t = lambda a: jnp.swapaxes(a, 1, 2)
    pad_rows = lambda a, rows: jnp.concatenate(
        [a, jnp.zeros((depth, rows - a.shape[1]) + a.shape[2:], a.dtype)], axis=1)
    row = lambda a: a.reshape(depth, 1, -1)
    wr = jnp.concatenate([pad_rows(t(router_group_w), SUBLANES), t(router_expert_w)], axis=1)
    br = jnp.concatenate([pad_rows(router_group_b[:, :, None], SUBLANES),
                          router_expert_b[:, :, None]], axis=1)
    return dict(
        w_main=w_in[:, :, :n_main].astype(BF16),
        w_gate_t=pad_rows(t(w_in[:, :, n_main:n_main + N_HEADS]), SUBLANES).astype(BF16),
        fx_bias=pad_rows(fx_b_f[:, :, None], SUBLANES),
        gm_w=jnp.transpose(gm_w_s, (0, 2, 1, 3)).reshape(depth, GM_CHUNK, N_HEADS * GM_CHUNK),
        gm_b=jnp.repeat(t(gm_b), HEAD_DIM, axis=2),
        mu=row(rw_mu), w0=row(rw_w0), w2=rw_w2.astype(BF16), a0=row(rw_a0),
        a2=rw_a2.astype(BF16), g2=rw_g2.astype(BF16), k_k=row(rw_k_k), k_a=row(rw_k_a),
        r_k=row(rw_r_k), gn_w=row(rw_gn_w), gn_b=row(rw_gn_b),
        w_out=w_out.astype(BF16), wr=wr, br=br,
    )


def _layer_params(l, w_in, gm_w_s, gm_b, rw_mu, rw_w0, rw_w2, rw_a0, rw_a2, rw_g2, rw_k_k, rw_k_a,
                  rw_r_k, rw_gn_w, rw_gn_b, fx_b_f, w_out, router_group_w, router_group_b,
                  router_expert_w, router_expert_b):
    allp = _prepare_params(w_in, gm_w_s, gm_b, rw_mu, rw_w0, rw_w2, rw_a0, rw_a2, rw_g2, rw_k_k,
                           rw_k_a, rw_r_k, rw_gn_w, rw_gn_b, fx_b_f, w_out, router_group_w,
                           router_group_b, router_expert_w, router_expert_b)
    return {k: v[l] for k, v in allp.items()}


def kernel(x, norm1_g, w_in, gm_w_s, gm_b, rw_mu, rw_w0, rw_w2, rw_a0, rw_a2, rw_g2, rw_k_k, rw_k_a, rw_r_k, rw_gn_w, rw_gn_b, fx_b_f, w_out, norm2_g, router_group_w, router_group_b, router_expert_w, router_expert_b, exp_w_gate, exp_w_up, exp_w_down, final_norm_g):
    bsz, seq, d = x.shape
    depth = w_in.shape[0]
    assert seq % min(ATT_TQB, seq) == 0 and seq % RW_CHUNK == 0 and seq % (2 * LANES) == 0
    x2 = x.reshape(bsz * seq, d)
    hd = jnp.arange(W_MIX) // HEAD_DIM
    ones_bd = (hd[:, None] == hd[None, :]).astype(BF16)
    allp = _prepare_params(w_in, gm_w_s, gm_b, rw_mu, rw_w0, rw_w2, rw_a0, rw_a2, rw_g2, rw_k_k,
                           rw_k_a, rw_r_k, rw_gn_w, rw_gn_b, fx_b_f, w_out, router_group_w,
                           router_group_b, router_expert_w, router_expert_b)
    for l in range(depth):
        lp = {k: v[l] for k, v in allp.items()}
        p_sb, p_gm, p_rw, p_fx, grow = _norm_inproj(x2, norm1_g[l].reshape(1, d), lp["w_main"],
                                                    lp["w_gate_t"])
        y_sb = _sb_attention(p_sb, bsz, seq)
        y_gm = _gm_mix(p_gm, lp["gm_w"], lp["gm_b"], ones_bd)
        prep = _rw_prep(p_rw, seq, lp["mu"], lp["w0"], lp["w2"], lp["a0"], lp["a2"], lp["g2"],
                        lp["k_k"], lp["k_a"], lp["r_k"], ones_bd)
        intra = _rw_intra(prep[:6], bsz, seq)
        y_rw = _rw_state(intra, prep[6], prep[7], lp["gn_w"], lp["gn_b"], ones_bd, bsz, seq)
        grow3 = jnp.transpose(grow.reshape(SUBLANES, bsz, seq), (1, 0, 2))
        cum_row = _fx_cum(grow3, lp["fx_bias"])
        y_fx = _fx_attention(p_fx, cum_row, bsz, seq)
        x2, h2, eidx_t, gates_t, rank_t, cnt = _outproj_router(
            x2, (y_sb, y_gm, y_rw, y_fx), lp["w_out"], norm2_g[l].reshape(1, d), lp["wr"], lp["br"])
        x2 = _moe(x2, h2, eidx_t, gates_t, rank_t, cnt[:, 0], exp_w_gate, exp_w_up, exp_w_down, l)
    return _final_norm(x2, final_norm_g.reshape(1, d)).reshape(bsz, seq, d)
```

```python
import functools

import jax
import jax.numpy as jnp
from jax import lax
from jax.experimental import pallas as pl
from jax.experimental.pallas import tpu as pltpu

F32 = jnp.float32
BF16 = jnp.bfloat16
HIGHEST = lax.Precision.HIGHEST

HEAD_DIM = 64
N_HEADS = 4
W_MIX = N_HEADS * HEAD_DIM
GM_CHUNK = 128
GM_ROWS = 1024
RW_DECAY_LORA = 64
RW_AAA_LORA = 64
RW_GATE_LORA = 128
RW_COLS = 3 * W_MIX + RW_DECAY_LORA + RW_AAA_LORA + RW_GATE_LORA
N_EXPERT_GROUPS = 4
EXPERTS_PER_GROUP = 8
N_EXPERTS = N_EXPERT_GROUPS * EXPERTS_PER_GROUP
TOP_K = 2
RMS_EPS = 1e-6
LN_EPS = 1e-5
GN_EPS = 64e-5
L2_EPS = 1e-12
LOG2_E = 1.4426950408889634

V7X_VMEM_LIMIT_BYTES = 56 * 1024 * 1024
LANES = 128
SUBLANES = 8

ATT_TK = 128
FX_TK = 256
SB_TILES_PER_STEP = 2
ATT_TQB = 512
ATT_W = N_HEADS * LANES
RW_CHUNK = 64
RW_CPS = 8
RW_BATCH_GROUP = 8
ROW_TILE = 512
MOE_TB = 256
DMA_ISSUE_UNROLL = 8


def _cparams(*sem):
    return pltpu.CompilerParams(dimension_semantics=sem, vmem_limit_bytes=V7X_VMEM_LIMIT_BYTES)


def _dot(a, b, precision=None):
    return jnp.dot(a, b, preferred_element_type=F32, precision=precision)


def _dot_nt(a, b, precision=None):
    return lax.dot_general(a, b, (((1,), (1,)), ((), ())), preferred_element_type=F32,
                           precision=precision)


def _dot_tn(a, b, precision=None):
    return lax.dot_general(a, b, (((0,), (0,)), ((), ())), preferred_element_type=F32,
                           precision=precision)


def _iota2(shape, dim):
    return lax.broadcasted_iota(jnp.int32, shape, dim)


def _store_heads_padded(ref, col0, src, fill):
    low = _iota2((src.shape[0], LANES), 1) < HEAD_DIM
    for pair in range(N_HEADS // 2):
        slab = src[:, pair * LANES:(pair + 1) * LANES]
        swapped = pltpu.roll(slab, HEAD_DIM, axis=1)
        for odd, val in enumerate((slab, swapped)):
            c = col0 + (2 * pair + odd) * LANES
            ref[:, c:c + LANES] = jnp.where(low, val, fill).astype(ref.dtype)


def _norm_inproj_kernel(x_ref, g_ref, w_ref, wgt_ref, sb_ref, gm_ref, rw_ref, fx_ref, grow_ref):
    x = x_ref[...]
    h = x * lax.rsqrt(jnp.mean(x * x, axis=-1, keepdims=True) + RMS_EPS) * g_ref[...]
    hb = h.astype(BF16)
    p = _dot(hb, w_ref[...])
    scale = HEAD_DIM ** -0.5
    wp = N_HEADS * LANES
    o = 0
    _store_heads_padded(sb_ref, 0, p[:, o:o + W_MIX] * scale, 0.0)
    _store_heads_padded(sb_ref, wp, p[:, o + W_MIX:o + 2 * W_MIX], 0.0)
    _store_heads_padded(sb_ref, 2 * wp, p[:, o + 2 * W_MIX:o + 3 * W_MIX], 0.0)
    o += 3 * W_MIX
    gm_ref[...] = p[:, o:o + 2 * W_MIX]
    o += 2 * W_MIX
    rw_ref[...] = p[:, o:o + RW_COLS]
    o += RW_COLS
    _store_heads_padded(fx_ref, 0, p[:, o:o + W_MIX] * scale, 0.0)
    _store_heads_padded(fx_ref, wp, p[:, o + W_MIX:o + 2 * W_MIX], 0.0)
    _store_heads_padded(fx_ref, 2 * wp, p[:, o + 2 * W_MIX:o + 3 * W_MIX], 1.0)
    grow_ref[...] = _dot_nt(wgt_ref[...], hb)


def _norm_inproj(x2, g, w_main, w_gate_t):
    n, d = x2.shape
    tm = min(ROW_TILE, n)
    cols = w_main.shape[1]
    return pl.pallas_call(
        _norm_inproj_kernel,
        grid=(n // tm,),
        in_specs=[
            pl.BlockSpec((tm, d), lambda i: (i, 0)),
            pl.BlockSpec((1, d), lambda i: (0, 0)),
            pl.BlockSpec((d, cols), lambda i: (0, 0)),
            pl.BlockSpec((SUBLANES, d), lambda i: (0, 0)),
        ],
        out_specs=[
            pl.BlockSpec((tm, 3 * ATT_W), lambda i: (i, 0)),
            pl.BlockSpec((tm, 2 * W_MIX), lambda i: (i, 0)),
            pl.BlockSpec((tm, RW_COLS), lambda i: (i, 0)),
            pl.BlockSpec((tm, 3 * ATT_W), lambda i: (i, 0)),
            pl.BlockSpec((SUBLANES, tm), lambda i: (0, i)),
        ],
        out_shape=[
            jax.ShapeDtypeStruct((n, 3 * ATT_W), BF16),
            jax.ShapeDtypeStruct((n, 2 * W_MIX), F32),
            jax.ShapeDtypeStruct((n, RW_COLS), F32),
            jax.ShapeDtypeStruct((n, 3 * ATT_W), BF16),
            jax.ShapeDtypeStruct((SUBLANES, n), F32),
        ],
        compiler_params=_cparams("parallel"),
        name="norm_inproj",
    )(x2, g, w_main, w_gate_t)


def _sb_kernel(q_ref, k_ref, v_ref, tri_ref, o_ref, c_ref, acc_ref):
    qi = pl.program_id(1)
    tqb = q_ref.shape[1]
    hl = lambda h: slice(h * LANES, (h + 1) * LANES)

    def key_tiles(tiles):
        meta, zs, rrs = [], [], []
        for j, off in tiles:
            r0 = 0 if off is None else off * ATT_TK
            rows = tqb - r0
            strict = (None if off is None
                      else _iota2((rows, ATT_TK), 1) < _iota2((rows, ATT_TK), 0))
            meta.append((pl.multiple_of(j * ATT_TK, ATT_TK), r0, rows, strict))
        for start, r0, rows, strict in meta:
            zs.append([_dot_nt(q_ref[0, r0:, hl(h)], k_ref[0, pl.ds(start, ATT_TK), hl(h)])
                       for h in range(N_HEADS)])
        for (start, r0, rows, strict), z4 in zip(meta, zs):
            parts = []
            for z in z4:
                sp = jnp.maximum(z, 0.0) + jnp.log(1.0 + jnp.exp2(jnp.abs(z) * -LOG2_E))
                if strict is not None:
                    sp = jnp.where(strict, sp, 0.0)
                hi = sp.astype(BF16)
                lo = (sp - hi.astype(F32)).astype(BF16)
                parts.append(jnp.concatenate([hi, lo], axis=1))
            rrs.append(_dot(jnp.concatenate(parts, axis=0), tri_ref[...]))
        for (start, r0, rows, strict), z4, rr in zip(meta, zs, rrs):
            ps = []
            for h in range(N_HEADS):
                rh = rr[h * rows:(h + 1) * rows]
                p = jnp.exp(z4[h] - rh[:, :ATT_TK] - c_ref[h, r0:, :])
                if strict is not None:
                    p = jnp.where(strict, p, 0.0)
                ps.append(p.astype(BF16))
                c_ref[h, r0:, :] += rh[:, ATT_TK:]
            pv = [_dot(ps[h], v_ref[0, pl.ds(start, ATT_TK), hl(h)]) for h in range(N_HEADS)]
            for h in range(N_HEADS):
                acc_ref[h, r0:, :] += pv[h]

    c_ref[...] = jnp.zeros_like(c_ref)
    acc_ref[...] = jnp.zeros_like(acc_ref)
    n_diag = tqb // ATT_TK
    group = SB_TILES_PER_STEP if n_diag % SB_TILES_PER_STEP == 0 else 1
    for first in reversed(range(0, n_diag, group)):
        key_tiles([(qi * n_diag + off, off) for off in reversed(range(first, first + group))])

    def body(jj, carry):
        right = qi * n_diag - 1 - jj * group
        key_tiles([(right - g, None) for g in range(group)])
        return carry

    lax.fori_loop(0, qi * n_diag // group, body, 0)
    for h in range(N_HEADS):
        o_ref[0, :, h * HEAD_DIM:(h + 1) * HEAD_DIM] = acc_ref[h, :, :HEAD_DIM].astype(o_ref.dtype)


def _sb_attention(qkv, bsz, seq):
    qkv3 = qkv.reshape(bsz, seq, 3 * ATT_W)
    r = jnp.arange(ATT_TK)
    tri = (r[:, None] >= r[None, :]).astype(BF16)
    blk = jnp.concatenate([tri, jnp.ones((ATT_TK, ATT_TK), BF16)], axis=1)
    tri2 = jnp.concatenate([blk, blk], axis=0)
    tqb = min(ATT_TQB, seq)
    out = pl.pallas_call(
        _sb_kernel,
        grid=(bsz, seq // tqb),
        in_specs=[
            pl.BlockSpec((1, tqb, ATT_W), lambda b, i: (b, i, 0)),
            pl.BlockSpec((1, seq, ATT_W), lambda b, i: (b, 0, 1)),
            pl.BlockSpec((1, seq, ATT_W), lambda b, i: (b, 0, 2)),
            pl.BlockSpec((2 * ATT_TK, 2 * ATT_TK), lambda b, i: (0, 0)),
        ],
        out_specs=pl.BlockSpec((1, tqb, W_MIX), lambda b, i: (b, i, 0)),
        out_shape=jax.ShapeDtypeStruct((bsz, seq, W_MIX), BF16),
        scratch_shapes=[pltpu.VMEM((N_HEADS, tqb, ATT_TK), F32),
                        pltpu.VMEM((N_HEADS, tqb, LANES), F32)],
        compiler_params=_cparams("parallel", "parallel"),
        name="sb_attention",
    )(qkv3, qkv3, qkv3, tri2)
    return out.reshape(bsz * seq, W_MIX)


def _fx_cum_kernel(g_ref, b_ref, cum_ref):
    seq = g_ref.shape[2]
    blk = 2 * LANES
    r = _iota2((blk, blk), 0)
    c = _iota2((blk, blk), 1)
    tri = (r <= c).astype(F32)
    carry = jnp.zeros((SUBLANES, 1), F32)
    for s in range(seq // blk):
        x = g_ref[0, :, s * blk:(s + 1) * blk] + b_ref[...]
        lf = -(jnp.maximum(-x, 0.0) + jnp.log(1.0 + jnp.exp(-jnp.abs(x))))
        cs = _dot(lf, tri, precision=HIGHEST) + carry
        cum_ref[0, :, s * blk:(s + 1) * blk] = cs
        carry = cs[:, blk - 1:blk]


def _fx_cum(grow3, bias_col):
    bsz, _, seq = grow3.shape
    return pl.pallas_call(
        _fx_cum_kernel,
        grid=(bsz,),
        in_specs=[
            pl.BlockSpec((1, SUBLANES, seq), lambda b: (b, 0, 0)),
            pl.BlockSpec((SUBLANES, 1), lambda b: (0, 0)),
        ],
        out_specs=pl.BlockSpec((1, SUBLANES, seq), lambda b: (b, 0, 0)),
        out_shape=jax.ShapeDtypeStruct((bsz, SUBLANES, seq), F32),
        compiler_params=_cparams("parallel"),
        name="fx_cum",
    )(grow3, bias_col)


def _fx_kernel(q_ref, k_ref, v_ref, crow_ref, ccol_ref, o_ref, m_ref, acc_ref, ct_ref):
    qi = pl.program_id(1)
    tqb = q_ref.shape[1]
    hl = lambda h: slice(h * LANES, (h + 1) * LANES)

    def key_tile(j, off):
        r0 = 0 if off is None else off * FX_TK
        rows = tqb - r0
        start = pl.multiple_of(j * FX_TK, FX_TK)
        halves = range(FX_TK // LANES)
        if off is not None:
            causal = [_iota2((rows, LANES), 1) + s * LANES <= _iota2((rows, LANES), 0)
                      for s in halves]
        heads = range(N_HEADS)
        z = [_dot_nt(q_ref[0, r0:, hl(h)], k_ref[0, pl.ds(start, FX_TK), hl(h)]) for h in heads]
        ps, alphas = [], []
        for h in heads:
            ct = ct_ref[h, r0:, :]
            xs = []
            for s in halves:
                cs = crow_ref[0, h:h + 1, pl.ds(pl.multiple_of(start + s * LANES, LANES), LANES)]
                x = z[h][:, s * LANES:(s + 1) * LANES] + ct - cs
                if off is not None:
                    x = jnp.where(causal[s], x, -jnp.inf)
                xs.append(x)
            m_old = m_ref[h, r0:, :]
            m_new = jnp.maximum(m_old, jnp.max(functools.reduce(jnp.maximum, xs), axis=-1,
                                               keepdims=True))
            m_ref[h, r0:, :] = m_new
            ps.append(jnp.concatenate([jnp.exp(x - m_new).astype(BF16) for x in xs], axis=1))
            alphas.append(jnp.exp(m_old - m_new))
        pv = [_dot(ps[h], v_ref[0, pl.ds(start, FX_TK), hl(h)]) for h in heads]
        for h in heads:
            acc_ref[h, r0:, :] = alphas[h] * acc_ref[h, r0:, :] + pv[h]

    m_ref[...] = jnp.full_like(m_ref, -jnp.inf)
    acc_ref[...] = jnp.zeros_like(acc_ref)
    for h in range(N_HEADS):
        ct_ref[h] = jnp.broadcast_to(ccol_ref[0, :, h:h + 1], (tqb, LANES))
    n_diag = tqb // FX_TK

    def body(j, carry):
        key_tile(j, None)
        return carry

    lax.fori_loop(0, qi * n_diag, body, 0)
    for off in range(n_diag):
        key_tile(qi * n_diag + off, off)
    for h in range(N_HEADS):
        acc = acc_ref[h]
        den = pltpu.roll(acc, HEAD_DIM, axis=1)
        o_ref[0, :, h * HEAD_DIM:(h + 1) * HEAD_DIM] = (acc / den)[:, :HEAD_DIM].astype(o_ref.dtype)


def _fx_attention(qkv, cum_row, bsz, seq):
    qkv3 = qkv.reshape(bsz, seq, 3 * ATT_W)
    cum_col = jnp.transpose(cum_row, (0, 2, 1))
    tqb = min(ATT_TQB, seq)
    out = pl.pallas_call(
        _fx_kernel,
        grid=(bsz, seq // tqb),
        in_specs=[
            pl.BlockSpec((1, tqb, ATT_W), lambda b, i: (b, i, 0)),
            pl.BlockSpec((1, seq, ATT_W), lambda b, i: (b, 0, 1)),
            pl.BlockSpec((1, seq, ATT_W), lambda b, i: (b, 0, 2)),
            pl.BlockSpec((1, SUBLANES, seq), lambda b, i: (b, 0, 0)),
            pl.BlockSpec((1, tqb, SUBLANES), lambda b, i: (b, i, 0)),
        ],
        out_specs=pl.BlockSpec((1, tqb, W_MIX), lambda b, i: (b, i, 0)),
        out_shape=jax.ShapeDtypeStruct((bsz, seq, W_MIX), BF16),
        scratch_shapes=[pltpu.VMEM((N_HEADS, tqb, LANES), F32)] * 3,
        compiler_params=_cparams("parallel", "parallel"),
        name="fx_attention",
    )(qkv3, qkv3, qkv3, cum_row, cum_col)
    return out.reshape(bsz * seq, W_MIX)


def _gm_kernel(p_ref, w_ref, b_ref, ones_ref, o_ref):
    hid = jax.nn.gelu(p_ref[...])
    u = hid[:, :W_MIX]
    v = hid[:, W_MIX:]
    inv_n = 1.0 / HEAD_DIM
    mean = _mm_lhs2(v, ones_ref[...], _NN) * inv_n
    vc = v - mean
    var = _mm_lhs2(vc * vc, ones_ref[...], _NN) * inv_n
    vn = (vc * lax.rsqrt(var + LN_EPS)).astype(BF16)
    wshape = (GM_CHUNK, N_HEADS * GM_CHUNK)
    lower = jnp.bitwise_and(_iota2(wshape, 1), GM_CHUNK - 1) <= _iota2(wshape, 0)
    w = jnp.where(lower, w_ref[...], 0.0).astype(BF16)
    chunks = range(p_ref.shape[0] // GM_CHUNK)
    rows = [slice(c * GM_CHUNK, (c + 1) * GM_CHUNK) for c in chunks]
    mixed = [_dot(w, _head_blocks(vn[rows[c]], HEAD_DIM)) for c in chunks]
    for c in chunks:
        o_ref[rows[c], :] = (u[rows[c]] * (mixed[c] + b_ref[...])).astype(o_ref.dtype)


def _gm_mix(p_gm, w_cat, b_full, ones_bd):
    n = p_gm.shape[0]
    tm = min(GM_ROWS, n)
    return pl.pallas_call(
        _gm_kernel,
        grid=(n // tm,),
        in_specs=[
            pl.BlockSpec((tm, 2 * W_MIX), lambda i: (i, 0)),
            pl.BlockSpec((GM_CHUNK, N_HEADS * GM_CHUNK), lambda i: (0, 0)),
            pl.BlockSpec((GM_CHUNK, W_MIX), lambda i: (0, 0)),
            pl.BlockSpec((W_MIX, W_MIX), lambda i: (0, 0)),
        ],
        out_specs=pl.BlockSpec((tm, W_MIX), lambda i: (i, 0)),
        out_shape=jax.ShapeDtypeStruct((n, W_MIX), BF16),
        compiler_params=_cparams("parallel"),
        name="gm_mix",
    )(p_gm, w_cat, b_full, ones_bd)


def _head_sum(x, ones_bd):
    return _mm_lhs2(x, ones_bd, _NN)


def _rw_prep_kernel(p_ref, prev_ref, mu_ref, w0_ref, w2_ref, a0_ref, a2_ref, g2_ref,
                    kk_ref, ka_ref, rk_ref, bd_ref,
                    r_out, ld_out, k_out, v_out, kk_out, ba_out, bonus_out, g_out, *, seq):
    i = pl.program_id(0)
    tm = p_ref.shape[0]
    p = p_ref[...]
    first = (i * tm) % seq == 0
    prev_row = jnp.where(first, 0.0, prev_ref[SUBLANES - 1:SUBLANES, :])
    rows = _iota2(p.shape, 0)
    prev = jnp.where(rows == 0, prev_row, pltpu.roll(p, 1, axis=0))
    p = p + (prev - p) * mu_ref[...]
    r = p[:, 0:W_MIX]
    k = p[:, W_MIX:2 * W_MIX]
    v = p[:, 2 * W_MIX:3 * W_MIX]
    o = 3 * W_MIX
    xw = p[:, o:o + RW_DECAY_LORA]
    o += RW_DECAY_LORA
    xa = p[:, o:o + RW_AAA_LORA]
    o += RW_AAA_LORA
    xg = p[:, o:o + RW_GATE_LORA]
    wpre = -(w0_ref[...] + _dot(jnp.tanh(xw).astype(BF16), w2_ref[...]))
    w = -(jnp.maximum(wpre, 0.0) + jnp.log(1.0 + jnp.exp(-jnp.abs(wpre)))) - 0.5
    a = jax.nn.sigmoid(a0_ref[...] + _dot(xa.astype(BF16), a2_ref[...]))
    g = _dot(jax.nn.sigmoid(xg).astype(BF16), g2_ref[...])
    kk = k * kk_ref[...]
    nrm = jnp.maximum(jnp.sqrt(_head_sum(kk * kk, bd_ref[...])), L2_EPS)
    kk = kk / nrm
    k2 = k * (1.0 + (a - 1.0) * ka_ref[...])
    r_out[...] = r
    ld_out[...] = -jnp.exp(w)
    k_out[...] = k2
    v_out[...] = v
    kk_out[...] = kk
    ba_out[...] = kk * a
    bonus_out[...] = _head_sum(r * k2 * rk_ref[...], bd_ref[...]) * v
    g_out[...] = g


def _rw_prep(p_rw, seq, mu, w0, w2, a0, a2, g2, k_k, k_a, r_k, ones_bd):
    n = p_rw.shape[0]
    tm = min(256, seq)
    vec = lambda width: pl.BlockSpec((1, width), lambda i: (0, 0))
    mat = lambda rws: pl.BlockSpec((rws, W_MIX), lambda i: (0, 0))
    out_spec = pl.BlockSpec((tm, W_MIX), lambda i: (i, 0))
    return pl.pallas_call(
        functools.partial(_rw_prep_kernel, seq=seq),
        grid=(n // tm,),
        in_specs=[
            pl.BlockSpec((tm, RW_COLS), lambda i: (i, 0)),
            pl.BlockSpec((SUBLANES, RW_COLS),
                         lambda i: (jnp.maximum(i * (tm // SUBLANES) - 1, 0), 0)),
            vec(RW_COLS), vec(W_MIX), mat(RW_DECAY_LORA), vec(W_MIX), mat(RW_AAA_LORA),
            mat(RW_GATE_LORA), vec(W_MIX), vec(W_MIX), vec(W_MIX), mat(W_MIX),
        ],
        out_specs=[out_spec] * 8,
        out_shape=[jax.ShapeDtypeStruct((n, W_MIX), F32)] * 8,
        compiler_params=_cparams("parallel"),
        name="rw_prep",
    )(p_rw, p_rw, mu, w0, w2, a0, a2, g2, k_k, k_a, r_k, ones_bd)


_NN = (((1,), (0,)), ((), ()))
_NT = (((1,), (1,)), ((), ()))


def _split2(x):
    hi = x.astype(BF16)
    return hi, (x - hi.astype(F32)).astype(BF16)


def _mm_lhs2(a, b, dims):
    m = a.shape[0]
    hi, lo = _split2(a)
    r = lax.dot_general(jnp.concatenate([hi, lo], axis=0), b, dims, preferred_element_type=F32)
    return r[:m] + r[m:]


def _lane_head(shape, width):
    return lax.shift_right_logical(_iota2(shape, 1), width.bit_length() - 1)


def _head_blocks(x, width):
    lh = _lane_head(x.shape, width)
    return jnp.concatenate([jnp.where(lh == h, x, jnp.zeros_like(x)) for h in range(N_HEADS)],
                           axis=0)


def _head_diag(full, width):
    rows = full.shape[0] // N_HEADS
    lh = _lane_head((rows, full.shape[1]), width)
    out = jnp.zeros((rows, full.shape[1]), F32)
    for h in range(N_HEADS):
        out = out + jnp.where(lh == h, full[h * rows:(h + 1) * rows], 0.0)
    return out


def _tn_lhs2(a, b_bf16):
    hi = a.astype(BF16).astype(F32)
    b = b_bf16.astype(F32)
    return _dot_tn(hi, b) + _dot_tn(a - hi, b)


def _rw_intra_kernel(r_ref, ld_ref, k_ref, v_ref, kk_ref, ba_ref,
                     w_out, u0_out, y0_out, rt_out, btp_out, mrb_out, g0_out, pc_out):
    cs = RW_CHUNK
    cw = N_HEADS * cs
    row = _iota2((cs, cw), 0)
    colc = jnp.bitwise_and(_iota2((cs, cw), 1), cs - 1)
    lower = colc <= row
    strict = colc < row
    eye = (colc == row).astype(F32)
    tri3 = (jnp.bitwise_and(_iota2((cs, 3 * cs), 1), cs - 1) <= _iota2((cs, 3 * cs), 0)).astype(BF16)

    chunks = range(r_ref.shape[1] // cs)
    rows = [slice(c * cs, (c + 1) * cs) for c in chunks]
    each = lambda f: [f(c) for c in chunks]
    hb = lambda x: _head_blocks(x.astype(BF16), HEAD_DIM)

    def running_log_decay(c):
        ld = ld_ref[0, rows[c], :]
        l1 = ld.astype(BF16)
        l2, l3 = _split2(ld - l1.astype(F32))
        return _dot(tri3, jnp.concatenate([l1, l2, l3], axis=0))

    cl = each(running_log_decay)
    p_in = each(lambda c: jnp.exp(cl[c]))
    p_inv = each(lambda c: jnp.exp(-cl[c]))
    at = each(lambda c: -kk_ref[0, rows[c], :] * jnp.exp(cl[c] - ld_ref[0, rows[c], :]))
    bt = each(lambda c: ba_ref[0, rows[c], :] * p_inv[c])
    kt = each(lambda c: k_ref[0, rows[c], :] * p_inv[c])
    rt = each(lambda c: r_ref[0, rows[c], :] * p_in[c])
    pc = each(lambda c: p_in[c][cs - 1:cs, :])

    lhs = each(lambda c: jnp.concatenate([at[c], rt[c]], axis=0))
    ab_mb = each(lambda c: _mm_lhs2(lhs[c], hb(bt[c]), _NT))
    ak_mk = each(lambda c: _mm_lhs2(lhs[c], hb(kt[c]), _NT))
    a_ab = each(lambda c: jnp.where(strict, ab_mb[c][:cs], 0.0))
    m_rb = each(lambda c: jnp.where(lower, ab_mb[c][cs:], 0.0))
    a_ak = each(lambda c: jnp.where(strict, ak_mk[c][:cs], 0.0))
    m_rk = each(lambda c: jnp.where(lower, ak_mk[c][cs:], 0.0))

    inv = each(lambda c: eye + a_ab[c])
    pw = a_ab
    pw_blocks = each(lambda c: _head_blocks(pw[c].astype(BF16), cs))
    steps = 1
    while steps * 2 < cs:
        pw = each(lambda c: _mm_lhs2(pw[c], pw_blocks[c], _NN))
        pw_blocks = each(lambda c: _head_blocks(pw[c].astype(BF16), cs))
        inv = each(lambda c: inv[c] + _mm_lhs2(inv[c], pw_blocks[c], _NN))
        steps *= 2

    akv_mkv = each(lambda c: _mm_lhs2(jnp.concatenate([a_ak[c], m_rk[c]], axis=0),
                                      hb(v_ref[0, rows[c], :]), _NN))
    wu = each(lambda c: _mm_lhs2(
        inv[c], jnp.concatenate([hb(at[c]), hb(akv_mkv[c][:cs])], axis=1), _NN))
    g0 = each(lambda c: _head_diag(_tn_lhs2(v_ref[0, rows[c], :], kt[c].astype(BF16)), HEAD_DIM))
    for c in chunks:
        w_out[0, rows[c], :] = wu[c][:, :W_MIX]
        u0_out[0, rows[c], :] = wu[c][:, W_MIX:]
        y0_out[0, rows[c], :] = akv_mkv[c][cs:]
        rt_out[0, rows[c], :] = rt[c]
        btp_out[0, rows[c], :] = bt[c] * pc[c]
        mrb_out[0, rows[c], :] = m_rb[c]
        g0_out[0, rows[c], :] = g0[c] * pc[c]
        pc_out[0, c * SUBLANES:(c + 1) * SUBLANES, :] = jnp.broadcast_to(pc[c], (SUBLANES, W_MIX))


def _rw_intra(prep6, bsz, seq):
    arrs = [a.reshape(bsz, seq, W_MIX) for a in prep6]
    rows = RW_CHUNK * RW_CPS
    spec = pl.BlockSpec((1, rows, W_MIX), lambda b, c: (b, c, 0))
    pc_spec = pl.BlockSpec((1, SUBLANES * RW_CPS, W_MIX), lambda b, c: (b, c, 0))
    big = jax.ShapeDtypeStruct((bsz, seq, W_MIX), F32)
    return pl.pallas_call(
        _rw_intra_kernel,
        grid=(bsz, seq // rows),
        in_specs=[spec] * 6,
        out_specs=[spec] * 7 + [pc_spec],
        out_shape=[big] * 7 + [jax.ShapeDtypeStruct((bsz, seq // RW_CHUNK * SUBLANES, W_MIX), F32)],
        compiler_params=_cparams("parallel", "parallel"),
        name="rw_intra",
    )(*arrs)


def _rw_state_kernel(w_ref, u0_ref, y0_ref, rt_ref, btp_ref, mrb_ref, g0_ref, pc_ref,
                     bonus_ref, g_ref, gnw_ref, gnb_ref, ones_ref, o_ref, state_ref):
    cs = RW_CHUNK

    @pl.when(pl.program_id(1) == 0)
    def _():
        state_ref[...] = jnp.zeros_like(state_ref)

    batch = range(w_ref.shape[0])
    each = lambda f: [f(b) for b in batch]
    hb = lambda x: _head_blocks(x.astype(BF16), HEAD_DIM)
    s0 = each(lambda b: state_ref[b])
    wr = each(lambda b: _mm_lhs2(jnp.concatenate([w_ref[b], rt_ref[b]], axis=0), hb(s0[b]), _NT))
    u = each(lambda b: wr[b][:cs] + u0_ref[b])
    y = each(lambda b: wr[b][cs:] + _mm_lhs2(mrb_ref[b], hb(u[b]), _NN) + y0_ref[b])
    su = each(lambda b: _head_diag(_tn_lhs2(u[b], btp_ref[b].astype(BF16)), HEAD_DIM))
    for b in batch:
        state_ref[b] = s0[b] * pc_ref[b, 0:1, :] + g0_ref[b] + su[b]

    inv_n = 1.0 / HEAD_DIM
    mean = each(lambda b: _mm_lhs2(y[b], ones_ref[...], _NN) * inv_n)
    yc = each(lambda b: y[b] - mean[b])
    var = each(lambda b: _mm_lhs2(yc[b] * yc[b], ones_ref[...], _NN) * inv_n)
    for b in batch:
        yn = yc[b] * lax.rsqrt(var[b] + GN_EPS) * gnw_ref[...] + gnb_ref[...]
        o_ref[b] = ((yn + bonus_ref[b]) * g_ref[b]).astype(o_ref.dtype)


def _rw_state(intra, bonus, g, gn_w, gn_b, ones_bd, bsz, seq):
    gb = RW_BATCH_GROUP if bsz % RW_BATCH_GROUP == 0 else 1
    spec = pl.BlockSpec((gb, RW_CHUNK, W_MIX), lambda b, c: (b, c, 0))
    pc_spec = pl.BlockSpec((gb, SUBLANES, W_MIX), lambda b, c: (b, c, 0))
    vec = pl.BlockSpec((1, W_MIX), lambda b, c: (0, 0))
    out = pl.pallas_call(
        _rw_state_kernel,
        grid=(bsz // gb, seq // RW_CHUNK),
        in_specs=[spec] * 7 + [pc_spec, spec, spec, vec, vec,
                               pl.BlockSpec((W_MIX, W_MIX), lambda b, c: (0, 0))],
        out_specs=spec,
        out_shape=jax.ShapeDtypeStruct((bsz, seq, W_MIX), BF16),
        scratch_shapes=[pltpu.VMEM((gb, HEAD_DIM, W_MIX), F32)],
        compiler_params=_cparams("parallel", "arbitrary"),
        name="rw_state",
    )(*intra, bonus.reshape(bsz, seq, W_MIX), g.reshape(bsz, seq, W_MIX), gn_w, gn_b,
      ones_bd.astype(BF16))
    return out.reshape(bsz * seq, W_MIX)


def _store_token_rows(ref, val):
    tm, d = val.shape
    ch = d // LANES
    for s in range(ch):
        ref[pl.ds(s, tm, stride=ch), :] = val[:, s * LANES:(s + 1) * LANES]


def _load_token_rows(ref, tm, ch):
    return jnp.concatenate([ref[pl.ds(s, tm, stride=ch), :] for s in range(ch)], axis=1)


def _outproj_router_kernel(x_ref, sb_ref, gm_ref, rw_ref, fx_ref, wo_ref, g_ref, wr_ref, br_ref,
                           su_ref, x_out, h_out, idx_out, gate_out, rank_out, cnt_out, cnt_ref):
    mix = jnp.concatenate([sb_ref[...], gm_ref[...], rw_ref[...], fx_ref[...]], axis=1)
    x = x_ref[...] + _dot(mix, wo_ref[...])
    x_out[...] = x
    h = x * lax.rsqrt(jnp.mean(x * x, axis=-1, keepdims=True) + RMS_EPS) * g_ref[...]
    _store_token_rows(h_out, h)

    nr = wr_ref.shape[0]
    h_hi, h_lo = _split2(h)
    w_hi, w_lo = _split2(wr_ref[...])
    part = _dot_nt(jnp.concatenate([w_hi, w_lo], axis=0), h_hi)
    lg = part[:nr] + part[nr:] + _dot_nt(w_hi, h_lo) + br_ref[...]
    tm = lg.shape[1]
    gl = [lg[g:g + 1, :] for g in range(N_EXPERT_GROUPS)]
    gmax = gl[0]
    gsel = jnp.zeros((1, tm), jnp.int32)
    for g in range(1, N_EXPERT_GROUPS):
        better = gl[g] > gmax
        gsel = jnp.where(better, g, gsel)
        gmax = jnp.where(better, gl[g], gmax)
    denom = gl[0] * 0.0
    for g in range(N_EXPERT_GROUPS):
        denom = denom + jnp.exp(gl[g] - gmax)
    g_gate = 1.0 / denom

    e0 = SUBLANES
    ing = lg[e0:e0 + EXPERTS_PER_GROUP, :]
    for g in range(1, N_EXPERT_GROUPS):
        ing = jnp.where(gsel == g, lg[e0 + g * EXPERTS_PER_GROUP:e0 + (g + 1) * EXPERTS_PER_GROUP, :], ing)
    ridx = _iota2(ing.shape, 0)
    m1 = jnp.max(ing, axis=0, keepdims=True)
    i1 = jnp.min(jnp.where(ing == m1, ridx, EXPERTS_PER_GROUP), axis=0, keepdims=True)
    rest = jnp.where(ridx == i1, -jnp.inf, ing)
    m2 = jnp.max(rest, axis=0, keepdims=True)
    i2 = jnp.min(jnp.where(rest == m2, ridx, EXPERTS_PER_GROUP), axis=0, keepdims=True)
    e2 = jnp.exp(m2 - m1)
    s = 1.0 + e2
    e_sel = (gsel * EXPERTS_PER_GROUP + i1, gsel * EXPERTS_PER_GROUP + i2)
    idx_out[0:1, :] = e_sel[0]
    idx_out[1:2, :] = e_sel[1]
    gate_out[0:1, :] = (1.0 / s) * g_gate
    gate_out[1:2, :] = (e2 / s) * g_gate

    @pl.when(pl.program_id(0) == 0)
    def _():
        cnt_ref[...] = jnp.zeros_like(cnt_ref)

    erow = _iota2((N_EXPERTS, tm), 0)
    onehot = [(erow == e).astype(F32) for e in e_sel]
    both = onehot[0] + onehot[1]
    before = cnt_ref[...] + _dot(both.astype(BF16), su_ref[...])
    for k in range(TOP_K):
        rank_out[k:k + 1, :] = jnp.sum(onehot[k] * before, axis=0, keepdims=True).astype(jnp.int32)
    cnt_ref[...] = cnt_ref[...] + jnp.sum(both, axis=1, keepdims=True)
    cnt_out[...] = jnp.broadcast_to(cnt_ref[...], cnt_out.shape).astype(jnp.int32)


def _outproj_router(x2, ys, w_out, g2, w_router_t, b_router):
    n, d = x2.shape
    tm = min(ROW_TILE, n)
    nr = w_router_t.shape[0]
    ymix = pl.BlockSpec((tm, W_MIX), lambda i: (i, 0))
    r = jnp.arange(tm)
    strict_upper = (r[:, None] < r[None, :]).astype(BF16)
    return pl.pallas_call(
        _outproj_router_kernel,
        grid=(n // tm,),
        in_specs=[
            pl.BlockSpec((tm, d), lambda i: (i, 0)),
            ymix, ymix, ymix, ymix,
            pl.BlockSpec((4 * W_MIX, d), lambda i: (0, 0)),
            pl.BlockSpec((1, d), lambda i: (0, 0)),
            pl.BlockSpec((nr, d), lambda i: (0, 0)),
            pl.BlockSpec((nr, 1), lambda i: (0, 0)),
            pl.BlockSpec((tm, tm), lambda i: (0, 0)),
        ],
        out_specs=[
            pl.BlockSpec((tm, d), lambda i: (i, 0)),
            pl.BlockSpec((tm * (d // LANES), LANES), lambda i: (i, 0)),
            pl.BlockSpec((TOP_K, tm), lambda i: (0, i)),
            pl.BlockSpec((TOP_K, tm), lambda i: (0, i)),
            pl.BlockSpec((TOP_K, tm), lambda i: (0, i)),
            pl.BlockSpec((N_EXPERTS, LANES), lambda i: (0, 0)),
        ],
        out_shape=[
            jax.ShapeDtypeStruct((n, d), F32),
            jax.ShapeDtypeStruct((n * (d // LANES), LANES), F32),
            jax.ShapeDtypeStruct((TOP_K, n), jnp.int32),
            jax.ShapeDtypeStruct((TOP_K, n), F32),
            jax.ShapeDtypeStruct((TOP_K, n), jnp.int32),
            jax.ShapeDtypeStruct((N_EXPERTS, LANES), jnp.int32),
        ],
        scratch_shapes=[pltpu.VMEM((N_EXPERTS, 1), F32)],
        compiler_params=_cparams("arbitrary"),
        name="outproj_router",
    )(x2, *ys, w_out, g2, w_router_t, b_router, strict_upper)


def _token_copy(src_hbm, dst, src_tok, dst_tok, sem, ch):
    return pltpu.make_async_copy(
        src_hbm.at[pl.ds(pl.multiple_of(src_tok * ch, ch), ch)],
        dst.at[pl.ds(pl.multiple_of(dst_tok * ch, ch), ch)], sem)


def _dispatch_kernel(pad_end_ref, slot_ref, h_ref, o_hbm, zbuf, zsem, sem, *, ch, tm):
    i = pl.program_id(0)

    def zero_copy(first_slot):
        return pltpu.make_async_copy(
            zbuf, o_hbm.at[pl.ds(pl.multiple_of(first_slot * ch, ch), MOE_TB * ch)], zsem)

    @pl.when(i == 0)
    def _():
        zbuf[...] = jnp.zeros_like(zbuf)
        tails = [jnp.maximum(pad_end_ref[e] - MOE_TB, 0) for e in range(N_EXPERTS)]
        for first in tails:
            zero_copy(first).start()
        for first in tails:
            zero_copy(first).wait()

        def unused_block(b, _):
            zero_copy(b * MOE_TB).start()
            zero_copy(b * MOE_TB).wait()
            return 0

        lax.fori_loop(pad_end_ref[N_EXPERTS - 1] // MOE_TB, o_hbm.shape[0] // (MOE_TB * ch),
                      unused_block, 0)

    def start(r, _):
        for k in range(TOP_K):
            _token_copy(h_ref, o_hbm, r, slot_ref[0, 0, r * TOP_K + k], sem, ch).start(
                priority=k % 2)
        return 0

    lax.fori_loop(0, tm, start, 0, unroll=DMA_ISSUE_UNROLL)
    for k in range(TOP_K):
        pltpu.make_async_copy(h_ref, o_hbm.at[pl.ds(0, tm * ch)], sem).wait()


def _dispatch(pad_end, slot_flat, h_rows, n_slots, ch):
    n_tok = h_rows.shape[0] // ch
    tm = min(256, n_tok)
    n_steps = n_tok // tm
    return pl.pallas_call(
        functools.partial(_dispatch_kernel, ch=ch, tm=tm),
        grid_spec=pltpu.PrefetchScalarGridSpec(
            num_scalar_prefetch=1,
            grid=(n_steps,),
            in_specs=[
                pl.BlockSpec((1, 1, tm * TOP_K), lambda i, pe: (i, 0, 0), memory_space=pltpu.SMEM),
                pl.BlockSpec((tm * ch, LANES), lambda i, pe: (i, 0)),
            ],
            out_specs=pl.BlockSpec(memory_space=pl.ANY),
            scratch_shapes=[pltpu.VMEM((MOE_TB * ch, LANES), F32), pltpu.SemaphoreType.DMA,
                            pltpu.SemaphoreType.DMA],
        ),
        out_shape=jax.ShapeDtypeStruct((n_slots * ch, LANES), h_rows.dtype),
        compiler_params=_cparams("arbitrary"),
        name="moe_dispatch",
    )(pad_end, slot_flat.reshape(n_steps, 1, tm * TOP_K), h_rows)


def _expert_kernel(be_ref, nb_ref, x_ref, wg_ref, wu_ref, wd_ref, o_ref, wg_c, wu_c, wd_c, *, ch):
    i = pl.program_id(0)

    @pl.when((i < nb_ref[0]) & ((i == 0) | (be_ref[i] != be_ref[jnp.maximum(i - 1, 0)])))
    def _():
        wg_c[...] = wg_ref[0, 0].astype(BF16)
        wu_c[...] = wu_ref[0, 0].astype(BF16)
        wd_c[...] = wd_ref[0, 0].astype(BF16)

    @pl.when(i < nb_ref[0])
    def _():
        x = _load_token_rows(x_ref, MOE_TB, ch).astype(BF16)
        a = _dot(x, wg_c[...])
        u = _dot(x, wu_c[...])
        hid = (a * jax.nn.sigmoid(a) * u).astype(BF16)
        _store_token_rows(o_ref, _dot(hid, wd_c[...]))

    @pl.when(i >= nb_ref[0])
    def _():
        o_ref[...] = jnp.zeros_like(o_ref)


def _expert_blocks(block_expert, n_used, xb_rows, w_gate, w_up, w_down, layer):
    d, hid = w_gate.shape[2], w_gate.shape[3]
    ch = d // LANES
    n_slots = xb_rows.shape[0] // ch
    return pl.pallas_call(
        functools.partial(_expert_kernel, ch=ch),
        grid_spec=pltpu.PrefetchScalarGridSpec(
            num_scalar_prefetch=2,
            grid=(n_slots // MOE_TB,),
            in_specs=[
                pl.BlockSpec((MOE_TB * ch, LANES), lambda i, be, nb: (i, 0)),
                pl.BlockSpec((1, 1, d, hid), lambda i, be, nb: (layer, be[i], 0, 0)),
                pl.BlockSpec((1, 1, d, hid), lambda i, be, nb: (layer, be[i], 0, 0)),
                pl.BlockSpec((1, 1, hid, d), lambda i, be, nb: (layer, be[i], 0, 0)),
            ],
            out_specs=pl.BlockSpec((MOE_TB * ch, LANES), lambda i, be, nb: (i, 0)),
            scratch_shapes=[pltpu.VMEM((d, hid), BF16), pltpu.VMEM((d, hid), BF16),
                            pltpu.VMEM((hid, d), BF16)],
        ),
        out_shape=jax.ShapeDtypeStruct((n_slots * ch, LANES), F32),
        compiler_params=_cparams("arbitrary"),
        name="moe_experts",
    )(block_expert, n_used, xb_rows, w_gate, w_up, w_down)


def _combine_kernel(slot_ref, slot_next_ref, x_ref, gate_ref, yb_hbm, o_ref, buf, sem, *, ch):
    i = pl.program_id(0)
    last = pl.num_programs(0) - 1
    tm = x_ref.shape[0]

    def issue(idx_ref, parity):
        def start(r, _):
            for k in range(TOP_K):
                _token_copy(yb_hbm, buf.at[parity, k], idx_ref[0, 0, r * TOP_K + k], r,
                            sem.at[parity], ch).start(priority=k % 2)
            return 0

        lax.fori_loop(0, tm, start, 0, unroll=DMA_ISSUE_UNROLL)

    @pl.when(i == 0)
    def _():
        issue(slot_ref, 0)

    @pl.when(i < last)
    def _():
        issue(slot_next_ref, (i + 1) % 2)

    par = i % 2
    for k in range(TOP_K):
        pltpu.make_async_copy(yb_hbm.at[pl.ds(0, tm * ch)], buf.at[par, k], sem.at[par]).wait()
    y = (_load_token_rows(buf.at[par, 0], tm, ch) * gate_ref[:, 0:1]
         + _load_token_rows(buf.at[par, 1], tm, ch) * gate_ref[:, 1:2])
    o_ref[...] = x_ref[...] + y


def _combine(slot_flat, x2, gates, yb_rows):
    n, d = x2.shape
    ch = d // LANES
    tm = min(256, n)
    n_steps = n // tm
    slots3 = slot_flat.reshape(n_steps, 1, tm * TOP_K)
    return pl.pallas_call(
        functools.partial(_combine_kernel, ch=ch),
        grid=(n_steps,),
        in_specs=[
            pl.BlockSpec((1, 1, tm * TOP_K), lambda i: (i, 0, 0), memory_space=pltpu.SMEM),
            pl.BlockSpec((1, 1, tm * TOP_K), lambda i: (jnp.minimum(i + 1, n_steps - 1), 0, 0),
                         memory_space=pltpu.SMEM),
            pl.BlockSpec((tm, d), lambda i: (i, 0)),
            pl.BlockSpec((tm, TOP_K), lambda i: (i, 0)),
            pl.BlockSpec(memory_space=pl.ANY),
        ],
        out_specs=pl.BlockSpec((tm, d), lambda i: (i, 0)),
        scratch_shapes=[pltpu.VMEM((2, TOP_K, tm * ch, LANES), F32),
                        pltpu.SemaphoreType.DMA((2,))],
        out_shape=jax.ShapeDtypeStruct((n, d), F32),
        compiler_params=_cparams("arbitrary"),
        name="moe_combine",
    )(slots3, slots3, x2, gates, yb_rows)


def _routing_plan(expert_idx_t, rank_t, counts, n_tok):
    padded = (counts + MOE_TB - 1) // MOE_TB * MOE_TB
    pad_end = jnp.cumsum(padded)
    pad_start = pad_end - padded
    start_t = jnp.sum(jnp.where(expert_idx_t[:, :, None] == jnp.arange(N_EXPERTS), pad_start, 0),
                      axis=-1)
    slot = jnp.transpose(start_t + rank_t).reshape(n_tok * TOP_K)
    n_blocks = -(-(n_tok * TOP_K) // MOE_TB) + N_EXPERTS
    block_expert = jnp.minimum(
        jnp.sum(pad_end[None, :] <= (jnp.arange(n_blocks) * MOE_TB)[:, None], axis=1),
        N_EXPERTS - 1).astype(jnp.int32)
    n_used = (pad_end[-1] // MOE_TB).astype(jnp.int32).reshape(1)
    return slot.astype(jnp.int32), pad_end.astype(jnp.int32), block_expert, n_used, n_blocks * MOE_TB


def _moe(x2, h2, expert_idx_t, gates_t, rank_t, counts, w_gate, w_up, w_down, layer):
    n_tok = x2.shape[0]
    slot, pad_end, block_expert, n_used, n_slots = _routing_plan(expert_idx_t, rank_t, counts, n_tok)
    xb = _dispatch(pad_end, slot, h2, n_slots, x2.shape[1] // LANES)
    yb = _expert_blocks(block_expert, n_used, xb, w_gate, w_up, w_down, layer)
    return _combine(slot, x2, jnp.transpose(gates_t), yb)


def _final_norm_kernel(x_ref, g_ref, o_ref):
    x = x_ref[...]
    o_ref[...] = x * lax.rsqrt(jnp.mean(x * x, axis=-1, keepdims=True) + RMS_EPS) * g_ref[...]


def _final_norm(x2, g):
    n, d = x2.shape
    tm = min(ROW_TILE, n)
    return pl.pallas_call(
        _final_norm_kernel,
        grid=(n // tm,),
        in_specs=[pl.BlockSpec((tm, d), lambda i: (i, 0)), pl.BlockSpec((1, d), lambda i: (0, 0))],
        out_specs=pl.BlockSpec((tm, d), lambda i: (i, 0)),
        out_shape=jax.ShapeDtypeStruct((n, d), F32),
        compiler_params=_cparams("parallel"),
        name="final_norm",
    )(x2, g)


def _prepare_params(w_in, gm_w_s, gm_b, rw_mu, rw_w0, rw_w2, rw_a0, rw_a2, rw_g2, rw_k_k, rw_k_a,
                    rw_r_k, rw_gn_w, rw_gn_b, fx_b_f, w_out, router_group_w, router_group_b,
                    router_expert_w, router_expert_b):
    depth, d = w_in.shape[0], w_in.shape[1]
    n_main = 3 * W_MIX + 2 * W_MIX + RW_COLS + 3 * W_MIX
    t = lambda a: jnp.swapaxes(a, 1, 2)
    pad_rows = lambda a, rows: jnp.concatenate(
        [a, jnp.zeros((depth, rows - a.shape[1]) + a.shape[2:], a.dtype)], axis=1)
    row = lambda a: a.reshape(depth, 1, -1)
    wr = jnp.concatenate([pad_rows(t(router_group_w), SUBLANES), t(router_expert_w)], axis=1)
    br = jnp.concatenate([pad_rows(router_group_b[:, :, None], SUBLANES),
                          router_expert_b[:, :, None]], axis=1)
    return dict(
        w_main=w_in[:, :, :n_main].astype(BF16),
        w_gate_t=pad_rows(t(w_in[:, :, n_main:n_main + N_HEADS]), SUBLANES).astype(BF16),
        fx_bias=pad_rows(fx_b_f[:, :, None], SUBLANES),
        gm_w=jnp.transpose(gm_w_s, (0, 2, 1, 3)).reshape(depth, GM_CHUNK, N_HEADS * GM_CHUNK),
        gm_b=jnp.repeat(t(gm_b), HEAD_DIM, axis=2),
        mu=row(rw_mu), w0=row(rw_w0), w2=rw_w2.astype(BF16), a0=row(rw_a0),
        a2=rw_a2.astype(BF16), g2=rw_g2.astype(BF16), k_k=row(rw_k_k), k_a=row(rw_k_a),
        r_k=row(rw_r_k), gn_w=row(rw_gn_w), gn_b=row(rw_gn_b),
        w_out=w_out.astype(BF16), wr=wr, br=br,
    )


def _layer_params(l, w_in, gm_w_s, gm_b, rw_mu, rw_w0, rw_w2, rw_a0, rw_a2, rw_g2, rw_k_k, rw_k_a,
                  rw_r_k, rw_gn_w, rw_gn_b, fx_b_f, w_out, router_group_w, router_group_b,
                  router_expert_w, router_expert_b):
    allp = _prepare_params(w_in, gm_w_s, gm_b, rw_mu, rw_w0, rw_w2, rw_a0, rw_a2, rw_g2, rw_k_k,
                           rw_k_a, rw_r_k, rw_gn_w, rw_gn_b, fx_b_f, w_out, router_group_w,
                           router_group_b, router_expert_w, router_expert_b)
    return {k: v[l] for k, v in allp.items()}


def kernel(x, norm1_g, w_in, gm_w_s, gm_b, rw_mu, rw_w0, rw_w2, rw_a0, rw_a2, rw_g2, rw_k_k, rw_k_a, rw_r_k, rw_gn_w, rw_gn_b, fx_b_f, w_out, norm2_g, router_group_w, router_group_b, router_expert_w, router_expert_b, exp_w_gate, exp_w_up, exp_w_down, final_norm_g):
    bsz, seq, d = x.shape
    depth = w_in.shape[0]
    assert seq % min(ATT_TQB, seq) == 0 and seq % RW_CHUNK == 0 and seq % (2 * LANES) == 0
    x2 = x.reshape(bsz * seq, d)
    hd = jnp.arange(W_MIX) // HEAD_DIM
    ones_bd = (hd[:, None] == hd[None, :]).astype(BF16)
    allp = _prepare_params(w_in, gm_w_s, gm_b, rw_mu, rw_w0, rw_w2, rw_a0, rw_a2, rw_g2, rw_k_k,
                           rw_k_a, rw_r_k, rw_gn_w, rw_gn_b, fx_b_f, w_out, router_group_w,
                           router_group_b, router_expert_w, router_expert_b)
    for l in range(depth):
        lp = {k: v[l] for k, v in allp.items()}
        p_sb, p_gm, p_rw, p_fx, grow = _norm_inproj(x2, norm1_g[l].reshape(1, d), lp["w_main"],
                                                    lp["w_gate_t"])
        y_sb = _sb_attention(p_sb, bsz, seq)
        y_gm = _gm_mix(p_gm, lp["gm_w"], lp["gm_b"], ones_bd)
        prep = _rw_prep(p_rw, seq, lp["mu"], lp["w0"], lp["w2"], lp["a0"], lp["a2"], lp["g2"],
                        lp["k_k"], lp["k_a"], lp["r_k"], ones_bd)
        intra = _rw_intra(prep[:6], bsz, seq)
        y_rw = _rw_state(intra, prep[6], prep[7], lp["gn_w"], lp["gn_b"], ones_bd, bsz, seq)
        grow3 = jnp.transpose(grow.reshape(SUBLANES, bsz, seq), (1, 0, 2))
        cum_row = _fx_cum(grow3, lp["fx_bias"])
        y_fx = _fx_attention(p_fx, cum_row, bsz, seq)
        x2, h2, eidx_t, gates_t, rank_t, cnt = _outproj_router(
            x2, (y_sb, y_gm, y_rw, y_fx), lp["w_out"], norm2_g[l].reshape(1, d), lp["wr"], lp["br"])
        x2 = _moe(x2, h2, eidx_t, gates_t, rank_t, cnt[:, 0], exp_w_gate, exp_w_up, exp_w_down, l)
    return _final_norm(x2, final_norm_g.reshape(1, d)).reshape(bsz, seq, d)
```

```python
import functools

import jax
import jax.numpy as jnp
from jax import lax
from jax.experimental import pallas as pl
from jax.experimental.pallas import tpu as pltpu

F32 = jnp.float32
BF16 = jnp.bfloat16
HIGHEST = lax.Precision.HIGHEST

HEAD_DIM = 64
N_HEADS = 4
W_MIX = N_HEADS * HEAD_DIM
GM_CHUNK = 128
GM_ROWS = 1024
RW_DECAY_LORA = 64
RW_AAA_LORA = 64
RW_GATE_LORA = 128
RW_COLS = 3 * W_MIX + RW_DECAY_LORA + RW_AAA_LORA + RW_GATE_LORA
N_EXPERT_GROUPS = 4
EXPERTS_PER_GROUP = 8
N_EXPERTS = N_EXPERT_GROUPS * EXPERTS_PER_GROUP
TOP_K = 2
RMS_EPS = 1e-6
LN_EPS = 1e-5
GN_EPS = 64e-5
L2_EPS = 1e-12
LOG2_E = 1.4426950408889634

V7X_VMEM_LIMIT_BYTES = 56 * 1024 * 1024
LANES = 128
SUBLANES = 8

ATT_TK = 128
FX_TK = 256
SB_TILES_PER_STEP = 2
ATT_TQB = 512
ATT_W = N_HEADS * LANES
RW_CHUNK = 64
RW_CPS = 8
RW_BATCH_GROUP = 8
ROW_TILE = 512
MOE_TB = 512
DMA_ISSUE_UNROLL = 8


def _cparams(*sem):
    return pltpu.CompilerParams(dimension_semantics=sem, vmem_limit_bytes=V7X_VMEM_LIMIT_BYTES)


def _dot(a, b, precision=None):
    return jnp.dot(a, b, preferred_element_type=F32, precision=precision)


def _dot_nt(a, b, precision=None):
    return lax.dot_general(a, b, (((1,), (1,)), ((), ())), preferred_element_type=F32,
                           precision=precision)


def _dot_tn(a, b, precision=None):
    return lax.dot_general(a, b, (((0,), (0,)), ((), ())), preferred_element_type=F32,
                           precision=precision)


def _iota2(shape, dim):
    return lax.broadcasted_iota(jnp.int32, shape, dim)


def _store_heads_padded(ref, col0, src, fill):
    low = _iota2((src.shape[0], LANES), 1) < HEAD_DIM
    for pair in range(N_HEADS // 2):
        slab = src[:, pair * LANES:(pair + 1) * LANES]
        swapped = pltpu.roll(slab, HEAD_DIM, axis=1)
        for odd, val in enumerate((slab, swapped)):
            c = col0 + (2 * pair + odd) * LANES
            ref[:, c:c + LANES] = jnp.where(low, val, fill).astype(ref.dtype)


def _norm_inproj_kernel(x_ref, g_ref, w_ref, wgt_ref, sb_ref, gm_ref, rw_ref, fx_ref, grow_ref):
    x = x_ref[...]
    h = x * lax.rsqrt(jnp.mean(x * x, axis=-1, keepdims=True) + RMS_EPS) * g_ref[...]
    hb = h.astype(BF16)
    p = _dot(hb, w_ref[...])
    scale = HEAD_DIM ** -0.5
    wp = N_HEADS * LANES
    o = 0
    _store_heads_padded(sb_ref, 0, p[:, o:o + W_MIX] * scale, 0.0)
    _store_heads_padded(sb_ref, wp, p[:, o + W_MIX:o + 2 * W_MIX], 0.0)
    _store_heads_padded(sb_ref, 2 * wp, p[:, o + 2 * W_MIX:o + 3 * W_MIX], 0.0)
    o += 3 * W_MIX
    gm_ref[...] = p[:, o:o + 2 * W_MIX]
    o += 2 * W_MIX
    rw_ref[...] = p[:, o:o + RW_COLS]
    o += RW_COLS
    _store_heads_padded(fx_ref, 0, p[:, o:o + W_MIX] * scale, 0.0)
    _store_heads_padded(fx_ref, wp, p[:, o + W_MIX:o + 2 * W_MIX], 0.0)
    _store_heads_padded(fx_ref, 2 * wp, p[:, o + 2 * W_MIX:o + 3 * W_MIX], 1.0)
    grow_ref[...] = _dot_nt(wgt_ref[...], hb)


def _norm_inproj(x2, g, w_main, w_gate_t):
    n, d = x2.shape
    tm = min(ROW_TILE, n)
    cols = w_main.shape[1]
    return pl.pallas_call(
        _norm_inproj_kernel,
        grid=(n // tm,),
        in_specs=[
            pl.BlockSpec((tm, d), lambda i: (i, 0)),
            pl.BlockSpec((1, d), lambda i: (0, 0)),
            pl.BlockSpec((d, cols), lambda i: (0, 0)),
            pl.BlockSpec((SUBLANES, d), lambda i: (0, 0)),
        ],
        out_specs=[
            pl.BlockSpec((tm, 3 * ATT_W), lambda i: (i, 0)),
            pl.BlockSpec((tm, 2 * W_MIX), lambda i: (i, 0)),
            pl.BlockSpec((tm, RW_COLS), lambda i: (i, 0)),
            pl.BlockSpec((tm, 3 * ATT_W), lambda i: (i, 0)),
            pl.BlockSpec((SUBLANES, tm), lambda i: (0, i)),
        ],
        out_shape=[
            jax.ShapeDtypeStruct((n, 3 * ATT_W), BF16),
            jax.ShapeDtypeStruct((n, 2 * W_MIX), F32),
            jax.ShapeDtypeStruct((n, RW_COLS), F32),
            jax.ShapeDtypeStruct((n, 3 * ATT_W), BF16),
            jax.ShapeDtypeStruct((SUBLANES, n), F32),
        ],
        compiler_params=_cparams("parallel"),
        name="norm_inproj",
    )(x2, g, w_main, w_gate_t)


def _sb_kernel(q_ref, k_ref, v_ref, tri_ref, o_ref, c_ref, acc_ref):
    qi = pl.program_id(1)
    tqb = q_ref.shape[1]
    hl = lambda h: slice(h * LANES, (h + 1) * LANES)

    def key_tiles(tiles):
        meta, zs, rrs = [], [], []
        for j, off in tiles:
            r0 = 0 if off is None else off * ATT_TK
            rows = tqb - r0
            strict = (None if off is None
                      else _iota2((rows, ATT_TK), 1) < _iota2((rows, ATT_TK), 0))
            meta.append((pl.multiple_of(j * ATT_TK, ATT_TK), r0, rows, strict))
        for start, r0, rows, strict in meta:
            zs.append([_dot_nt(q_ref[0, r0:, hl(h)], k_ref[0, pl.ds(start, ATT_TK), hl(h)])
                       for h in range(N_HEADS)])
        for (start, r0, rows, strict), z4 in zip(meta, zs):
            parts = []
            for z in z4:
                sp = jnp.maximum(z, 0.0) + jnp.log(1.0 + jnp.exp2(jnp.abs(z) * -LOG2_E))
                if strict is not None:
                    sp = jnp.where(strict, sp, 0.0)
                hi = sp.astype(BF16)
                lo = (sp - hi.astype(F32)).astype(BF16)
                parts.append(jnp.concatenate([hi, lo], axis=1))
            rrs.append(_dot(jnp.concatenate(parts, axis=0), tri_ref[...]))
        for (start, r0, rows, strict), z4, rr in zip(meta, zs, rrs):
            ps = []
            for h in range(N_HEADS):
                rh = rr[h * rows:(h + 1) * rows]
                p = jnp.exp(z4[h] - rh[:, :ATT_TK] - c_ref[h, r0:, :])
                if strict is not None:
                    p = jnp.where(strict, p, 0.0)
                ps.append(p.astype(BF16))
                c_ref[h, r0:, :] += rh[:, ATT_TK:]
            pv = [_dot(ps[h], v_ref[0, pl.ds(start, ATT_TK), hl(h)]) for h in range(N_HEADS)]
            for h in range(N_HEADS):
                acc_ref[h, r0:, :] += pv[h]

    c_ref[...] = jnp.zeros_like(c_ref)
    acc_ref[...] = jnp.zeros_like(acc_ref)
    n_diag = tqb // ATT_TK
    group = SB_TILES_PER_STEP if n_diag % SB_TILES_PER_STEP == 0 else 1
    for first in reversed(range(0, n_diag, group)):
        key_tiles([(qi * n_diag + off, off) for off in reversed(range(first, first + group))])

    def body(jj, carry):
        right = qi * n_diag - 1 - jj * group
        key_tiles([(right - g, None) for g in range(group)])
        return carry

    lax.fori_loop(0, qi * n_diag // group, body, 0)
    for h in range(N_HEADS):
        o_ref[0, :, h * HEAD_DIM:(h + 1) * HEAD_DIM] = acc_ref[h, :, :HEAD_DIM].astype(o_ref.dtype)


def _sb_attention(qkv, bsz, seq):
    qkv3 = qkv.reshape(bsz, seq, 3 * ATT_W)
    r = jnp.arange(ATT_TK)
    tri = (r[:, None] >= r[None, :]).astype(BF16)
    blk = jnp.concatenate([tri, jnp.ones((ATT_TK, ATT_TK), BF16)], axis=1)
    tri2 = jnp.concatenate([blk, blk], axis=0)
    tqb = min(ATT_TQB, seq)
    out = pl.pallas_call(
        _sb_kernel,
        grid=(bsz, seq // tqb),
        in_specs=[
            pl.BlockSpec((1, tqb, ATT_W), lambda b, i: (b, i, 0)),
            pl.BlockSpec((1, seq, ATT_W), lambda b, i: (b, 0, 1)),
            pl.BlockSpec((1, seq, ATT_W), lambda b, i: (b, 0, 2)),
            pl.BlockSpec((2 * ATT_TK, 2 * ATT_TK), lambda b, i: (0, 0)),
        ],
        out_specs=pl.BlockSpec((1, tqb, W_MIX), lambda b, i: (b, i, 0)),
        out_shape=jax.ShapeDtypeStruct((bsz, seq, W_MIX), BF16),
        scratch_shapes=[pltpu.VMEM((N_HEADS, tqb, ATT_TK), F32),
                        pltpu.VMEM((N_HEADS, tqb, LANES), F32)],
        compiler_params=_cparams("parallel", "parallel"),
        name="sb_attention",
    )(qkv3, qkv3, qkv3, tri2)
    return out.reshape(bsz * seq, W_MIX)


def _fx_cum_kernel(g_ref, b_ref, cum_ref):
    seq = g_ref.shape[2]
    blk = 2 * LANES
    r = _iota2((blk, blk), 0)
    c = _iota2((blk, blk), 1)
    tri = (r <= c).astype(F32)
    carry = jnp.zeros((SUBLANES, 1), F32)
    for s in range(seq // blk):
        x = g_ref[0, :, s * blk:(s + 1) * blk] + b_ref[...]
        lf = -(jnp.maximum(-x, 0.0) + jnp.log(1.0 + jnp.exp(-jnp.abs(x))))
        cs = _dot(lf, tri, precision=HIGHEST) + carry
        cum_ref[0, :, s * blk:(s + 1) * blk] = cs
        carry = cs[:, blk - 1:blk]


def _fx_cum(grow3, bias_col):
    bsz, _, seq = grow3.shape
    return pl.pallas_call(
        _fx_cum_kernel,
        grid=(bsz,),
        in_specs=[
            pl.BlockSpec((1, SUBLANES, seq), lambda b: (b, 0, 0)),
            pl.BlockSpec((SUBLANES, 1), lambda b: (0, 0)),
        ],
        out_specs=pl.BlockSpec((1, SUBLANES, seq), lambda b: (b, 0, 0)),
        out_shape=jax.ShapeDtypeStruct((bsz, SUBLANES, seq), F32),
        compiler_params=_cparams("parallel"),
        name="fx_cum",
    )(grow3, bias_col)


def _fx_kernel(q_ref, k_ref, v_ref, crow_ref, ccol_ref, o_ref, m_ref, acc_ref, ct_ref):
    qi = pl.program_id(1)
    tqb = q_ref.shape[1]
    hl = lambda h: slice(h * LANES, (h + 1) * LANES)

    def key_tile(j, off):
        r0 = 0 if off is None else off * FX_TK
        rows = tqb - r0
        start = pl.multiple_of(j * FX_TK, FX_TK)
        halves = range(FX_TK // LANES)
        if off is not None:
            causal = [_iota2((rows, LANES), 1) + s * LANES <= _iota2((rows, LANES), 0)
                      for s in halves]
        heads = range(N_HEADS)
        z = [_dot_nt(q_ref[0, r0:, hl(h)], k_ref[0, pl.ds(start, FX_TK), hl(h)]) for h in heads]
        ps, alphas = [], []
        for h in heads:
            ct = ct_ref[h, r0:, :]
            xs = []
            for s in halves:
                cs = crow_ref[0, h:h + 1, pl.ds(pl.multiple_of(start + s * LANES, LANES), LANES)]
                x = z[h][:, s * LANES:(s + 1) * LANES] + ct - cs
                if off is not None:
                    x = jnp.where(causal[s], x, -jnp.inf)
                xs.append(x)
            m_old = m_ref[h, r0:, :]
            m_new = jnp.maximum(m_old, jnp.max(functools.reduce(jnp.maximum, xs), axis=-1,
                                               keepdims=True))
            m_ref[h, r0:, :] = m_new
            ps.append(jnp.concatenate([jnp.exp(x - m_new).astype(BF16) for x in xs], axis=1))
            alphas.append(jnp.exp(m_old - m_new))
        pv = [_dot(ps[h], v_ref[0, pl.ds(start, FX_TK), hl(h)]) for h in heads]
        for h in heads:
            acc_ref[h, r0:, :] = alphas[h] * acc_ref[h, r0:, :] + pv[h]

    m_ref[...] = jnp.full_like(m_ref, -jnp.inf)
    acc_ref[...] = jnp.zeros_like(acc_ref)
    for h in range(N_HEADS):
        ct_ref[h] = jnp.broadcast_to(ccol_ref[0, :, h:h + 1], (tqb, LANES))
    n_diag = tqb // FX_TK

    def body(j, carry):
        key_tile(j, None)
        return carry

    lax.fori_loop(0, qi * n_diag, body, 0)
    for off in range(n_diag):
        key_tile(qi * n_diag + off, off)
    for h in range(N_HEADS):
        acc = acc_ref[h]
        den = pltpu.roll(acc, HEAD_DIM, axis=1)
        o_ref[0, :, h * HEAD_DIM:(h + 1) * HEAD_DIM] = (acc / den)[:, :HEAD_DIM].astype(o_ref.dtype)


def _fx_attention(qkv, cum_row, bsz, seq):
    qkv3 = qkv.reshape(bsz, seq, 3 * ATT_W)
    cum_col = jnp.transpose(cum_row, (0, 2, 1))
    tqb = min(ATT_TQB, seq)
    out = pl.pallas_call(
        _fx_kernel,
        grid=(bsz, seq // tqb),
        in_specs=[
            pl.BlockSpec((1, tqb, ATT_W), lambda b, i: (b, i, 0)),
            pl.BlockSpec((1, seq, ATT_W), lambda b, i: (b, 0, 1)),
            pl.BlockSpec((1, seq, ATT_W), lambda b, i: (b, 0, 2)),
            pl.BlockSpec((1, SUBLANES, seq), lambda b, i: (b, 0, 0)),
            pl.BlockSpec((1, tqb, SUBLANES), lambda b, i: (b, i, 0)),
        ],
        out_specs=pl.BlockSpec((1, tqb, W_MIX), lambda b, i: (b, i, 0)),
        out_shape=jax.ShapeDtypeStruct((bsz, seq, W_MIX), BF16),
        scratch_shapes=[pltpu.VMEM((N_HEADS, tqb, LANES), F32)] * 3,
        compiler_params=_cparams("parallel", "parallel"),
        name="fx_attention",
    )(qkv3, qkv3, qkv3, cum_row, cum_col)
    return out.reshape(bsz * seq, W_MIX)


def _gm_kernel(p_ref, w_ref, b_ref, ones_ref, o_ref):
    hid = jax.nn.gelu(p_ref[...])
    u = hid[:, :W_MIX]
    v = hid[:, W_MIX:]
    inv_n = 1.0 / HEAD_DIM
    mean = _mm_lhs2(v, ones_ref[...], _NN) * inv_n
    vc = v - mean
    var = _mm_lhs2(vc * vc, ones_ref[...], _NN) * inv_n
    vn = (vc * lax.rsqrt(var + LN_EPS)).astype(BF16)
    wshape = (GM_CHUNK, N_HEADS * GM_CHUNK)
    lower = jnp.bitwise_and(_iota2(wshape, 1), GM_CHUNK - 1) <= _iota2(wshape, 0)
    w = jnp.where(lower, w_ref[...], 0.0).astype(BF16)
    chunks = range(p_ref.shape[0] // GM_CHUNK)
    rows = [slice(c * GM_CHUNK, (c + 1) * GM_CHUNK) for c in chunks]
    mixed = [_dot(w, _head_blocks(vn[rows[c]], HEAD_DIM)) for c in chunks]
    for c in chunks:
        o_ref[rows[c], :] = (u[rows[c]] * (mixed[c] + b_ref[...])).astype(o_ref.dtype)


def _gm_mix(p_gm, w_cat, b_full, ones_bd):
    n = p_gm.shape[0]
    tm = min(GM_ROWS, n)
    return pl.pallas_call(
        _gm_kernel,
        grid=(n // tm,),
        in_specs=[
            pl.BlockSpec((tm, 2 * W_MIX), lambda i: (i, 0)),
            pl.BlockSpec((GM_CHUNK, N_HEADS * GM_CHUNK), lambda i: (0, 0)),
            pl.BlockSpec((GM_CHUNK, W_MIX), lambda i: (0, 0)),
            pl.BlockSpec((W_MIX, W_MIX), lambda i: (0, 0)),
        ],
        out_specs=pl.BlockSpec((tm, W_MIX), lambda i: (i, 0)),
        out_shape=jax.ShapeDtypeStruct((n, W_MIX), BF16),
        compiler_params=_cparams("parallel"),
        name="gm_mix",
    )(p_gm, w_cat, b_full, ones_bd)


def _head_sum(x, ones_bd):
    return _mm_lhs2(x, ones_bd, _NN)


def _rw_prep_math(p, prev_row, mu_ref, w0_ref, w2_ref, a0_ref, a2_ref, g2_ref, kk_ref, ka_ref,
                  rk_ref, bd_ref):
    rows = _iota2(p.shape, 0)
    prev = jnp.where(rows == 0, prev_row, pltpu.roll(p, 1, axis=0))
    p = p + (prev - p) * mu_ref[...]
    r = p[:, 0:W_MIX]
    k = p[:, W_MIX:2 * W_MIX]
    v = p[:, 2 * W_MIX:3 * W_MIX]
    o = 3 * W_MIX
    xw = p[:, o:o + RW_DECAY_LORA]
    o += RW_DECAY_LORA
    xa = p[:, o:o + RW_AAA_LORA]
    o += RW_AAA_LORA
    xg = p[:, o:o + RW_GATE_LORA]
    wpre = -(w0_ref[...] + _dot(jnp.tanh(xw).astype(BF16), w2_ref[...]))
    w = -(jnp.maximum(wpre, 0.0) + jnp.log(1.0 + jnp.exp(-jnp.abs(wpre)))) - 0.5
    a = jax.nn.sigmoid(a0_ref[...] + _dot(xa.astype(BF16), a2_ref[...]))
    g = _dot(jax.nn.sigmoid(xg).astype(BF16), g2_ref[...])
    kk = k * kk_ref[...]
    nrm = jnp.maximum(jnp.sqrt(_head_sum(kk * kk, bd_ref[...])), L2_EPS)
    kk = kk / nrm
    k2 = k * (1.0 + (a - 1.0) * ka_ref[...])
    ld = -jnp.exp(w)
    bonus = _head_sum(r * k2 * rk_ref[...], bd_ref[...]) * v
    return r, ld, k2, v, kk, kk * a, bonus, g


_NN = (((1,), (0,)), ((), ()))
_NT = (((1,), (1,)), ((), ()))


def _split2(x):
    hi = x.astype(BF16)
    return hi, (x - hi.astype(F32)).astype(BF16)


def _mm_lhs2(a, b, dims):
    m = a.shape[0]
    hi, lo = _split2(a)
    r = lax.dot_general(jnp.concatenate([hi, lo], axis=0), b, dims, preferred_element_type=F32)
    return r[:m] + r[m:]


def _lane_head(shape, width):
    return lax.shift_right_logical(_iota2(shape, 1), width.bit_length() - 1)


def _head_blocks(x, width):
    lh = _lane_head(x.shape, width)
    return jnp.concatenate([jnp.where(lh == h, x, jnp.zeros_like(x)) for h in range(N_HEADS)],
                           axis=0)


def _head_diag(full, width):
    rows = full.shape[0] // N_HEADS
    lh = _lane_head((rows, full.shape[1]), width)
    out = jnp.zeros((rows, full.shape[1]), F32)
    for h in range(N_HEADS):
        out = out + jnp.where(lh == h, full[h * rows:(h + 1) * rows], 0.0)
    return out


def _tn_lhs2(a, b_bf16):
    hi = a.astype(BF16).astype(F32)
    b = b_bf16.astype(F32)
    return _dot_tn(hi, b) + _dot_tn(a - hi, b)


def _rw_intra_kernel(p_ref, prev_ref, mu_ref, w0_ref, w2_ref, a0_ref, a2_ref, g2_ref, kkw_ref,
                     kaw_ref, rkw_ref, bd_ref, w_out, u0_out, y0_out, rt_out, btp_out, mrb_out,
                     g0_out, pc_out, bonus_out, g_out):
    cs = RW_CHUNK
    cw = N_HEADS * cs
    row = _iota2((cs, cw), 0)
    colc = jnp.bitwise_and(_iota2((cs, cw), 1), cs - 1)
    lower = colc <= row
    strict = colc < row
    eye = (colc == row).astype(F32)
    tri3 = (jnp.bitwise_and(_iota2((cs, 3 * cs), 1), cs - 1) <= _iota2((cs, 3 * cs), 0)).astype(BF16)

    prev_row = jnp.where(pl.program_id(1) == 0, 0.0, prev_ref[0, SUBLANES - 1:SUBLANES, :])
    r_all, ld_all, k_all, v_all, kk_all, ba_all, bonus, g = _rw_prep_math(
        p_ref[0], prev_row, mu_ref, w0_ref, w2_ref, a0_ref, a2_ref, g2_ref, kkw_ref, kaw_ref,
        rkw_ref, bd_ref)
    bonus_out[0] = bonus
    g_out[0] = g

    chunks = range(p_ref.shape[1] // cs)
    rows = [slice(c * cs, (c + 1) * cs) for c in chunks]
    each = lambda f: [f(c) for c in chunks]
    hb = lambda x: _head_blocks(x.astype(BF16), HEAD_DIM)

    def running_log_decay(c):
        ld = ld_all[rows[c]]
        l1 = ld.astype(BF16)
        l2, l3 = _split2(ld - l1.astype(F32))
        return _dot(tri3, jnp.concatenate([l1, l2, l3], axis=0))

    cl = each(running_log_decay)
    p_in = each(lambda c: jnp.exp(cl[c]))
    p_inv = each(lambda c: jnp.exp(-cl[c]))
    at = each(lambda c: -kk_all[rows[c]] * jnp.exp(cl[c] - ld_all[rows[c]]))
    bt = each(lambda c: ba_all[rows[c]] * p_inv[c])
    kt = each(lambda c: k_all[rows[c]] * p_inv[c])
    rt = each(lambda c: r_all[rows[c]] * p_in[c])
    pc = each(lambda c: p_in[c][cs - 1:cs, :])

    lhs = each(lambda c: jnp.concatenate([at[c], rt[c]], axis=0))
    ab_mb = each(lambda c: _mm_lhs2(lhs[c], hb(bt[c]), _NT))
    ak_mk = each(lambda c: _mm_lhs2(lhs[c], hb(kt[c]), _NT))
    a_ab = each(lambda c: jnp.where(strict, ab_mb[c][:cs], 0.0))
    m_rb = each(lambda c: jnp.where(lower, ab_mb[c][cs:], 0.0))
    a_ak = each(lambda c: jnp.where(strict, ak_mk[c][:cs], 0.0))
    m_rk = each(lambda c: jnp.where(lower, ak_mk[c][cs:], 0.0))

    inv = each(lambda c: eye + a_ab[c])
    pw = a_ab
    pw_blocks = each(lambda c: _head_blocks(pw[c].astype(BF16), cs))
    steps = 1
    while steps * 2 < cs:
        pw = each(lambda c: _mm_lhs2(pw[c], pw_blocks[c], _NN))
        pw_blocks = each(lambda c: _head_blocks(pw[c].astype(BF16), cs))
        inv = each(lambda c: inv[c] + _mm_lhs2(inv[c], pw_blocks[c], _NN))
        steps *= 2

    akv_mkv = each(lambda c: _mm_lhs2(jnp.concatenate([a_ak[c], m_rk[c]], axis=0),
                                      hb(v_all[rows[c]]), _NN))
    wu = each(lambda c: _mm_lhs2(
        inv[c], jnp.concatenate([hb(at[c]), hb(akv_mkv[c][:cs])], axis=1), _NN))
    g0 = each(lambda c: _head_diag(_tn_lhs2(v_all[rows[c]], kt[c].astype(BF16)), HEAD_DIM))
    for c in chunks:
        w_out[0, rows[c], :] = wu[c][:, :W_MIX]
        u0_out[0, rows[c], :] = wu[c][:, W_MIX:]
        y0_out[0, rows[c], :] = akv_mkv[c][cs:]
        rt_out[0, rows[c], :] = rt[c]
        btp_out[0, rows[c], :] = bt[c] * pc[c]
        mrb_out[0, rows[c], :] = m_rb[c]
        g0_out[0, rows[c], :] = g0[c] * pc[c]
        pc_out[0, c * SUBLANES:(c + 1) * SUBLANES, :] = jnp.broadcast_to(pc[c], (SUBLANES, W_MIX))


def _rw_intra(p_rw, bsz, seq, mu, w0, w2, a0, a2, g2, k_k, k_a, r_k, ones_bd):
    p3 = p_rw.reshape(bsz, seq, RW_COLS)
    rows = min(RW_CHUNK * RW_CPS, seq)
    cps = rows // RW_CHUNK
    spec = pl.BlockSpec((1, rows, W_MIX), lambda b, c: (b, c, 0))
    pc_spec = pl.BlockSpec((1, SUBLANES * cps, W_MIX), lambda b, c: (b, c, 0))
    vec = lambda width: pl.BlockSpec((1, width), lambda b, c: (0, 0))
    mat = lambda rws: pl.BlockSpec((rws, W_MIX), lambda b, c: (0, 0))
    big = jax.ShapeDtypeStruct((bsz, seq, W_MIX), F32)
    return pl.pallas_call(
        _rw_intra_kernel,
        grid=(bsz, seq // rows),
        in_specs=[
            pl.BlockSpec((1, rows, RW_COLS), lambda b, c: (b, c, 0)),
            pl.BlockSpec((1, SUBLANES, RW_COLS),
                         lambda b, c: (b, jnp.maximum(c * (rows // SUBLANES) - 1, 0), 0)),
            vec(RW_COLS), vec(W_MIX), mat(RW_DECAY_LORA), vec(W_MIX), mat(RW_AAA_LORA),
            mat(RW_GATE_LORA), vec(W_MIX), vec(W_MIX), vec(W_MIX), mat(W_MIX),
        ],
        out_specs=[spec] * 7 + [pc_spec, spec, spec],
        out_shape=[big] * 7 + [jax.ShapeDtypeStruct((bsz, seq // RW_CHUNK * SUBLANES, W_MIX), F32),
                               big, big],
        compiler_params=_cparams("parallel", "parallel"),
        name="rw_intra",
    )(p3, p3, mu, w0, w2, a0, a2, g2, k_k, k_a, r_k, ones_bd)


def _rw_state_kernel(w_ref, u0_ref, y0_ref, rt_ref, btp_ref, mrb_ref, g0_ref, pc_ref,
                     bonus_ref, g_ref, gnw_ref, gnb_ref, ones_ref, o_ref, state_ref):
    cs = RW_CHUNK

    @pl.when(pl.program_id(1) == 0)
    def _():
        state_ref[...] = jnp.zeros_like(state_ref)

    batch = range(w_ref.shape[0])
    each = lambda f: [f(b) for b in batch]
    hb = lambda x: _head_blocks(x.astype(BF16), HEAD_DIM)
    s0 = each(lambda b: state_ref[b])
    wr = each(lambda b: _mm_lhs2(jnp.concatenate([w_ref[b], rt_ref[b]], axis=0), hb(s0[b]), _NT))
    u = each(lambda b: wr[b][:cs] + u0_ref[b])
    y = each(lambda b: wr[b][cs:] + _mm_lhs2(mrb_ref[b], hb(u[b]), _NN) + y0_ref[b])
    su = each(lambda b: _head_diag(_tn_lhs2(u[b], btp_ref[b].astype(BF16)), HEAD_DIM))
    for b in batch:
        state_ref[b] = s0[b] * pc_ref[b, 0:1, :] + g0_ref[b] + su[b]

    inv_n = 1.0 / HEAD_DIM
    mean = each(lambda b: _mm_lhs2(y[b], ones_ref[...], _NN) * inv_n)
    yc = each(lambda b: y[b] - mean[b])
    var = each(lambda b: _mm_lhs2(yc[b] * yc[b], ones_ref[...], _NN) * inv_n)
    for b in batch:
        yn = yc[b] * lax.rsqrt(var[b] + GN_EPS) * gnw_ref[...] + gnb_ref[...]
        o_ref[b] = ((yn + bonus_ref[b]) * g_ref[b]).astype(o_ref.dtype)


def _rw_state(intra, bonus, g, gn_w, gn_b, ones_bd, bsz, seq):
    gb = RW_BATCH_GROUP if bsz % RW_BATCH_GROUP == 0 else 1
    spec = pl.BlockSpec((gb, RW_CHUNK, W_MIX), lambda b, c: (b, c, 0))
    pc_spec = pl.BlockSpec((gb, SUBLANES, W_MIX), lambda b, c: (b, c, 0))
    vec = pl.BlockSpec((1, W_MIX), lambda b, c: (0, 0))
    out = pl.pallas_call(
        _rw_state_kernel,
        grid=(bsz // gb, seq // RW_CHUNK),
        in_specs=[spec] * 7 + [pc_spec, spec, spec, vec, vec,
                               pl.BlockSpec((W_MIX, W_MIX), lambda b, c: (0, 0))],
        out_specs=spec,
        out_shape=jax.ShapeDtypeStruct((bsz, seq, W_MIX), BF16),
        scratch_shapes=[pltpu.VMEM((gb, HEAD_DIM, W_MIX), F32)],
        compiler_params=_cparams("parallel", "arbitrary"),
        name="rw_state",
    )(*intra, bonus.reshape(bsz, seq, W_MIX), g.reshape(bsz, seq, W_MIX), gn_w, gn_b,
      ones_bd.astype(BF16))
    return out.reshape(bsz * seq, W_MIX)


def _store_token_rows(ref, val):
    tm, d = val.shape
    ch = d // LANES
    for s in range(ch):
        ref[pl.ds(s, tm, stride=ch), :] = val[:, s * LANES:(s + 1) * LANES]


def _load_token_rows(ref, tm, ch):
    return jnp.concatenate([ref[pl.ds(s, tm, stride=ch), :] for s in range(ch)], axis=1)


def _outproj_router_kernel(x_ref, sb_ref, gm_ref, rw_ref, fx_ref, wo_ref, g_ref, wr_ref, br_ref,
                           su_ref, x_out, h_out, idx_out, gate_out, rank_out, cnt_out, cnt_ref):
    mix = jnp.concatenate([sb_ref[...], gm_ref[...], rw_ref[...], fx_ref[...]], axis=1)
    x = x_ref[...] + _dot(mix, wo_ref[...])
    x_out[...] = x
    h = x * lax.rsqrt(jnp.mean(x * x, axis=-1, keepdims=True) + RMS_EPS) * g_ref[...]
    _store_token_rows(h_out, h)

    nr = wr_ref.shape[0]
    h_hi, h_lo = _split2(h)
    w_hi, w_lo = _split2(wr_ref[...])
    part = _dot_nt(jnp.concatenate([w_hi, w_lo], axis=0), h_hi)
    lg = part[:nr] + part[nr:] + _dot_nt(w_hi, h_lo) + br_ref[...]
    tm = lg.shape[1]
    gl = [lg[g:g + 1, :] for g in range(N_EXPERT_GROUPS)]
    gmax = gl[0]
    gsel = jnp.zeros((1, tm), jnp.int32)
    for g in range(1, N_EXPERT_GROUPS):
        better = gl[g] > gmax
        gsel = jnp.where(better, g, gsel)
        gmax = jnp.where(better, gl[g], gmax)
    denom = gl[0] * 0.0
    for g in range(N_EXPERT_GROUPS):
        denom = denom + jnp.exp(gl[g] - gmax)
    g_gate = 1.0 / denom

    e0 = SUBLANES
    ing = lg[e0:e0 + EXPERTS_PER_GROUP, :]
    for g in range(1, N_EXPERT_GROUPS):
        ing = jnp.where(gsel == g, lg[e0 + g * EXPERTS_PER_GROUP:e0 + (g + 1) * EXPERTS_PER_GROUP, :], ing)
    ridx = _iota2(ing.shape, 0)
    m1 = jnp.max(ing, axis=0, keepdims=True)
    i1 = jnp.min(jnp.where(ing == m1, ridx, EXPERTS_PER_GROUP), axis=0, keepdims=True)
    rest = jnp.where(ridx == i1, -jnp.inf, ing)
    m2 = jnp.max(rest, axis=0, keepdims=True)
    i2 = jnp.min(jnp.where(rest == m2, ridx, EXPERTS_PER_GROUP), axis=0, keepdims=True)
    e2 = jnp.exp(m2 - m1)
    s = 1.0 + e2
    e_sel = (gsel * EXPERTS_PER_GROUP + i1, gsel * EXPERTS_PER_GROUP + i2)
    idx_out[0:1, :] = e_sel[0]
    idx_out[1:2, :] = e_sel[1]
    gate_out[0:1, :] = (1.0 / s) * g_gate
    gate_out[1:2, :] = (e2 / s) * g_gate

    @pl.when(pl.program_id(0) == 0)
    def _():
        cnt_ref[...] = jnp.zeros_like(cnt_ref)

    erow = _iota2((N_EXPERTS, tm), 0)
    onehot = [(erow == e).astype(F32) for e in e_sel]
    both = onehot[0] + onehot[1]
    before = cnt_ref[...] + _dot(both.astype(BF16), su_ref[...])
    for k in range(TOP_K):
        rank_out[k:k + 1, :] = jnp.sum(onehot[k] * before, axis=0, keepdims=True).astype(jnp.int32)
    cnt_ref[...] = cnt_ref[...] + jnp.sum(both, axis=1, keepdims=True)
    cnt_out[...] = jnp.broadcast_to(cnt_ref[...], cnt_out.shape).astype(jnp.int32)


def _outproj_router(x2, ys, w_out, g2, w_router_t, b_router):
    n, d = x2.shape
    tm = min(ROW_TILE, n)
    nr = w_router_t.shape[0]
    ymix = pl.BlockSpec((tm, W_MIX), lambda i: (i, 0))
    r = jnp.arange(tm)
    strict_upper = (r[:, None] < r[None, :]).astype(BF16)
    return pl.pallas_call(
        _outproj_router_kernel,
        grid=(n // tm,),
        in_specs=[
            pl.BlockSpec((tm, d), lambda i: (i, 0)),
            ymix, ymix, ymix, ymix,
            pl.BlockSpec((4 * W_MIX, d), lambda i: (0, 0)),
            pl.BlockSpec((1, d), lambda i: (0, 0)),
            pl.BlockSpec((nr, d), lambda i: (0, 0)),
            pl.BlockSpec((nr, 1), lambda i: (0, 0)),
            pl.BlockSpec((tm, tm), lambda i: (0, 0)),
        ],
        out_specs=[
            pl.BlockSpec((tm, d), lambda i: (i, 0)),
            pl.BlockSpec((tm * (d // LANES), LANES), lambda i: (i, 0)),
            pl.BlockSpec((TOP_K, tm), lambda i: (0, i)),
            pl.BlockSpec((TOP_K, tm), lambda i: (0, i)),
            pl.BlockSpec((TOP_K, tm), lambda i: (0, i)),
            pl.BlockSpec((N_EXPERTS, LANES), lambda i: (0, 0)),
        ],
        out_shape=[
            jax.ShapeDtypeStruct((n, d), F32),
            jax.ShapeDtypeStruct((n * (d // LANES), LANES), F32),
            jax.ShapeDtypeStruct((TOP_K, n), jnp.int32),
            jax.ShapeDtypeStruct((TOP_K, n), F32),
            jax.ShapeDtypeStruct((TOP_K, n), jnp.int32),
            jax.ShapeDtypeStruct((N_EXPERTS, LANES), jnp.int32),
        ],
        scratch_shapes=[pltpu.VMEM((N_EXPERTS, 1), F32)],
        compiler_params=_cparams("arbitrary"),
        name="outproj_router",
    )(x2, *ys, w_out, g2, w_router_t, b_router, strict_upper)


def _token_copy(src_hbm, dst, src_tok, dst_tok, sem, ch):
    return pltpu.make_async_copy(
        src_hbm.at[pl.ds(pl.multiple_of(src_tok * ch, ch), ch)],
        dst.at[pl.ds(pl.multiple_of(dst_tok * ch, ch), ch)], sem)


def _dispatch_kernel(pad_end_ref, slot_ref, h_ref, o_hbm, zbuf, zsem, sem, *, ch, tm):
    i = pl.program_id(0)

    def zero_copy(first_slot):
        return pltpu.make_async_copy(
            zbuf, o_hbm.at[pl.ds(pl.multiple_of(first_slot * ch, ch), MOE_TB * ch)], zsem)

    @pl.when(i == 0)
    def _():
        zbuf[...] = jnp.zeros_like(zbuf)
        tails = [jnp.maximum(pad_end_ref[e] - MOE_TB, 0) for e in range(N_EXPERTS)]
        for first in tails:
            zero_copy(first).start()
        for first in tails:
            zero_copy(first).wait()

        def unused_block(b, _):
            zero_copy(b * MOE_TB).start()
            zero_copy(b * MOE_TB).wait()
            return 0

        lax.fori_loop(pad_end_ref[N_EXPERTS - 1] // MOE_TB, o_hbm.shape[0] // (MOE_TB * ch),
                      unused_block, 0)

    def start(r, _):
        for k in range(TOP_K):
            _token_copy(h_ref, o_hbm, r, slot_ref[0, 0, r * TOP_K + k], sem, ch).start(
                priority=k % 2)
        return 0

    lax.fori_loop(0, tm, start, 0, unroll=DMA_ISSUE_UNROLL)
    for k in range(TOP_K):
        pltpu.make_async_copy(h_ref, o_hbm.at[pl.ds(0, tm * ch)], sem).wait()


def _dispatch(pad_end, slot_flat, h_rows, n_slots, ch):
    n_tok = h_rows.shape[0] // ch
    tm = min(256, n_tok)
    n_steps = n_tok // tm
    return pl.pallas_call(
        functools.partial(_dispatch_kernel, ch=ch, tm=tm),
        grid_spec=pltpu.PrefetchScalarGridSpec(
            num_scalar_prefetch=1,
            grid=(n_steps,),
            in_specs=[
                pl.BlockSpec((1, 1, tm * TOP_K), lambda i, pe: (i, 0, 0), memory_space=pltpu.SMEM),
                pl.BlockSpec((tm * ch, LANES), lambda i, pe: (i, 0)),
            ],
            out_specs=pl.BlockSpec(memory_space=pl.ANY),
            scratch_shapes=[pltpu.VMEM((MOE_TB * ch, LANES), F32), pltpu.SemaphoreType.DMA,
                            pltpu.SemaphoreType.DMA],
        ),
        out_shape=jax.ShapeDtypeStruct((n_slots * ch, LANES), h_rows.dtype),
        compiler_params=_cparams("arbitrary"),
        name="moe_dispatch",
    )(pad_end, slot_flat.reshape(n_steps, 1, tm * TOP_K), h_rows)


def _expert_kernel(be_ref, nb_ref, x_ref, wg_ref, wu_ref, wd_ref, o_ref, wg_c, wu_c, wd_c, *, ch):
    i = pl.program_id(0)

    @pl.when((i < nb_ref[0]) & ((i == 0) | (be_ref[i] != be_ref[jnp.maximum(i - 1, 0)])))
    def _():
        wg_c[...] = wg_ref[0, 0].astype(BF16)
        wu_c[...] = wu_ref[0, 0].astype(BF16)
        wd_c[...] = wd_ref[0, 0].astype(BF16)

    @pl.when(i < nb_ref[0])
    def _():
        x = _load_token_rows(x_ref, MOE_TB, ch).astype(BF16)
        a = _dot(x, wg_c[...])
        u = _dot(x, wu_c[...])
        hid = (a * jax.nn.sigmoid(a) * u).astype(BF16)
        _store_token_rows(o_ref, _dot(hid, wd_c[...]))

    @pl.when(i >= nb_ref[0])
    def _():
        o_ref[...] = jnp.zeros_like(o_ref)


def _expert_blocks(block_expert, n_used, xb_rows, w_gate, w_up, w_down, layer):
    d, hid = w_gate.shape[2], w_gate.shape[3]
    ch = d // LANES
    n_slots = xb_rows.shape[0] // ch
    return pl.pallas_call(
        functools.partial(_expert_kernel, ch=ch),
        grid_spec=pltpu.PrefetchScalarGridSpec(
            num_scalar_prefetch=2,
            grid=(n_slots // MOE_TB,),
            in_specs=[
                pl.BlockSpec((MOE_TB * ch, LANES), lambda i, be, nb: (i, 0)),
                pl.BlockSpec((1, 1, d, hid), lambda i, be, nb: (layer, be[i], 0, 0)),
                pl.BlockSpec((1, 1, d, hid), lambda i, be, nb: (layer, be[i], 0, 0)),
                pl.BlockSpec((1, 1, hid, d), lambda i, be, nb: (layer, be[i], 0, 0)),
            ],
            out_specs=pl.BlockSpec((MOE_TB * ch, LANES), lambda i, be, nb: (i, 0)),
            scratch_shapes=[pltpu.VMEM((d, hid), BF16), pltpu.VMEM((d, hid), BF16),
                            pltpu.VMEM((hid, d), BF16)],
        ),
        out_shape=jax.ShapeDtypeStruct((n_slots * ch, LANES), F32),
        compiler_params=_cparams("arbitrary"),
        name="moe_experts",
    )(block_expert, n_used, xb_rows, w_gate, w_up, w_down)


def _combine_kernel(slot_ref, slot_next_ref, x_ref, gate_ref, norm_ref, yb_hbm, o_ref, buf, sem,
                    *, ch, final_norm):
    i = pl.program_id(0)
    last = pl.num_programs(0) - 1
    tm = x_ref.shape[0]

    def issue(idx_ref, parity):
        def start(r, _):
            for k in range(TOP_K):
                _token_copy(yb_hbm, buf.at[parity, k], idx_ref[0, 0, r * TOP_K + k], r,
                            sem.at[parity], ch).start(priority=k % 2)
            return 0

        lax.fori_loop(0, tm, start, 0, unroll=DMA_ISSUE_UNROLL)

    @pl.when(i == 0)
    def _():
        issue(slot_ref, 0)

    @pl.when(i < last)
    def _():
        issue(slot_next_ref, (i + 1) % 2)

    par = i % 2
    for k in range(TOP_K):
        pltpu.make_async_copy(yb_hbm.at[pl.ds(0, tm * ch)], buf.at[par, k], sem.at[par]).wait()
    y = (_load_token_rows(buf.at[par, 0], tm, ch) * gate_ref[:, 0:1]
         + _load_token_rows(buf.at[par, 1], tm, ch) * gate_ref[:, 1:2])
    x = x_ref[...] + y
    if final_norm:
        x = x * lax.rsqrt(jnp.mean(x * x, axis=-1, keepdims=True) + RMS_EPS) * norm_ref[...]
    o_ref[...] = x


def _combine(slot_flat, x2, gates, yb_rows, norm_g, final_norm):
    n, d = x2.shape
    ch = d // LANES
    tm = min(256, n)
    n_steps = n // tm
    slots3 = slot_flat.reshape(n_steps, 1, tm * TOP_K)
    return pl.pallas_call(
        functools.partial(_combine_kernel, ch=ch, final_norm=final_norm),
        grid=(n_steps,),
        in_specs=[
            pl.BlockSpec((1, 1, tm * TOP_K), lambda i: (i, 0, 0), memory_space=pltpu.SMEM),
            pl.BlockSpec((1, 1, tm * TOP_K), lambda i: (jnp.minimum(i + 1, n_steps - 1), 0, 0),
                         memory_space=pltpu.SMEM),
            pl.BlockSpec((tm, d), lambda i: (i, 0)),
            pl.BlockSpec((tm, TOP_K), lambda i: (i, 0)),
            pl.BlockSpec((1, d), lambda i: (0, 0)),
            pl.BlockSpec(memory_space=pl.ANY),
        ],
        out_specs=pl.BlockSpec((tm, d), lambda i: (i, 0)),
        scratch_shapes=[pltpu.VMEM((2, TOP_K, tm * ch, LANES), F32),
                        pltpu.SemaphoreType.DMA((2,))],
        out_shape=jax.ShapeDtypeStruct((n, d), F32),
        compiler_params=_cparams("arbitrary"),
        name="moe_combine",
    )(slots3, slots3, x2, gates, norm_g, yb_rows)


def _routing_plan(expert_idx_t, rank_t, counts, n_tok):
    padded = (counts + MOE_TB - 1) // MOE_TB * MOE_TB
    pad_end = jnp.cumsum(padded)
    pad_start = pad_end - padded
    start_t = jnp.sum(jnp.where(expert_idx_t[:, :, None] == jnp.arange(N_EXPERTS), pad_start, 0),
                      axis=-1)
    slot = jnp.transpose(start_t + rank_t).reshape(n_tok * TOP_K)
    n_blocks = -(-(n_tok * TOP_K) // MOE_TB) + N_EXPERTS
    block_expert = jnp.minimum(
        jnp.sum(pad_end[None, :] <= (jnp.arange(n_blocks) * MOE_TB)[:, None], axis=1),
        N_EXPERTS - 1).astype(jnp.int32)
    n_used = (pad_end[-1] // MOE_TB).astype(jnp.int32).reshape(1)
    return slot.astype(jnp.int32), pad_end.astype(jnp.int32), block_expert, n_used, n_blocks * MOE_TB


def _moe(x2, h2, expert_idx_t, gates_t, rank_t, counts, w_gate, w_up, w_down, layer, norm_g,
         final_norm):
    n_tok = x2.shape[0]
    slot, pad_end, block_expert, n_used, n_slots = _routing_plan(expert_idx_t, rank_t, counts, n_tok)
    xb = _dispatch(pad_end, slot, h2, n_slots, x2.shape[1] // LANES)
    yb = _expert_blocks(block_expert, n_used, xb, w_gate, w_up, w_down, layer)
    return _combine(slot, x2, jnp.transpose(gates_t), yb, norm_g, final_norm)


def _prepare_params(w_in, gm_w_s, gm_b, rw_mu, rw_w0, rw_w2, rw_a0, rw_a2, rw_g2, rw_k_k, rw_k_a,
                    rw_r_k, rw_gn_w, rw_gn_b, fx_b_f, w_out, router_group_w, router_group_b,
                    router_expert_w, router_expert_b):
    depth, d = w_in.shape[0], w_in.shape[1]
    n_main = 3 * W_MIX + 2 * W_MIX + RW_COLS + 3 * W_MIX
    t = lambda a: jnp.swapaxes(a, 1, 2)
    pad_rows = lambda a, rows: jnp.concatenate(
        [a, jnp.zeros((depth, rows - a.shape[1]) + a.shape[2:], a.dtype)], axis=1)
    row = lambda a: a.reshape(depth, 1, -1)
    wr = jnp.concatenate([pad_rows(t(router_group_w), SUBLANES), t(router_expert_w)], axis=1)
    br = jnp.concatenate([pad_rows(router_group_b[:, :, None], SUBLANES),
                          router_expert_b[:, :, None]], axis=1)
    return dict(
        w_main=w_in[:, :, :n_main].astype(BF16),
        w_gate_t=pad_rows(t(w_in[:, :, n_main:n_main + N_HEADS]), SUBLANES).astype(BF16),
        fx_bias=pad_rows(fx_b_f[:, :, None], SUBLANES),
        gm_w=jnp.transpose(gm_w_s, (0, 2, 1, 3)).reshape(depth, GM_CHUNK, N_HEADS * GM_CHUNK),
        gm_b=jnp.repeat(t(gm_b), HEAD_DIM, axis=2),
        mu=row(rw_mu), w0=row(rw_w0), w2=rw_w2.astype(BF16), a0=row(rw_a0),
        a2=rw_a2.astype(BF16), g2=rw_g2.astype(BF16), k_k=row(rw_k_k), k_a=row(rw_k_a),
        r_k=row(rw_r_k), gn_w=row(rw_gn_w), gn_b=row(rw_gn_b),
        w_out=w_out.astype(BF16), wr=wr, br=br,
    )


def _layer_params(l, w_in, gm_w_s, gm_b, rw_mu, rw_w0, rw_w2, rw_a0, rw_a2, rw_g2, rw_k_k, rw_k_a,
                  rw_r_k, rw_gn_w, rw_gn_b, fx_b_f, w_out, router_group_w, router_group_b,
                  router_expert_w, router_expert_b):
    allp = _prepare_params(w_in, gm_w_s, gm_b, rw_mu, rw_w0, rw_w2, rw_a0, rw_a2, rw_g2, rw_k_k,
                           rw_k_a, rw_r_k, rw_gn_w, rw_gn_b, fx_b_f, w_out, router_group_w,
                           router_group_b, router_expert_w, router_expert_b)
    return {k: v[l] for k, v in allp.items()}


def kernel(x, norm1_g, w_in, gm_w_s, gm_b, rw_mu, rw_w0, rw_w2, rw_a0, rw_a2, rw_g2, rw_k_k, rw_k_a, rw_r_k, rw_gn_w, rw_gn_b, fx_b_f, w_out, norm2_g, router_group_w, router_group_b, router_expert_w, router_expert_b, exp_w_gate, exp_w_up, exp_w_down, final_norm_g):
    bsz, seq, d = x.shape
    depth = w_in.shape[0]
    assert seq % min(ATT_TQB, seq) == 0 and seq % RW_CHUNK == 0 and seq % (2 * LANES) == 0
    x2 = x.reshape(bsz * seq, d)
    hd = jnp.arange(W_MIX) // HEAD_DIM
    ones_bd = (hd[:, None] == hd[None, :]).astype(BF16)
    allp = _prepare_params(w_in, gm_w_s, gm_b, rw_mu, rw_w0, rw_w2, rw_a0, rw_a2, rw_g2, rw_k_k,
                           rw_k_a, rw_r_k, rw_gn_w, rw_gn_b, fx_b_f, w_out, router_group_w,
                           router_group_b, router_expert_w, router_expert_b)
    for l in range(depth):
        lp = {k: v[l] for k, v in allp.items()}
        p_sb, p_gm, p_rw, p_fx, grow = _norm_inproj(x2, norm1_g[l].reshape(1, d), lp["w_main"],
                                                    lp["w_gate_t"])
        y_sb = _sb_attention(p_sb, bsz, seq)
        y_gm = _gm_mix(p_gm, lp["gm_w"], lp["gm_b"], ones_bd)
        intra = _rw_intra(p_rw, bsz, seq, lp["mu"], lp["w0"], lp["w2"], lp["a0"], lp["a2"],
                          lp["g2"], lp["k_k"], lp["k_a"], lp["r_k"], ones_bd)
        y_rw = _rw_state(intra[:8], intra[8], intra[9], lp["gn_w"], lp["gn_b"], ones_bd, bsz, seq)
        grow3 = jnp.transpose(grow.reshape(SUBLANES, bsz, seq), (1, 0, 2))
        cum_row = _fx_cum(grow3, lp["fx_bias"])
        y_fx = _fx_attention(p_fx, cum_row, bsz, seq)
        x2, h2, eidx_t, gates_t, rank_t, cnt = _outproj_router(
            x2, (y_sb, y_gm, y_rw, y_fx), lp["w_out"], norm2_g[l].reshape(1, d), lp["wr"], lp["br"])
        x2 = _moe(x2, h2, eidx_t, gates_t, rank_t, cnt[:, 0], exp_w_gate, exp_w_up, exp_w_down, l,
                  final_norm_g.reshape(1, d), l == depth - 1)
    return x2.reshape(bsz, seq, d)
```

```python
import functools

import jax
import jax.numpy as jnp
from jax import lax
from jax.experimental import pallas as pl
from jax.experimental.pallas import tpu as pltpu

F32 = jnp.float32
BF16 = jnp.bfloat16
HIGHEST = lax.Precision.HIGHEST

HEAD_DIM = 64
N_HEADS = 4
W_MIX = N_HEADS * HEAD_DIM
GM_CHUNK = 128
GM_ROWS = 1024
RW_DECAY_LORA = 64
RW_AAA_LORA = 64
RW_GATE_LORA = 128
RW_COLS = 3 * W_MIX + RW_DECAY_LORA + RW_AAA_LORA + RW_GATE_LORA
N_EXPERT_GROUPS = 4
EXPERTS_PER_GROUP = 8
N_EXPERTS = N_EXPERT_GROUPS * EXPERTS_PER_GROUP
TOP_K = 2
RMS_EPS = 1e-6
LN_EPS = 1e-5
GN_EPS = 64e-5
L2_EPS = 1e-12
LOG2_E = 1.4426950408889634

V7X_VMEM_LIMIT_BYTES = 56 * 1024 * 1024
LANES = 128
SUBLANES = 8

ATT_TK = 128
FX_TK = 256
SB_TILES_PER_STEP = 2
ATT_TQB = 512
FX_TQB = 1024
ATT_W = N_HEADS * LANES
RW_CHUNK = 64
RW_CPS = 8
RW_BATCH_GROUP = 8
ROW_TILE = 512
MOE_TB = 512
DMA_ISSUE_UNROLL = 8
DISPATCH_TOKENS = 1024


def _cparams(*sem):
    return pltpu.CompilerParams(dimension_semantics=sem, vmem_limit_bytes=V7X_VMEM_LIMIT_BYTES)


def _dot(a, b, precision=None):
    return jnp.dot(a, b, preferred_element_type=F32, precision=precision)


def _dot_nt(a, b, precision=None):
    return lax.dot_general(a, b, (((1,), (1,)), ((), ())), preferred_element_type=F32,
                           precision=precision)


def _dot_tn(a, b, precision=None):
    return lax.dot_general(a, b, (((0,), (0,)), ((), ())), preferred_element_type=F32,
                           precision=precision)


def _iota2(shape, dim):
    return lax.broadcasted_iota(jnp.int32, shape, dim)


def _store_heads_padded(ref, col0, src, fill):
    low = _iota2((src.shape[0], LANES), 1) < HEAD_DIM
    for pair in range(N_HEADS // 2):
        slab = src[:, pair * LANES:(pair + 1) * LANES]
        swapped = pltpu.roll(slab, HEAD_DIM, axis=1)
        for odd, val in enumerate((slab, swapped)):
            c = col0 + (2 * pair + odd) * LANES
            ref[:, c:c + LANES] = jnp.where(low, val, fill).astype(ref.dtype)


def _norm_inproj_kernel(x_ref, g_ref, w_ref, wgt_ref, sb_ref, gm_ref, rw_ref, fx_ref, grow_ref):
    x = x_ref[...]
    h = x * lax.rsqrt(jnp.mean(x * x, axis=-1, keepdims=True) + RMS_EPS) * g_ref[...]
    hb = h.astype(BF16)
    p = _dot(hb, w_ref[...])
    scale = HEAD_DIM ** -0.5
    wp = N_HEADS * LANES
    o = 0
    _store_heads_padded(sb_ref, 0, p[:, o:o + W_MIX] * scale, 0.0)
    _store_heads_padded(sb_ref, wp, p[:, o + W_MIX:o + 2 * W_MIX], 0.0)
    _store_heads_padded(sb_ref, 2 * wp, p[:, o + 2 * W_MIX:o + 3 * W_MIX], 0.0)
    o += 3 * W_MIX
    gm_ref[...] = p[:, o:o + 2 * W_MIX]
    o += 2 * W_MIX
    rw_ref[...] = p[:, o:o + RW_COLS]
    o += RW_COLS
    _store_heads_padded(fx_ref, 0, p[:, o:o + W_MIX] * scale, 0.0)
    _store_heads_padded(fx_ref, wp, p[:, o + W_MIX:o + 2 * W_MIX], 0.0)
    _store_heads_padded(fx_ref, 2 * wp, p[:, o + 2 * W_MIX:o + 3 * W_MIX], 1.0)
    grow_ref[...] = _dot_nt(wgt_ref[...], hb)


def _norm_inproj(x2, g, w_main, w_gate_t):
    n, d = x2.shape
    tm = min(ROW_TILE, n)
    cols = w_main.shape[1]
    return pl.pallas_call(
        _norm_inproj_kernel,
        grid=(n // tm,),
        in_specs=[
            pl.BlockSpec((tm, d), lambda i: (i, 0)),
            pl.BlockSpec((1, d), lambda i: (0, 0)),
            pl.BlockSpec((d, cols), lambda i: (0, 0)),
            pl.BlockSpec((SUBLANES, d), lambda i: (0, 0)),
        ],
        out_specs=[
            pl.BlockSpec((tm, 3 * ATT_W), lambda i: (i, 0)),
            pl.BlockSpec((tm, 2 * W_MIX), lambda i: (i, 0)),
            pl.BlockSpec((tm, RW_COLS), lambda i: (i, 0)),
            pl.BlockSpec((tm, 3 * ATT_W), lambda i: (i, 0)),
            pl.BlockSpec((SUBLANES, tm), lambda i: (0, i)),
        ],
        out_shape=[
            jax.ShapeDtypeStruct((n, 3 * ATT_W), BF16),
            jax.ShapeDtypeStruct((n, 2 * W_MIX), F32),
            jax.ShapeDtypeStruct((n, RW_COLS), F32),
            jax.ShapeDtypeStruct((n, 3 * ATT_W), BF16),
            jax.ShapeDtypeStruct((SUBLANES, n), F32),
        ],
        compiler_params=_cparams("parallel"),
        name="norm_inproj",
    )(x2, g, w_main, w_gate_t)


def _sb_kernel(q_ref, k_ref, v_ref, tri_ref, o_ref, c_ref, acc_ref):
    qi = pl.program_id(1)
    tqb = q_ref.shape[1]
    hl = lambda h: slice(h * LANES, (h + 1) * LANES)

    def key_tiles(tiles):
        meta, zs, rrs = [], [], []
        for j, off in tiles:
            r0 = 0 if off is None else off * ATT_TK
            rows = tqb - r0
            strict = (None if off is None
                      else _iota2((rows, ATT_TK), 1) < _iota2((rows, ATT_TK), 0))
            meta.append((pl.multiple_of(j * ATT_TK, ATT_TK), r0, rows, strict))
        for start, r0, rows, strict in meta:
            zs.append([_dot_nt(q_ref[0, r0:, hl(h)], k_ref[0, pl.ds(start, ATT_TK), hl(h)])
                       for h in range(N_HEADS)])
        for (start, r0, rows, strict), z4 in zip(meta, zs):
            parts = []
            for z in z4:
                sp = jnp.maximum(z, 0.0) + jnp.log(1.0 + jnp.exp2(jnp.abs(z) * -LOG2_E))
                if strict is not None:
                    sp = jnp.where(strict, sp, 0.0)
                hi = sp.astype(BF16)
                lo = (sp - hi.astype(F32)).astype(BF16)
                parts.append(jnp.concatenate([hi, lo], axis=1))
            rrs.append(_dot(jnp.concatenate(parts, axis=0), tri_ref[...]))
        for (start, r0, rows, strict), z4, rr in zip(meta, zs, rrs):
            ps = []
            for h in range(N_HEADS):
                rh = rr[h * rows:(h + 1) * rows]
                p = jnp.exp(z4[h] - rh[:, :ATT_TK] - c_ref[h, r0:, :])
                if strict is not None:
                    p = jnp.where(strict, p, 0.0)
                ps.append(p.astype(BF16))
                c_ref[h, r0:, :] += rh[:, ATT_TK:]
            pv = [_dot(ps[h], v_ref[0, pl.ds(start, ATT_TK), hl(h)]) for h in range(N_HEADS)]
            for h in range(N_HEADS):
                acc_ref[h, r0:, :] += pv[h]

    c_ref[...] = jnp.zeros_like(c_ref)
    acc_ref[...] = jnp.zeros_like(acc_ref)
    n_diag = tqb // ATT_TK
    group = SB_TILES_PER_STEP if n_diag % SB_TILES_PER_STEP == 0 else 1
    for first in reversed(range(0, n_diag, group)):
        key_tiles([(qi * n_diag + off, off) for off in reversed(range(first, first + group))])

    def body(jj, carry):
        right = qi * n_diag - 1 - jj * group
        key_tiles([(right - g, None) for g in range(group)])
        return carry

    lax.fori_loop(0, qi * n_diag // group, body, 0)
    for h in range(N_HEADS):
        o_ref[0, :, h * HEAD_DIM:(h + 1) * HEAD_DIM] = acc_ref[h, :, :HEAD_DIM].astype(o_ref.dtype)


def _sb_attention(qkv, bsz, seq):
    qkv3 = qkv.reshape(bsz, seq, 3 * ATT_W)
    r = jnp.arange(ATT_TK)
    tri = (r[:, None] >= r[None, :]).astype(BF16)
    blk = jnp.concatenate([tri, jnp.ones((ATT_TK, ATT_TK), BF16)], axis=1)
    tri2 = jnp.concatenate([blk, blk], axis=0)
    tqb = min(ATT_TQB, seq)
    out = pl.pallas_call(
        _sb_kernel,
        grid=(bsz, seq // tqb),
        in_specs=[
            pl.BlockSpec((1, tqb, ATT_W), lambda b, i: (b, i, 0)),
            pl.BlockSpec((1, seq, ATT_W), lambda b, i: (b, 0, 1)),
            pl.BlockSpec((1, seq, ATT_W), lambda b, i: (b, 0, 2)),
            pl.BlockSpec((2 * ATT_TK, 2 * ATT_TK), lambda b, i: (0, 0)),
        ],
        out_specs=pl.BlockSpec((1, tqb, W_MIX), lambda b, i: (b, i, 0)),
        out_shape=jax.ShapeDtypeStruct((bsz, seq, W_MIX), BF16),
        scratch_shapes=[pltpu.VMEM((N_HEADS, tqb, ATT_TK), F32),
                        pltpu.VMEM((N_HEADS, tqb, LANES), F32)],
        compiler_params=_cparams("parallel", "parallel"),
        name="sb_attention",
    )(qkv3, qkv3, qkv3, tri2)
    return out.reshape(bsz * seq, W_MIX)


def _fx_cum_kernel(g_ref, b_ref, cum_ref):
    seq = g_ref.shape[2]
    blk = 2 * LANES
    r = _iota2((blk, blk), 0)
    c = _iota2((blk, blk), 1)
    tri = (r <= c).astype(F32)
    carry = jnp.zeros((SUBLANES, 1), F32)
    for s in range(seq // blk):
        x = g_ref[0, :, s * blk:(s + 1) * blk] + b_ref[...]
        lf = -(jnp.maximum(-x, 0.0) + jnp.log(1.0 + jnp.exp(-jnp.abs(x))))
        cs = _dot(lf, tri, precision=HIGHEST) + carry
        cum_ref[0, :, s * blk:(s + 1) * blk] = cs
        carry = cs[:, blk - 1:blk]


def _fx_cum(grow3, bias_col):
    bsz, _, seq = grow3.shape
    return pl.pallas_call(
        _fx_cum_kernel,
        grid=(bsz,),
        in_specs=[
            pl.BlockSpec((1, SUBLANES, seq), lambda b: (b, 0, 0)),
            pl.BlockSpec((SUBLANES, 1), lambda b: (0, 0)),
        ],
        out_specs=pl.BlockSpec((1, SUBLANES, seq), lambda b: (b, 0, 0)),
        out_shape=jax.ShapeDtypeStruct((bsz, SUBLANES, seq), F32),
        compiler_params=_cparams("parallel"),
        name="fx_cum",
    )(grow3, bias_col)


def _fx_kernel(q_ref, k_ref, v_ref, crow_ref, ccol_ref, o_ref, m_ref, acc_ref, ct_ref):
    qi = pl.program_id(1)
    tqb = q_ref.shape[1]
    hl = lambda h: slice(h * LANES, (h + 1) * LANES)

    def key_tile(j, off):
        r0 = 0 if off is None else off * FX_TK
        rows = tqb - r0
        start = pl.multiple_of(j * FX_TK, FX_TK)
        halves = range(FX_TK // LANES)
        if off is not None:
            causal = [_iota2((rows, LANES), 1) + s * LANES <= _iota2((rows, LANES), 0)
                      for s in halves]
        heads = range(N_HEADS)
        z = [_dot_nt(q_ref[0, r0:, hl(h)], k_ref[0, pl.ds(start, FX_TK), hl(h)]) for h in heads]
        ps, alphas = [], []
        for h in heads:
            ct = ct_ref[h, r0:, :]
            xs = []
            for s in halves:
                cs = crow_ref[0, h:h + 1, pl.ds(pl.multiple_of(start + s * LANES, LANES), LANES)]
                x = z[h][:, s * LANES:(s + 1) * LANES] + ct - cs
                if off is not None:
                    x = jnp.where(causal[s], x, -jnp.inf)
                xs.append(x)
            m_old = m_ref[h, r0:, :]
            m_new = jnp.maximum(m_old, jnp.max(functools.reduce(jnp.maximum, xs), axis=-1,
                                               keepdims=True))
            m_ref[h, r0:, :] = m_new
            ps.append(jnp.concatenate([jnp.exp(x - m_new).astype(BF16) for x in xs], axis=1))
            alphas.append(jnp.exp(m_old - m_new))
        pv = [_dot(ps[h], v_ref[0, pl.ds(start, FX_TK), hl(h)]) for h in heads]
        for h in heads:
            acc_ref[h, r0:, :] = alphas[h] * acc_ref[h, r0:, :] + pv[h]

    m_ref[...] = jnp.full_like(m_ref, -jnp.inf)
    acc_ref[...] = jnp.zeros_like(acc_ref)
    for h in range(N_HEADS):
        ct_ref[h] = jnp.broadcast_to(ccol_ref[0, :, h:h + 1], (tqb, LANES))
    n_diag = tqb // FX_TK

    def body(j, carry):
        key_tile(j, None)
        return carry

    lax.fori_loop(0, qi * n_diag, body, 0)
    for off in range(n_diag):
        key_tile(qi * n_diag + off, off)
    for h in range(N_HEADS):
        acc = acc_ref[h]
        den = pltpu.roll(acc, HEAD_DIM, axis=1)
        o_ref[0, :, h * HEAD_DIM:(h + 1) * HEAD_DIM] = (acc / den)[:, :HEAD_DIM].astype(o_ref.dtype)


def _fx_attention(qkv, cum_row, bsz, seq):
    qkv3 = qkv.reshape(bsz, seq, 3 * ATT_W)
    cum_col = jnp.transpose(cum_row, (0, 2, 1))
    tqb = min(FX_TQB, seq)
    out = pl.pallas_call(
        _fx_kernel,
        grid=(bsz, seq // tqb),
        in_specs=[
            pl.BlockSpec((1, tqb, ATT_W), lambda b, i: (b, i, 0)),
            pl.BlockSpec((1, seq, ATT_W), lambda b, i: (b, 0, 1)),
            pl.BlockSpec((1, seq, ATT_W), lambda b, i: (b, 0, 2)),
            pl.BlockSpec((1, SUBLANES, seq), lambda b, i: (b, 0, 0)),
            pl.BlockSpec((1, tqb, SUBLANES), lambda b, i: (b, i, 0)),
        ],
        out_specs=pl.BlockSpec((1, tqb, W_MIX), lambda b, i: (b, i, 0)),
        out_shape=jax.ShapeDtypeStruct((bsz, seq, W_MIX), BF16),
        scratch_shapes=[pltpu.VMEM((N_HEADS, tqb, LANES), F32)] * 3,
        compiler_params=_cparams("parallel", "parallel"),
        name="fx_attention",
    )(qkv3, qkv3, qkv3, cum_row, cum_col)
    return out.reshape(bsz * seq, W_MIX)


def _gm_kernel(p_ref, w_ref, b_ref, ones_ref, o_ref):
    hid = jax.nn.gelu(p_ref[...])
    u = hid[:, :W_MIX]
    v = hid[:, W_MIX:]
    inv_n = 1.0 / HEAD_DIM
    mean = _mm_lhs2(v, ones_ref[...], _NN) * inv_n
    vc = v - mean
    var = _mm_lhs2(vc * vc, ones_ref[...], _NN) * inv_n
    vn = (vc * lax.rsqrt(var + LN_EPS)).astype(BF16)
    wshape = (GM_CHUNK, N_HEADS * GM_CHUNK)
    lower = jnp.bitwise_and(_iota2(wshape, 1), GM_CHUNK - 1) <= _iota2(wshape, 0)
    w = jnp.where(lower, w_ref[...], 0.0).astype(BF16)
    chunks = range(p_ref.shape[0] // GM_CHUNK)
    rows = [slice(c * GM_CHUNK, (c + 1) * GM_CHUNK) for c in chunks]
    mixed = [_dot(w, _head_blocks(vn[rows[c]], HEAD_DIM)) for c in chunks]
    for c in chunks:
        o_ref[rows[c], :] = (u[rows[c]] * (mixed[c] + b_ref[...])).astype(o_ref.dtype)


def _gm_mix(p_gm, w_cat, b_full, ones_bd):
    n = p_gm.shape[0]
    tm = min(GM_ROWS, n)
    return pl.pallas_call(
        _gm_kernel,
        grid=(n // tm,),
        in_specs=[
            pl.BlockSpec((tm, 2 * W_MIX), lambda i: (i, 0)),
            pl.BlockSpec((GM_CHUNK, N_HEADS * GM_CHUNK), lambda i: (0, 0)),
            pl.BlockSpec((GM_CHUNK, W_MIX), lambda i: (0, 0)),
            pl.BlockSpec((W_MIX, W_MIX), lambda i: (0, 0)),
        ],
        out_specs=pl.BlockSpec((tm, W_MIX), lambda i: (i, 0)),
        out_shape=jax.ShapeDtypeStruct((n, W_MIX), BF16),
        compiler_params=_cparams("parallel"),
        name="gm_mix",
    )(p_gm, w_cat, b_full, ones_bd)


def _head_sum(x, ones_bd):
    return _mm_lhs2(x, ones_bd, _NN)


def _rw_prep_math(p, prev_row, mu_ref, w0_ref, w2_ref, a0_ref, a2_ref, g2_ref, kk_ref, ka_ref,
                  rk_ref, bd_ref):
    rows = _iota2(p.shape, 0)
    prev = jnp.where(rows == 0, prev_row, pltpu.roll(p, 1, axis=0))
    p = p + (prev - p) * mu_ref[...]
    r = p[:, 0:W_MIX]
    k = p[:, W_MIX:2 * W_MIX]
    v = p[:, 2 * W_MIX:3 * W_MIX]
    o = 3 * W_MIX
    xw = p[:, o:o + RW_DECAY_LORA]
    o += RW_DECAY_LORA
    xa = p[:, o:o + RW_AAA_LORA]
    o += RW_AAA_LORA
    xg = p[:, o:o + RW_GATE_LORA]
    wpre = -(w0_ref[...] + _dot(jnp.tanh(xw).astype(BF16), w2_ref[...]))
    w = -(jnp.maximum(wpre, 0.0) + jnp.log(1.0 + jnp.exp(-jnp.abs(wpre)))) - 0.5
    a = jax.nn.sigmoid(a0_ref[...] + _dot(xa.astype(BF16), a2_ref[...]))
    g = _dot(jax.nn.sigmoid(xg).astype(BF16), g2_ref[...])
    kk = k * kk_ref[...]
    nrm = jnp.maximum(jnp.sqrt(_head_sum(kk * kk, bd_ref[...])), L2_EPS)
    kk = kk / nrm
    k2 = k * (1.0 + (a - 1.0) * ka_ref[...])
    ld = -jnp.exp(w)
    bonus = _head_sum(r * k2 * rk_ref[...], bd_ref[...]) * v
    return r, ld, k2, v, kk, kk * a, bonus, g


_NN = (((1,), (0,)), ((), ()))
_NT = (((1,), (1,)), ((), ()))


def _split2(x):
    hi = x.astype(BF16)
    return hi, (x - hi.astype(F32)).astype(BF16)


def _mm_lhs2(a, b, dims):
    m = a.shape[0]
    hi, lo = _split2(a)
    r = lax.dot_general(jnp.concatenate([hi, lo], axis=0), b, dims, preferred_element_type=F32)
    return r[:m] + r[m:]


def _lane_head(shape, width):
    return lax.shift_right_logical(_iota2(shape, 1), width.bit_length() - 1)


def _head_blocks(x, width):
    lh = _lane_head(x.shape, width)
    return jnp.concatenate([jnp.where(lh == h, x, jnp.zeros_like(x)) for h in range(N_HEADS)],
                           axis=0)


def _head_diag(full, width):
    rows = full.shape[0] // N_HEADS
    lh = _lane_head((rows, full.shape[1]), width)
    out = jnp.zeros((rows, full.shape[1]), F32)
    for h in range(N_HEADS):
        out = out + jnp.where(lh == h, full[h * rows:(h + 1) * rows], 0.0)
    return out


def _tn_lhs2(a, b_bf16):
    hi = a.astype(BF16).astype(F32)
    b = b_bf16.astype(F32)
    return _dot_tn(hi, b) + _dot_tn(a - hi, b)


def _rw_intra_kernel(p_ref, prev_ref, mu_ref, w0_ref, w2_ref, a0_ref, a2_ref, g2_ref, kkw_ref,
                     kaw_ref, rkw_ref, bd_ref, w_out, u0_out, y0_out, rt_out, btp_out, mrb_out,
                     g0_out, pc_out, bonus_out, g_out):
    cs = RW_CHUNK
    cw = N_HEADS * cs
    row = _iota2((cs, cw), 0)
    colc = jnp.bitwise_and(_iota2((cs, cw), 1), cs - 1)
    lower = colc <= row
    strict = colc < row
    eye = (colc == row).astype(F32)
    tri3 = (jnp.bitwise_and(_iota2((cs, 3 * cs), 1), cs - 1) <= _iota2((cs, 3 * cs), 0)).astype(BF16)

    prev_row = jnp.where(pl.program_id(1) == 0, 0.0, prev_ref[0, SUBLANES - 1:SUBLANES, :])
    r_all, ld_all, k_all, v_all, kk_all, ba_all, bonus, g = _rw_prep_math(
        p_ref[0], prev_row, mu_ref, w0_ref, w2_ref, a0_ref, a2_ref, g2_ref, kkw_ref, kaw_ref,
        rkw_ref, bd_ref)
    bonus_out[0] = bonus
    g_out[0] = g

    chunks = range(p_ref.shape[1] // cs)
    rows = [slice(c * cs, (c + 1) * cs) for c in chunks]
    each = lambda f: [f(c) for c in chunks]
    hb = lambda x: _head_blocks(x.astype(BF16), HEAD_DIM)

    def running_log_decay(c):
        ld = ld_all[rows[c]]
        l1 = ld.astype(BF16)
        l2, l3 = _split2(ld - l1.astype(F32))
        return _dot(tri3, jnp.concatenate([l1, l2, l3], axis=0))

    cl = each(running_log_decay)
    p_in = each(lambda c: jnp.exp(cl[c]))
    p_inv = each(lambda c: jnp.exp(-cl[c]))
    at = each(lambda c: -kk_all[rows[c]] * jnp.exp(cl[c] - ld_all[rows[c]]))
    bt = each(lambda c: ba_all[rows[c]] * p_inv[c])
    kt = each(lambda c: k_all[rows[c]] * p_inv[c])
    rt = each(lambda c: r_all[rows[c]] * p_in[c])
    pc = each(lambda c: p_in[c][cs - 1:cs, :])

    lhs = each(lambda c: jnp.concatenate([at[c], rt[c]], axis=0))
    ab_mb = each(lambda c: _mm_lhs2(lhs[c], hb(bt[c]), _NT))
    ak_mk = each(lambda c: _mm_lhs2(lhs[c], hb(kt[c]), _NT))
    a_ab = each(lambda c: jnp.where(strict, ab_mb[c][:cs], 0.0))
    m_rb = each(lambda c: jnp.where(lower, ab_mb[c][cs:], 0.0))
    a_ak = each(lambda c: jnp.where(strict, ak_mk[c][:cs], 0.0))
    m_rk = each(lambda c: jnp.where(lower, ak_mk[c][cs:], 0.0))

    inv = each(lambda c: eye + a_ab[c])
    pw = a_ab
    pw_blocks = each(lambda c: _head_blocks(pw[c].astype(BF16), cs))
    steps = 1
    while steps * 2 < cs:
        pw = each(lambda c: _mm_lhs2(pw[c], pw_blocks[c], _NN))
        pw_blocks = each(lambda c: _head_blocks(pw[c].astype(BF16), cs))
        inv = each(lambda c: inv[c] + _mm_lhs2(inv[c], pw_blocks[c], _NN))
        steps *= 2

    akv_mkv = each(lambda c: _mm_lhs2(jnp.concatenate([a_ak[c], m_rk[c]], axis=0),
                                      hb(v_all[rows[c]]), _NN))
    wu = each(lambda c: _mm_lhs2(
        inv[c], jnp.concatenate([hb(at[c]), hb(akv_mkv[c][:cs])], axis=1), _NN))
    g0 = each(lambda c: _head_diag(_tn_lhs2(v_all[rows[c]], kt[c].astype(BF16)), HEAD_DIM))
    for c in chunks:
        w_out[0, rows[c], :] = wu[c][:, :W_MIX]
        u0_out[0, rows[c], :] = wu[c][:, W_MIX:]
        y0_out[0, rows[c], :] = akv_mkv[c][cs:]
        rt_out[0, rows[c], :] = rt[c]
        btp_out[0, rows[c], :] = bt[c] * pc[c]
        mrb_out[0, rows[c], :] = m_rb[c]
        g0_out[0, rows[c], :] = g0[c] * pc[c]
        pc_out[0, c * SUBLANES:(c + 1) * SUBLANES, :] = jnp.broadcast_to(pc[c], (SUBLANES, W_MIX))


def _rw_intra(p_rw, bsz, seq, mu, w0, w2, a0, a2, g2, k_k, k_a, r_k, ones_bd):
    p3 = p_rw.reshape(bsz, seq, RW_COLS)
    rows = min(RW_CHUNK * RW_CPS, seq)
    cps = rows // RW_CHUNK
    spec = pl.BlockSpec((1, rows, W_MIX), lambda b, c: (b, c, 0))
    pc_spec = pl.BlockSpec((1, SUBLANES * cps, W_MIX), lambda b, c: (b, c, 0))
    vec = lambda width: pl.BlockSpec((1, width), lambda b, c: (0, 0))
    mat = lambda rws: pl.BlockSpec((rws, W_MIX), lambda b, c: (0, 0))
    big = jax.ShapeDtypeStruct((bsz, seq, W_MIX), F32)
    return pl.pallas_call(
        _rw_intra_kernel,
        grid=(bsz, seq // rows),
        in_specs=[
            pl.BlockSpec((1, rows, RW_COLS), lambda b, c: (b, c, 0)),
            pl.BlockSpec((1, SUBLANES, RW_COLS),
                         lambda b, c: (b, jnp.maximum(c * (rows // SUBLANES) - 1, 0), 0)),
            vec(RW_COLS), vec(W_MIX), mat(RW_DECAY_LORA), vec(W_MIX), mat(RW_AAA_LORA),
            mat(RW_GATE_LORA), vec(W_MIX), vec(W_MIX), vec(W_MIX), mat(W_MIX),
        ],
        out_specs=[spec] * 7 + [pc_spec, spec, spec],
        out_shape=[big] * 7 + [jax.ShapeDtypeStruct((bsz, seq // RW_CHUNK * SUBLANES, W_MIX), F32),
                               big, big],
        compiler_params=_cparams("parallel", "parallel"),
        name="rw_intra",
    )(p3, p3, mu, w0, w2, a0, a2, g2, k_k, k_a, r_k, ones_bd)


def _rw_state_kernel(w_ref, u0_ref, y0_ref, rt_ref, btp_ref, mrb_ref, g0_ref, pc_ref,
                     bonus_ref, g_ref, gnw_ref, gnb_ref, ones_ref, o_ref, state_ref):
    cs = RW_CHUNK

    @pl.when(pl.program_id(1) == 0)
    def _():
        state_ref[...] = jnp.zeros_like(state_ref)

    batch = range(w_ref.shape[0])
    each = lambda f: [f(b) for b in batch]
    hb = lambda x: _head_blocks(x.astype(BF16), HEAD_DIM)
    s0 = each(lambda b: state_ref[b])
    wr = each(lambda b: _mm_lhs2(jnp.concatenate([w_ref[b], rt_ref[b]], axis=0), hb(s0[b]), _NT))
    u = each(lambda b: wr[b][:cs] + u0_ref[b])
    y = each(lambda b: wr[b][cs:] + _mm_lhs2(mrb_ref[b], hb(u[b]), _NN) + y0_ref[b])
    su = each(lambda b: _head_diag(_tn_lhs2(u[b], btp_ref[b].astype(BF16)), HEAD_DIM))
    for b in batch:
        state_ref[b] = s0[b] * pc_ref[b, 0:1, :] + g0_ref[b] + su[b]

    inv_n = 1.0 / HEAD_DIM
    mean = each(lambda b: _mm_lhs2(y[b], ones_ref[...], _NN) * inv_n)
    yc = each(lambda b: y[b] - mean[b])
    var = each(lambda b: _mm_lhs2(yc[b] * yc[b], ones_ref[...], _NN) * inv_n)
    for b in batch:
        yn = yc[b] * lax.rsqrt(var[b] + GN_EPS) * gnw_ref[...] + gnb_ref[...]
        o_ref[b] = ((yn + bonus_ref[b]) * g_ref[b]).astype(o_ref.dtype)


def _rw_state(intra, bonus, g, gn_w, gn_b, ones_bd, bsz, seq):
    gb = RW_BATCH_GROUP if bsz % RW_BATCH_GROUP == 0 else 1
    spec = pl.BlockSpec((gb, RW_CHUNK, W_MIX), lambda b, c: (b, c, 0))
    pc_spec = pl.BlockSpec((gb, SUBLANES, W_MIX), lambda b, c: (b, c, 0))
    vec = pl.BlockSpec((1, W_MIX), lambda b, c: (0, 0))
    out = pl.pallas_call(
        _rw_state_kernel,
        grid=(bsz // gb, seq // RW_CHUNK),
        in_specs=[spec] * 7 + [pc_spec, spec, spec, vec, vec,
                               pl.BlockSpec((W_MIX, W_MIX), lambda b, c: (0, 0))],
        out_specs=spec,
        out_shape=jax.ShapeDtypeStruct((bsz, seq, W_MIX), BF16),
        scratch_shapes=[pltpu.VMEM((gb, HEAD_DIM, W_MIX), F32)],
        compiler_params=_cparams("parallel", "arbitrary"),
        name="rw_state",
    )(*intra, bonus.reshape(bsz, seq, W_MIX), g.reshape(bsz, seq, W_MIX), gn_w, gn_b,
      ones_bd.astype(BF16))
    return out.reshape(bsz * seq, W_MIX)


def _store_token_rows(ref, val):
    tm, d = val.shape
    ch = d // LANES
    for s in range(ch):
        ref[pl.ds(s, tm, stride=ch), :] = val[:, s * LANES:(s + 1) * LANES]


def _load_token_rows(ref, tm, ch):
    return jnp.concatenate([ref[pl.ds(s, tm, stride=ch), :] for s in range(ch)], axis=1)


def _outproj_router_kernel(x_ref, sb_ref, gm_ref, rw_ref, fx_ref, wo_ref, g_ref, wr_ref, br_ref,
                           su_ref, x_out, h_out, idx_out, gate_out, rank_out, cnt_out, cnt_ref):
    mix = jnp.concatenate([sb_ref[...], gm_ref[...], rw_ref[...], fx_ref[...]], axis=1)
    x = x_ref[...] + _dot(mix, wo_ref[...])
    x_out[...] = x
    h = x * lax.rsqrt(jnp.mean(x * x, axis=-1, keepdims=True) + RMS_EPS) * g_ref[...]
    _store_token_rows(h_out, h)

    nr = wr_ref.shape[0]
    h_hi, h_lo = _split2(h)
    w_hi, w_lo = _split2(wr_ref[...])
    part = _dot_nt(jnp.concatenate([w_hi, w_lo], axis=0), h_hi)
    lg = part[:nr] + part[nr:] + _dot_nt(w_hi, h_lo) + br_ref[...]
    tm = lg.shape[1]
    gl = [lg[g:g + 1, :] for g in range(N_EXPERT_GROUPS)]
    gmax = gl[0]
    gsel = jnp.zeros((1, tm), jnp.int32)
    for g in range(1, N_EXPERT_GROUPS):
        better = gl[g] > gmax
        gsel = jnp.where(better, g, gsel)
        gmax = jnp.where(better, gl[g], gmax)
    denom = gl[0] * 0.0
    for g in range(N_EXPERT_GROUPS):
        denom = denom + jnp.exp(gl[g] - gmax)
    g_gate = 1.0 / denom

    e0 = SUBLANES
    ing = lg[e0:e0 + EXPERTS_PER_GROUP, :]
    for g in range(1, N_EXPERT_GROUPS):
        ing = jnp.where(gsel == g, lg[e0 + g * EXPERTS_PER_GROUP:e0 + (g + 1) * EXPERTS_PER_GROUP, :], ing)
    ridx = _iota2(ing.shape, 0)
    m1 = jnp.max(ing, axis=0, keepdims=True)
    i1 = jnp.min(jnp.where(ing == m1, ridx, EXPERTS_PER_GROUP), axis=0, keepdims=True)
    rest = jnp.where(ridx == i1, -jnp.inf, ing)
    m2 = jnp.max(rest, axis=0, keepdims=True)
    i2 = jnp.min(jnp.where(rest == m2, ridx, EXPERTS_PER_GROUP), axis=0, keepdims=True)
    e2 = jnp.exp(m2 - m1)
    s = 1.0 + e2
    e_sel = (gsel * EXPERTS_PER_GROUP + i1, gsel * EXPERTS_PER_GROUP + i2)
    idx_out[0:1, :] = e_sel[0]
    idx_out[1:2, :] = e_sel[1]
    gate_out[0:1, :] = (1.0 / s) * g_gate
    gate_out[1:2, :] = (e2 / s) * g_gate

    @pl.when(pl.program_id(0) == 0)
    def _():
        cnt_ref[...] = jnp.zeros_like(cnt_ref)

    erow = _iota2((N_EXPERTS, tm), 0)
    onehot = [(erow == e).astype(F32) for e in e_sel]
    both = onehot[0] + onehot[1]
    before = cnt_ref[...] + _dot(both.astype(BF16), su_ref[...])
    for k in range(TOP_K):
        rank_out[k:k + 1, :] = jnp.sum(onehot[k] * before, axis=0, keepdims=True).astype(jnp.int32)
    cnt_ref[...] = cnt_ref[...] + jnp.sum(both, axis=1, keepdims=True)
    cnt_out[...] = jnp.broadcast_to(cnt_ref[...], cnt_out.shape).astype(jnp.int32)


def _outproj_router(x2, ys, w_out, g2, w_router_t, b_router):
    n, d = x2.shape
    tm = min(ROW_TILE, n)
    nr = w_router_t.shape[0]
    ymix = pl.BlockSpec((tm, W_MIX), lambda i: (i, 0))
    r = jnp.arange(tm)
    strict_upper = (r[:, None] < r[None, :]).astype(BF16)
    return pl.pallas_call(
        _outproj_router_kernel,
        grid=(n // tm,),
        in_specs=[
            pl.BlockSpec((tm, d), lambda i: (i, 0)),
            ymix, ymix, ymix, ymix,
            pl.BlockSpec((4 * W_MIX, d), lambda i: (0, 0)),
            pl.BlockSpec((1, d), lambda i: (0, 0)),
            pl.BlockSpec((nr, d), lambda i: (0, 0)),
            pl.BlockSpec((nr, 1), lambda i: (0, 0)),
            pl.BlockSpec((tm, tm), lambda i: (0, 0)),
        ],
        out_specs=[
            pl.BlockSpec((tm, d), lambda i: (i, 0)),
            pl.BlockSpec((tm * (d // LANES), LANES), lambda i: (i, 0)),
            pl.BlockSpec((TOP_K, tm), lambda i: (0, i)),
            pl.BlockSpec((TOP_K, tm), lambda i: (0, i)),
            pl.BlockSpec((TOP_K, tm), lambda i: (0, i)),
            pl.BlockSpec((N_EXPERTS, LANES), lambda i: (0, 0)),
        ],
        out_shape=[
            jax.ShapeDtypeStruct((n, d), F32),
            jax.ShapeDtypeStruct((n * (d // LANES), LANES), F32),
            jax.ShapeDtypeStruct((TOP_K, n), jnp.int32),
            jax.ShapeDtypeStruct((TOP_K, n), F32),
            jax.ShapeDtypeStruct((TOP_K, n), jnp.int32),
            jax.ShapeDtypeStruct((N_EXPERTS, LANES), jnp.int32),
        ],
        scratch_shapes=[pltpu.VMEM((N_EXPERTS, 1), F32)],
        compiler_params=_cparams("arbitrary"),
        name="outproj_router",
    )(x2, *ys, w_out, g2, w_router_t, b_router, strict_upper)


def _token_copy(src_hbm, dst, src_tok, dst_tok, sem, ch):
    return pltpu.make_async_copy(
        src_hbm.at[pl.ds(pl.multiple_of(src_tok * ch, ch), ch)],
        dst.at[pl.ds(pl.multiple_of(dst_tok * ch, ch), ch)], sem)


def _dispatch_kernel(pad_end_ref, slot_ref, h_ref, o_hbm, zbuf, zsem, sem, *, ch, tm):
    i = pl.program_id(0)

    def zero_copy(first_slot):
        return pltpu.make_async_copy(
            zbuf, o_hbm.at[pl.ds(pl.multiple_of(first_slot * ch, ch), MOE_TB * ch)], zsem)

    @pl.when(i == 0)
    def _():
        zbuf[...] = jnp.zeros_like(zbuf)
        tails = [jnp.maximum(pad_end_ref[e] - MOE_TB, 0) for e in range(N_EXPERTS)]
        for first in tails:
            zero_copy(first).start()
        for first in tails:
            zero_copy(first).wait()

        def unused_block(b, _):
            zero_copy(b * MOE_TB).start()
            zero_copy(b * MOE_TB).wait()
            return 0

        lax.fori_loop(pad_end_ref[N_EXPERTS - 1] // MOE_TB, o_hbm.shape[0] // (MOE_TB * ch),
                      unused_block, 0)

    def start(r, _):
        for k in range(TOP_K):
            _token_copy(h_ref, o_hbm, r, slot_ref[0, 0, r * TOP_K + k], sem, ch).start(
                priority=k % 2)
        return 0

    lax.fori_loop(0, tm, start, 0, unroll=DMA_ISSUE_UNROLL)
    for k in range(TOP_K):
        pltpu.make_async_copy(h_ref, o_hbm.at[pl.ds(0, tm * ch)], sem).wait()


def _dispatch(pad_end, slot_flat, h_rows, n_slots, ch):
    n_tok = h_rows.shape[0] // ch
    tm = min(DISPATCH_TOKENS, n_tok)
    n_steps = n_tok // tm
    return pl.pallas_call(
        functools.partial(_dispatch_kernel, ch=ch, tm=tm),
        grid_spec=pltpu.PrefetchScalarGridSpec(
            num_scalar_prefetch=1,
            grid=(n_steps,),
            in_specs=[
                pl.BlockSpec((1, 1, tm * TOP_K), lambda i, pe: (i, 0, 0), memory_space=pltpu.SMEM),
                pl.BlockSpec((tm * ch, LANES), lambda i, pe: (i, 0)),
            ],
            out_specs=pl.BlockSpec(memory_space=pl.ANY),
            scratch_shapes=[pltpu.VMEM((MOE_TB * ch, LANES), F32), pltpu.SemaphoreType.DMA,
                            pltpu.SemaphoreType.DMA],
        ),
        out_shape=jax.ShapeDtypeStruct((n_slots * ch, LANES), h_rows.dtype),
        compiler_params=_cparams("arbitrary"),
        name="moe_dispatch",
    )(pad_end, slot_flat.reshape(n_steps, 1, tm * TOP_K), h_rows)


def _expert_kernel(be_ref, nb_ref, x_ref, wg_ref, wu_ref, wd_ref, o_ref, wg_c, wu_c, wd_c, *, ch):
    i = pl.program_id(0)

    @pl.when((i < nb_ref[0]) & ((i == 0) | (be_ref[i] != be_ref[jnp.maximum(i - 1, 0)])))
    def _():
        wg_c[...] = wg_ref[0, 0].astype(BF16)
        wu_c[...] = wu_ref[0, 0].astype(BF16)
        wd_c[...] = wd_ref[0, 0].astype(BF16)

    @pl.when(i < nb_ref[0])
    def _():
        x = _load_token_rows(x_ref, MOE_TB, ch).astype(BF16)
        a = _dot(x, wg_c[...])
        u = _dot(x, wu_c[...])
        hid = (a * jax.nn.sigmoid(a) * u).astype(BF16)
        _store_token_rows(o_ref, _dot(hid, wd_c[...]))

    @pl.when(i >= nb_ref[0])
    def _():
        o_ref[...] = jnp.zeros_like(o_ref)


def _expert_blocks(block_expert, n_used, xb_rows, w_gate, w_up, w_down, layer):
    d, hid = w_gate.shape[2], w_gate.shape[3]
    ch = d // LANES
    n_slots = xb_rows.shape[0] // ch
    return pl.pallas_call(
        functools.partial(_expert_kernel, ch=ch),
        grid_spec=pltpu.PrefetchScalarGridSpec(
            num_scalar_prefetch=2,
            grid=(n_slots // MOE_TB,),
            in_specs=[
                pl.BlockSpec((MOE_TB * ch, LANES), lambda i, be, nb: (i, 0)),
                pl.BlockSpec((1, 1, d, hid), lambda i, be, nb: (layer, be[i], 0, 0)),
                pl.BlockSpec((1, 1, d, hid), lambda i, be, nb: (layer, be[i], 0, 0)),
                pl.BlockSpec((1, 1, hid, d), lambda i, be, nb: (layer, be[i], 0, 0)),
            ],
            out_specs=pl.BlockSpec((MOE_TB * ch, LANES), lambda i, be, nb: (i, 0)),
            scratch_shapes=[pltpu.VMEM((d, hid), BF16), pltpu.VMEM((d, hid), BF16),
                            pltpu.VMEM((hid, d), BF16)],
        ),
        out_shape=jax.ShapeDtypeStruct((n_slots * ch, LANES), F32),
        compiler_params=_cparams("arbitrary"),
        name="moe_experts",
    )(block_expert, n_used, xb_rows, w_gate, w_up, w_down)


def _combine_kernel(slot_ref, slot_next_ref, x_ref, gate_ref, norm_ref, yb_hbm, o_ref, buf, sem,
                    *, ch, final_norm):
    i = pl.program_id(0)
    last = pl.num_programs(0) - 1
    tm = x_ref.shape[0]

    def issue(idx_ref, parity):
        def start(r, _):
            for k in range(TOP_K):
                _token_copy(yb_hbm, buf.at[parity, k], idx_ref[0, 0, r * TOP_K + k], r,
                            sem.at[parity], ch).start(priority=k % 2)
            return 0

        lax.fori_loop(0, tm, start, 0, unroll=DMA_ISSUE_UNROLL)

    @pl.when(i == 0)
    def _():
        issue(slot_ref, 0)

    @pl.when(i < last)
    def _():
        issue(slot_next_ref, (i + 1) % 2)

    par = i % 2
    for k in range(TOP_K):
        pltpu.make_async_copy(yb_hbm.at[pl.ds(0, tm * ch)], buf.at[par, k], sem.at[par]).wait()
    y = (_load_token_rows(buf.at[par, 0], tm, ch) * gate_ref[:, 0:1]
         + _load_token_rows(buf.at[par, 1], tm, ch) * gate_ref[:, 1:2])
    x = x_ref[...] + y
    if final_norm:
        x = x * lax.rsqrt(jnp.mean(x * x, axis=-1, keepdims=True) + RMS_EPS) * norm_ref[...]
    o_ref[...] = x


def _combine(slot_flat, x2, gates, yb_rows, norm_g, final_norm):
    n, d = x2.shape
    ch = d // LANES
    tm = min(256, n)
    n_steps = n // tm
    slots3 = slot_flat.reshape(n_steps, 1, tm * TOP_K)
    return pl.pallas_call(
        functools.partial(_combine_kernel, ch=ch, final_norm=final_norm),
        grid=(n_steps,),
        in_specs=[
            pl.BlockSpec((1, 1, tm * TOP_K), lambda i: (i, 0, 0), memory_space=pltpu.SMEM),
            pl.BlockSpec((1, 1, tm * TOP_K), lambda i: (jnp.minimum(i + 1, n_steps - 1), 0, 0),
                         memory_space=pltpu.SMEM),
            pl.BlockSpec((tm, d), lambda i: (i, 0)),
            pl.BlockSpec((tm, TOP_K), lambda i: (i, 0)),
            pl.BlockSpec((1, d), lambda i: (0, 0)),
            pl.BlockSpec(memory_space=pl.ANY),
        ],
        out_specs=pl.BlockSpec((tm, d), lambda i: (i, 0)),
        scratch_shapes=[pltpu.VMEM((2, TOP_K, tm * ch, LANES), F32),
                        pltpu.SemaphoreType.DMA((2,))],
        out_shape=jax.ShapeDtypeStruct((n, d), F32),
        compiler_params=_cparams("arbitrary"),
        name="moe_combine",
    )(slots3, slots3, x2, gates, norm_g, yb_rows)


def _routing_plan(expert_idx_t, rank_t, counts, n_tok):
    padded = (counts + MOE_TB - 1) // MOE_TB * MOE_TB
    pad_end = jnp.cumsum(padded)
    pad_start = pad_end - padded
    start_t = jnp.sum(jnp.where(expert_idx_t[:, :, None] == jnp.arange(N_EXPERTS), pad_start, 0),
                      axis=-1)
    slot = jnp.transpose(start_t + rank_t).reshape(n_tok * TOP_K)
    n_blocks = -(-(n_tok * TOP_K) // MOE_TB) + N_EXPERTS
    block_expert = jnp.minimum(
        jnp.sum(pad_end[None, :] <= (jnp.arange(n_blocks) * MOE_TB)[:, None], axis=1),
        N_EXPERTS - 1).astype(jnp.int32)
    n_used = (pad_end[-1] // MOE_TB).astype(jnp.int32).reshape(1)
    return slot.astype(jnp.int32), pad_end.astype(jnp.int32), block_expert, n_used, n_blocks * MOE_TB


def _moe(x2, h2, expert_idx_t, gates_t, rank_t, counts, w_gate, w_up, w_down, layer, norm_g,
         final_norm):
    n_tok = x2.shape[0]
    slot, pad_end, block_expert, n_used, n_slots = _routing_plan(expert_idx_t, rank_t, counts, n_tok)
    xb = _dispatch(pad_end, slot, h2, n_slots, x2.shape[1] // LANES)
    yb = _expert_blocks(block_expert, n_used, xb, w_gate, w_up, w_down, layer)
    return _combine(slot, x2, jnp.transpose(gates_t), yb, norm_g, final_norm)


def _prepare_params(w_in, gm_w_s, gm_b, rw_mu, rw_w0, rw_w2, rw_a0, rw_a2, rw_g2, rw_k_k, rw_k_a,
                    rw_r_k, rw_gn_w, rw_gn_b, fx_b_f, w_out, router_group_w, router_group_b,
                    router_expert_w, router_expert_b):
    depth, d = w_in.shape[0], w_in.shape[1]
    n_main = 3 * W_MIX + 2 * W_MIX + RW_COLS + 3 * W_MIX
    t = lambda a: jnp.swapaxes(a, 1, 2)
    pad_rows = lambda a, rows: jnp.concatenate(
        [a, jnp.zeros((depth, rows - a.shape[1]) + a.shape[2:], a.dtype)], axis=1)
    row = lambda a: a.reshape(depth, 1, -1)
    wr = jnp.concatenate([pad_rows(t(router_group_w), SUBLANES), t(router_expert_w)], axis=1)
    br = jnp.concatenate([pad_rows(router_group_b[:, :, None], SUBLANES),
                          router_expert_b[:, :, None]], axis=1)
    return dict(
        w_main=w_in[:, :, :n_main].astype(BF16),
        w_gate_t=pad_rows(t(w_in[:, :, n_main:n_main + N_HEADS]), SUBLANES).astype(BF16),
        fx_bias=pad_rows(fx_b_f[:, :, None], SUBLANES),
        gm_w=jnp.transpose(gm_w_s, (0, 2, 1, 3)).reshape(depth, GM_CHUNK, N_HEADS * GM_CHUNK),
        gm_b=jnp.repeat(t(gm_b), HEAD_DIM, axis=2),
        mu=row(rw_mu), w0=row(rw_w0), w2=rw_w2.astype(BF16), a0=row(rw_a0),
        a2=rw_a2.astype(BF16), g2=rw_g2.astype(BF16), k_k=row(rw_k_k), k_a=row(rw_k_a),
        r_k=row(rw_r_k), gn_w=row(rw_gn_w), gn_b=row(rw_gn_b),
        w_out=w_out.astype(BF16), wr=wr, br=br,
    )


def _layer_params(l, w_in, gm_w_s, gm_b, rw_mu, rw_w0, rw_w2, rw_a0, rw_a2, rw_g2, rw_k_k, rw_k_a,
                  rw_r_k, rw_gn_w, rw_gn_b, fx_b_f, w_out, router_group_w, router_group_b,
                  router_expert_w, router_expert_b):
    allp = _prepare_params(w_in, gm_w_s, gm_b, rw_mu, rw_w0, rw_w2, rw_a0, rw_a2, rw_g2, rw_k_k,
                           rw_k_a, rw_r_k, rw_gn_w, rw_gn_b, fx_b_f, w_out, router_group_w,
                           router_group_b, router_expert_w, router_expert_b)
    return {k: v[l] for k, v in allp.items()}


def kernel(x, norm1_g, w_in, gm_w_s, gm_b, rw_mu, rw_w0, rw_w2, rw_a0, rw_a2, rw_g2, rw_k_k, rw_k_a, rw_r_k, rw_gn_w, rw_gn_b, fx_b_f, w_out, norm2_g, router_group_w, router_group_b, router_expert_w, router_expert_b, exp_w_gate, exp_w_up, exp_w_down, final_norm_g):
    bsz, seq, d = x.shape
    depth = w_in.shape[0]
    assert seq % min(ATT_TQB, seq) == 0 and seq % min(FX_TQB, seq) == 0
    assert seq % RW_CHUNK == 0 and seq % (2 * LANES) == 0
    x2 = x.reshape(bsz * seq, d)
    hd = jnp.arange(W_MIX) // HEAD_DIM
    ones_bd = (hd[:, None] == hd[None, :]).astype(BF16)
    allp = _prepare_params(w_in, gm_w_s, gm_b, rw_mu, rw_w0, rw_w2, rw_a0, rw_a2, rw_g2, rw_k_k,
                           rw_k_a, rw_r_k, rw_gn_w, rw_gn_b, fx_b_f, w_out, router_group_w,
                           router_group_b, router_expert_w, router_expert_b)
    for l in range(depth):
        lp = {k: v[l] for k, v in allp.items()}
        p_sb, p_gm, p_rw, p_fx, grow = _norm_inproj(x2, norm1_g[l].reshape(1, d), lp["w_main"],
                                                    lp["w_gate_t"])
        y_sb = _sb_attention(p_sb, bsz, seq)
        y_gm = _gm_mix(p_gm, lp["gm_w"], lp["gm_b"], ones_bd)
        intra = _rw_intra(p_rw, bsz, seq, lp["mu"], lp["w0"], lp["w2"], lp["a0"], lp["a2"],
                          lp["g2"], lp["k_k"], lp["k_a"], lp["r_k"], ones_bd)
        y_rw = _rw_state(intra[:8], intra[8], intra[9], lp["gn_w"], lp["gn_b"], ones_bd, bsz, seq)
        grow3 = jnp.transpose(grow.reshape(SUBLANES, bsz, seq), (1, 0, 2))
        cum_row = _fx_cum(grow3, lp["fx_bias"])
        y_fx = _fx_attention(p_fx, cum_row, bsz, seq)
        x2, h2, eidx_t, gates_t, rank_t, cnt = _outproj_router(
            x2, (y_sb, y_gm, y_rw, y_fx), lp["w_out"], norm2_g[l].reshape(1, d), lp["wr"], lp["br"])
        x2 = _moe(x2, h2, eidx_t, gates_t, rank_t, cnt[:, 0], exp_w_gate, exp_w_up, exp_w_down, l,
                  final_norm_g.reshape(1, d), l == depth - 1)
    return x2.reshape(bsz, seq, d)
```

```python
import functools

import jax
import jax.numpy as jnp
from jax import lax
from jax.experimental import pallas as pl
from jax.experimental.pallas import tpu as pltpu

F32 = jnp.float32
BF16 = jnp.bfloat16
HIGHEST = lax.Precision.HIGHEST

HEAD_DIM = 64
N_HEADS = 4
W_MIX = N_HEADS * HEAD_DIM
GM_CHUNK = 128
GM_ROWS = 1024
RW_DECAY_LORA = 64
RW_AAA_LORA = 64
RW_GATE_LORA = 128
RW_COLS = 3 * W_MIX + RW_DECAY_LORA + RW_AAA_LORA + RW_GATE_LORA
N_EXPERT_GROUPS = 4
EXPERTS_PER_GROUP = 8
N_EXPERTS = N_EXPERT_GROUPS * EXPERTS_PER_GROUP
TOP_K = 2
RMS_EPS = 1e-6
LN_EPS = 1e-5
GN_EPS = 64e-5
L2_EPS = 1e-12
LOG2_E = 1.4426950408889634

V7X_VMEM_LIMIT_BYTES = 56 * 1024 * 1024
LANES = 128
SUBLANES = 8

ATT_TK = 128
FX_TK = 256
SB_TILES_PER_STEP = 2
ATT_TQB = 512
FX_TQB = 1024
ATT_W = N_HEADS * LANES
RW_CHUNK = 64
RW_CPS = 8
RW_BATCH_GROUP = 16
ROW_TILE = 512
MOE_TB = 512
DMA_ISSUE_UNROLL = 8
DISPATCH_TOKENS = 1024


def _cparams(*sem):
    return pltpu.CompilerParams(dimension_semantics=sem, vmem_limit_bytes=V7X_VMEM_LIMIT_BYTES)


def _dot(a, b, precision=None):
    return jnp.dot(a, b, preferred_element_type=F32, precision=precision)


def _dot_nt(a, b, precision=None):
    return lax.dot_general(a, b, (((1,), (1,)), ((), ())), preferred_element_type=F32,
                           precision=precision)


def _dot_tn(a, b, precision=None):
    return lax.dot_general(a, b, (((0,), (0,)), ((), ())), preferred_element_type=F32,
                           precision=precision)


def _iota2(shape, dim):
    return lax.broadcasted_iota(jnp.int32, shape, dim)


def _store_heads_padded(ref, col0, src, fill):
    low = _iota2((src.shape[0], LANES), 1) < HEAD_DIM
    for pair in range(N_HEADS // 2):
        slab = src[:, pair * LANES:(pair + 1) * LANES]
        swapped = pltpu.roll(slab, HEAD_DIM, axis=1)
        for odd, val in enumerate((slab, swapped)):
            c = col0 + (2 * pair + odd) * LANES
            ref[:, c:c + LANES] = jnp.where(low, val, fill).astype(ref.dtype)


def _norm_inproj_kernel(x_ref, g_ref, w_ref, wgt_ref, sb_ref, gm_ref, rw_ref, fx_ref, grow_ref):
    x = x_ref[...]
    h = x * lax.rsqrt(jnp.mean(x * x, axis=-1, keepdims=True) + RMS_EPS) * g_ref[...]
    hb = h.astype(BF16)
    p = _dot(hb, w_ref[...])
    scale = HEAD_DIM ** -0.5
    wp = N_HEADS * LANES
    o = 0
    _store_heads_padded(sb_ref, 0, p[:, o:o + W_MIX] * scale, 0.0)
    _store_heads_padded(sb_ref, wp, p[:, o + W_MIX:o + 2 * W_MIX], 0.0)
    _store_heads_padded(sb_ref, 2 * wp, p[:, o + 2 * W_MIX:o + 3 * W_MIX], 0.0)
    o += 3 * W_MIX
    gm_ref[...] = p[:, o:o + 2 * W_MIX]
    o += 2 * W_MIX
    rw_ref[...] = p[:, o:o + RW_COLS]
    o += RW_COLS
    _store_heads_padded(fx_ref, 0, p[:, o:o + W_MIX] * scale, 0.0)
    _store_heads_padded(fx_ref, wp, p[:, o + W_MIX:o + 2 * W_MIX], 0.0)
    _store_heads_padded(fx_ref, 2 * wp, p[:, o + 2 * W_MIX:o + 3 * W_MIX], 1.0)
    grow_ref[...] = _dot_nt(wgt_ref[...], hb)


def _norm_inproj(x2, g, w_main, w_gate_t):
    n, d = x2.shape
    tm = min(ROW_TILE, n)
    cols = w_main.shape[1]
    return pl.pallas_call(
        _norm_inproj_kernel,
        grid=(n // tm,),
        in_specs=[
            pl.BlockSpec((tm, d), lambda i: (i, 0)),
            pl.BlockSpec((1, d), lambda i: (0, 0)),
            pl.BlockSpec((d, cols), lambda i: (0, 0)),
            pl.BlockSpec((SUBLANES, d), lambda i: (0, 0)),
        ],
        out_specs=[
            pl.BlockSpec((tm, 3 * ATT_W), lambda i: (i, 0)),
            pl.BlockSpec((tm, 2 * W_MIX), lambda i: (i, 0)),
            pl.BlockSpec((tm, RW_COLS), lambda i: (i, 0)),
            pl.BlockSpec((tm, 3 * ATT_W), lambda i: (i, 0)),
            pl.BlockSpec((SUBLANES, tm), lambda i: (0, i)),
        ],
        out_shape=[
            jax.ShapeDtypeStruct((n, 3 * ATT_W), BF16),
            jax.ShapeDtypeStruct((n, 2 * W_MIX), F32),
            jax.ShapeDtypeStruct((n, RW_COLS), F32),
            jax.ShapeDtypeStruct((n, 3 * ATT_W), BF16),
            jax.ShapeDtypeStruct((SUBLANES, n), F32),
        ],
        compiler_params=_cparams("parallel"),
        name="norm_inproj",
    )(x2, g, w_main, w_gate_t)


def _sb_kernel(q_ref, k_ref, v_ref, tri_ref, o_ref, c_ref, acc_ref):
    qi = pl.program_id(1)
    tqb = q_ref.shape[1]
    hl = lambda h: slice(h * LANES, (h + 1) * LANES)

    def key_tiles(tiles):
        meta, zs, rrs = [], [], []
        for j, off in tiles:
            r0 = 0 if off is None else off * ATT_TK
            rows = tqb - r0
            strict = (None if off is None
                      else _iota2((rows, ATT_TK), 1) < _iota2((rows, ATT_TK), 0))
            meta.append((pl.multiple_of(j * ATT_TK, ATT_TK), r0, rows, strict))
        for start, r0, rows, strict in meta:
            zs.append([_dot_nt(q_ref[0, r0:, hl(h)], k_ref[0, pl.ds(start, ATT_TK), hl(h)])
                       for h in range(N_HEADS)])
        for (start, r0, rows, strict), z4 in zip(meta, zs):
            parts = []
            for z in z4:
                sp = jnp.maximum(z, 0.0) + jnp.log(1.0 + jnp.exp2(jnp.abs(z) * -LOG2_E))
                if strict is not None:
                    sp = jnp.where(strict, sp, 0.0)
                hi = sp.astype(BF16)
                lo = (sp - hi.astype(F32)).astype(BF16)
                parts.append(jnp.concatenate([hi, lo], axis=1))
            rrs.append(_dot(jnp.concatenate(parts, axis=0), tri_ref[...]))
        for (start, r0, rows, strict), z4, rr in zip(meta, zs, rrs):
            ps = []
            for h in range(N_HEADS):
                rh = rr[h * rows:(h + 1) * rows]
                p = jnp.exp(z4[h] - rh[:, :ATT_TK] - c_ref[h, r0:, :])
                if strict is not None:
                    p = jnp.where(strict, p, 0.0)
                ps.append(p.astype(BF16))
                c_ref[h, r0:, :] += rh[:, ATT_TK:]
            pv = [_dot(ps[h], v_ref[0, pl.ds(start, ATT_TK), hl(h)]) for h in range(N_HEADS)]
            for h in range(N_HEADS):
                acc_ref[h, r0:, :] += pv[h]

    c_ref[...] = jnp.zeros_like(c_ref)
    acc_ref[...] = jnp.zeros_like(acc_ref)
    n_diag = tqb // ATT_TK
    group = SB_TILES_PER_STEP if n_diag % SB_TILES_PER_STEP == 0 else 1
    for first in reversed(range(0, n_diag, group)):
        key_tiles([(qi * n_diag + off, off) for off in reversed(range(first, first + group))])

    def body(jj, carry):
        right = qi * n_diag - 1 - jj * group
        key_tiles([(right - g, None) for g in range(group)])
        return carry

    lax.fori_loop(0, qi * n_diag // group, body, 0)
    for h in range(N_HEADS):
        o_ref[0, :, h * HEAD_DIM:(h + 1) * HEAD_DIM] = acc_ref[h, :, :HEAD_DIM].astype(o_ref.dtype)


def _sb_attention(qkv, bsz, seq):
    qkv3 = qkv.reshape(bsz, seq, 3 * ATT_W)
    r = jnp.arange(ATT_TK)
    tri = (r[:, None] >= r[None, :]).astype(BF16)
    blk = jnp.concatenate([tri, jnp.ones((ATT_TK, ATT_TK), BF16)], axis=1)
    tri2 = jnp.concatenate([blk, blk], axis=0)
    tqb = min(ATT_TQB, seq)
    out = pl.pallas_call(
        _sb_kernel,
        grid=(bsz, seq // tqb),
        in_specs=[
            pl.BlockSpec((1, tqb, ATT_W), lambda b, i: (b, i, 0)),
            pl.BlockSpec((1, seq, ATT_W), lambda b, i: (b, 0, 1)),
            pl.BlockSpec((1, seq, ATT_W), lambda b, i: (b, 0, 2)),
            pl.BlockSpec((2 * ATT_TK, 2 * ATT_TK), lambda b, i: (0, 0)),
        ],
        out_specs=pl.BlockSpec((1, tqb, W_MIX), lambda b, i: (b, i, 0)),
        out_shape=jax.ShapeDtypeStruct((bsz, seq, W_MIX), BF16),
        scratch_shapes=[pltpu.VMEM((N_HEADS, tqb, ATT_TK), F32),
                        pltpu.VMEM((N_HEADS, tqb, LANES), F32)],
        compiler_params=_cparams("parallel", "parallel"),
        name="sb_attention",
    )(qkv3, qkv3, qkv3, tri2)
    return out.reshape(bsz * seq, W_MIX)


def _fx_cum_kernel(g_ref, b_ref, cum_ref):
    seq = g_ref.shape[2]
    blk = 2 * LANES
    r = _iota2((blk, blk), 0)
    c = _iota2((blk, blk), 1)
    tri = (r <= c).astype(F32)
    carry = jnp.zeros((SUBLANES, 1), F32)
    for s in range(seq // blk):
        x = g_ref[0, :, s * blk:(s + 1) * blk] + b_ref[...]
        lf = -(jnp.maximum(-x, 0.0) + jnp.log(1.0 + jnp.exp(-jnp.abs(x))))
        cs = _dot(lf, tri, precision=HIGHEST) + carry
        cum_ref[0, :, s * blk:(s + 1) * blk] = cs
        carry = cs[:, blk - 1:blk]


def _fx_cum(grow3, bias_col):
    bsz, _, seq = grow3.shape
    return pl.pallas_call(
        _fx_cum_kernel,
        grid=(bsz,),
        in_specs=[
            pl.BlockSpec((1, SUBLANES, seq), lambda b: (b, 0, 0)),
            pl.BlockSpec((SUBLANES, 1), lambda b: (0, 0)),
        ],
        out_specs=pl.BlockSpec((1, SUBLANES, seq), lambda b: (b, 0, 0)),
        out_shape=jax.ShapeDtypeStruct((bsz, SUBLANES, seq), F32),
        compiler_params=_cparams("parallel"),
        name="fx_cum",
    )(grow3, bias_col)


def _fx_kernel(q_ref, k_ref, v_ref, crow_ref, ccol_ref, o_ref, m_ref, acc_ref, ct_ref):
    qi = pl.program_id(1)
    tqb = q_ref.shape[1]
    hl = lambda h: slice(h * LANES, (h + 1) * LANES)

    def key_tile(j, off):
        r0 = 0 if off is None else off * FX_TK
        rows = tqb - r0
        start = pl.multiple_of(j * FX_TK, FX_TK)
        halves = range(FX_TK // LANES)
        if off is not None:
            causal = [_iota2((rows, LANES), 1) + s * LANES <= _iota2((rows, LANES), 0)
                      for s in halves]
        heads = range(N_HEADS)
        z = [_dot_nt(q_ref[0, r0:, hl(h)], k_ref[0, pl.ds(start, FX_TK), hl(h)]) for h in heads]
        ps, alphas = [], []
        for h in heads:
            ct = ct_ref[h, r0:, :]
            xs = []
            for s in halves:
                cs = crow_ref[0, h:h + 1, pl.ds(pl.multiple_of(start + s * LANES, LANES), LANES)]
                x = z[h][:, s * LANES:(s + 1) * LANES] + ct - cs
                if off is not None:
                    x = jnp.where(causal[s], x, -jnp.inf)
                xs.append(x)
            m_old = m_ref[h, r0:, :]
            m_new = jnp.maximum(m_old, jnp.max(functools.reduce(jnp.maximum, xs), axis=-1,
                                               keepdims=True))
            m_ref[h, r0:, :] = m_new
            ps.append(jnp.concatenate([jnp.exp(x - m_new).astype(BF16) for x in xs], axis=1))
            alphas.append(jnp.exp(m_old - m_new))
        pv = [_dot(ps[h], v_ref[0, pl.ds(start, FX_TK), hl(h)]) for h in heads]
        for h in heads:
            acc_ref[h, r0:, :] = alphas[h] * acc_ref[h, r0:, :] + pv[h]

    m_ref[...] = jnp.full_like(m_ref, -jnp.inf)
    acc_ref[...] = jnp.zeros_like(acc_ref)
    for h in range(N_HEADS):
        ct_ref[h] = jnp.broadcast_to(ccol_ref[0, :, h:h + 1], (tqb, LANES))
    n_diag = tqb // FX_TK

    def body(j, carry):
        key_tile(j, None)
        return carry

    lax.fori_loop(0, qi * n_diag, body, 0)
    for off in range(n_diag):
        key_tile(qi * n_diag + off, off)
    for h in range(N_HEADS):
        acc = acc_ref[h]
        den = pltpu.roll(acc, HEAD_DIM, axis=1)
        o_ref[0, :, h * HEAD_DIM:(h + 1) * HEAD_DIM] = (acc / den)[:, :HEAD_DIM].astype(o_ref.dtype)


def _fx_attention(qkv, cum_row, bsz, seq):
    qkv3 = qkv.reshape(bsz, seq, 3 * ATT_W)
    cum_col = jnp.transpose(cum_row, (0, 2, 1))
    tqb = min(FX_TQB, seq)
    out = pl.pallas_call(
        _fx_kernel,
        grid=(bsz, seq // tqb),
        in_specs=[
            pl.BlockSpec((1, tqb, ATT_W), lambda b, i: (b, i, 0)),
            pl.BlockSpec((1, seq, ATT_W), lambda b, i: (b, 0, 1)),
            pl.BlockSpec((1, seq, ATT_W), lambda b, i: (b, 0, 2)),
            pl.BlockSpec((1, SUBLANES, seq), lambda b, i: (b, 0, 0)),
            pl.BlockSpec((1, tqb, SUBLANES), lambda b, i: (b, i, 0)),
        ],
        out_specs=pl.BlockSpec((1, tqb, W_MIX), lambda b, i: (b, i, 0)),
        out_shape=jax.ShapeDtypeStruct((bsz, seq, W_MIX), BF16),
        scratch_shapes=[pltpu.VMEM((N_HEADS, tqb, LANES), F32)] * 3,
        compiler_params=_cparams("parallel", "parallel"),
        name="fx_attention",
    )(qkv3, qkv3, qkv3, cum_row, cum_col)
    return out.reshape(bsz * seq, W_MIX)


def _gm_kernel(p_ref, w_ref, b_ref, ones_ref, o_ref):
    hid = jax.nn.gelu(p_ref[...])
    u = hid[:, :W_MIX]
    v = hid[:, W_MIX:]
    inv_n = 1.0 / HEAD_DIM
    mean = _mm_lhs2(v, ones_ref[...], _NN) * inv_n
    vc = v - mean
    var = _mm_lhs2(vc * vc, ones_ref[...], _NN) * inv_n
    vn = (vc * lax.rsqrt(var + LN_EPS)).astype(BF16)
    wshape = (GM_CHUNK, N_HEADS * GM_CHUNK)
    lower = jnp.bitwise_and(_iota2(wshape, 1), GM_CHUNK - 1) <= _iota2(wshape, 0)
    w = jnp.where(lower, w_ref[...], 0.0).astype(BF16)
    chunks = range(p_ref.shape[0] // GM_CHUNK)
    rows = [slice(c * GM_CHUNK, (c + 1) * GM_CHUNK) for c in chunks]
    mixed = [_dot(w, _head_blocks(vn[rows[c]], HEAD_DIM)) for c in chunks]
    for c in chunks:
        o_ref[rows[c], :] = (u[rows[c]] * (mixed[c] + b_ref[...])).astype(o_ref.dtype)


def _gm_mix(p_gm, w_cat, b_full, ones_bd):
    n = p_gm.shape[0]
    tm = min(GM_ROWS, n)
    return pl.pallas_call(
        _gm_kernel,
        grid=(n // tm,),
        in_specs=[
            pl.BlockSpec((tm, 2 * W_MIX), lambda i: (i, 0)),
            pl.BlockSpec((GM_CHUNK, N_HEADS * GM_CHUNK), lambda i: (0, 0)),
            pl.BlockSpec((GM_CHUNK, W_MIX), lambda i: (0, 0)),
            pl.BlockSpec((W_MIX, W_MIX), lambda i: (0, 0)),
        ],
        out_specs=pl.BlockSpec((tm, W_MIX), lambda i: (i, 0)),
        out_shape=jax.ShapeDtypeStruct((n, W_MIX), BF16),
        compiler_params=_cparams("parallel"),
        name="gm_mix",
    )(p_gm, w_cat, b_full, ones_bd)


def _head_sum(x, ones_bd):
    return _mm_lhs2(x, ones_bd, _NN)


def _rw_prep_math(p, prev_row, mu_ref, w0_ref, w2_ref, a0_ref, a2_ref, g2_ref, kk_ref, ka_ref,
                  rk_ref, bd_ref):
    rows = _iota2(p.shape, 0)
    prev = jnp.where(rows == 0, prev_row, pltpu.roll(p, 1, axis=0))
    p = p + (prev - p) * mu_ref[...]
    r = p[:, 0:W_MIX]
    k = p[:, W_MIX:2 * W_MIX]
    v = p[:, 2 * W_MIX:3 * W_MIX]
    o = 3 * W_MIX
    xw = p[:, o:o + RW_DECAY_LORA]
    o += RW_DECAY_LORA
    xa = p[:, o:o + RW_AAA_LORA]
    o += RW_AAA_LORA
    xg = p[:, o:o + RW_GATE_LORA]
    wpre = -(w0_ref[...] + _dot(jnp.tanh(xw).astype(BF16), w2_ref[...]))
    w = -(jnp.maximum(wpre, 0.0) + jnp.log(1.0 + jnp.exp(-jnp.abs(wpre)))) - 0.5
    a = jax.nn.sigmoid(a0_ref[...] + _dot(xa.astype(BF16), a2_ref[...]))
    g = _dot(jax.nn.sigmoid(xg).astype(BF16), g2_ref[...])
    kk = k * kk_ref[...]
    nrm = jnp.maximum(jnp.sqrt(_head_sum(kk * kk, bd_ref[...])), L2_EPS)
    kk = kk / nrm
    k2 = k * (1.0 + (a - 1.0) * ka_ref[...])
    ld = -jnp.exp(w)
    bonus = _head_sum(r * k2 * rk_ref[...], bd_ref[...]) * v
    return r, ld, k2, v, kk, kk * a, bonus, g


_NN = (((1,), (0,)), ((), ()))
_NT = (((1,), (1,)), ((), ()))


def _split2(x):
    hi = x.astype(BF16)
    return hi, (x - hi.astype(F32)).astype(BF16)


def _mm_lhs2(a, b, dims):
    m = a.shape[0]
    hi, lo = _split2(a)
    r = lax.dot_general(jnp.concatenate([hi, lo], axis=0), b, dims, preferred_element_type=F32)
    return r[:m] + r[m:]


def _mm1(a, b, dims):
    return lax.dot_general(a.astype(BF16), b, dims, preferred_element_type=F32)


def _lane_head(shape, width):
    return lax.shift_right_logical(_iota2(shape, 1), width.bit_length() - 1)


def _head_blocks(x, width):
    lh = _lane_head(x.shape, width)
    return jnp.concatenate([jnp.where(lh == h, x, jnp.zeros_like(x)) for h in range(N_HEADS)],
                           axis=0)


def _head_diag(full, width):
    rows = full.shape[0] // N_HEADS
    lh = _lane_head((rows, full.shape[1]), width)
    out = jnp.zeros((rows, full.shape[1]), F32)
    for h in range(N_HEADS):
        out = out + jnp.where(lh == h, full[h * rows:(h + 1) * rows], 0.0)
    return out


def _tn_lhs2(a, b_bf16):
    hi = a.astype(BF16).astype(F32)
    b = b_bf16.astype(F32)
    return _dot_tn(hi, b) + _dot_tn(a - hi, b)


def _rw_intra_kernel(p_ref, prev_ref, mu_ref, w0_ref, w2_ref, a0_ref, a2_ref, g2_ref, kkw_ref,
                     kaw_ref, rkw_ref, bd_ref, w_out, u0_out, y0_out, rt_out, btp_out, mrb_out,
                     g0_out, pc_out, bonus_out, g_out):
    cs = RW_CHUNK
    cw = N_HEADS * cs
    row = _iota2((cs, cw), 0)
    colc = jnp.bitwise_and(_iota2((cs, cw), 1), cs - 1)
    lower = colc <= row
    strict = colc < row
    eye = (colc == row).astype(F32)
    tri3 = (jnp.bitwise_and(_iota2((cs, 3 * cs), 1), cs - 1) <= _iota2((cs, 3 * cs), 0)).astype(BF16)

    prev_row = jnp.where(pl.program_id(1) == 0, 0.0, prev_ref[0, SUBLANES - 1:SUBLANES, :])
    r_all, ld_all, k_all, v_all, kk_all, ba_all, bonus, g = _rw_prep_math(
        p_ref[0], prev_row, mu_ref, w0_ref, w2_ref, a0_ref, a2_ref, g2_ref, kkw_ref, kaw_ref,
        rkw_ref, bd_ref)
    bonus_out[0] = bonus
    g_out[0] = g

    chunks = range(p_ref.shape[1] // cs)
    rows = [slice(c * cs, (c + 1) * cs) for c in chunks]
    each = lambda f: [f(c) for c in chunks]
    hb = lambda x: _head_blocks(x.astype(BF16), HEAD_DIM)

    def running_log_decay(c):
        ld = ld_all[rows[c]]
        l1 = ld.astype(BF16)
        l2, l3 = _split2(ld - l1.astype(F32))
        return _dot(tri3, jnp.concatenate([l1, l2, l3], axis=0))

    cl = each(running_log_decay)
    p_in = each(lambda c: jnp.exp(cl[c]))
    p_inv = each(lambda c: jnp.exp(-cl[c]))
    at = each(lambda c: -kk_all[rows[c]] * jnp.exp(cl[c] - ld_all[rows[c]]))
    bt = each(lambda c: ba_all[rows[c]] * p_inv[c])
    kt = each(lambda c: k_all[rows[c]] * p_inv[c])
    rt = each(lambda c: r_all[rows[c]] * p_in[c])
    pc = each(lambda c: p_in[c][cs - 1:cs, :])

    lhs = each(lambda c: jnp.concatenate([at[c], rt[c]], axis=0))
    ab_mb = each(lambda c: _mm_lhs2(lhs[c], hb(bt[c]), _NT))
    ak_mk = each(lambda c: _mm_lhs2(lhs[c], hb(kt[c]), _NT))
    a_ab = each(lambda c: jnp.where(strict, ab_mb[c][:cs], 0.0))
    m_rb = each(lambda c: jnp.where(lower, ab_mb[c][cs:], 0.0))
    a_ak = each(lambda c: jnp.where(strict, ak_mk[c][:cs], 0.0))
    m_rk = each(lambda c: jnp.where(lower, ak_mk[c][cs:], 0.0))

    inv = each(lambda c: eye + a_ab[c])
    pw = each(lambda c: _mm1(a_ab[c], _head_blocks(a_ab[c].astype(BF16), cs), _NN))
    power = 2
    while power < cs:
        pw_blocks = each(lambda c: _head_blocks(pw[c].astype(BF16), cs))
        if power * 2 < cs:
            both = each(lambda c: _mm1(jnp.concatenate([inv[c], pw[c]], axis=0), pw_blocks[c], _NN))
            inv = each(lambda c: inv[c] + both[c][:cs])
            pw = each(lambda c: both[c][cs:])
        else:
            inv = each(lambda c: inv[c] + _mm1(inv[c], pw_blocks[c], _NN))
        power *= 2

    akv_mkv = each(lambda c: _mm_lhs2(jnp.concatenate([a_ak[c], m_rk[c]], axis=0),
                                      hb(v_all[rows[c]]), _NN))
    wu = each(lambda c: _mm_lhs2(
        inv[c], jnp.concatenate([hb(at[c]), hb(akv_mkv[c][:cs])], axis=1), _NN))
    g0 = each(lambda c: _head_diag(_tn_lhs2(v_all[rows[c]], kt[c].astype(BF16)), HEAD_DIM))
    for c in chunks:
        w_out[0, rows[c], :] = wu[c][:, :W_MIX]
        u0_out[0, rows[c], :] = wu[c][:, W_MIX:]
        y0_out[0, rows[c], :] = akv_mkv[c][cs:]
        rt_out[0, rows[c], :] = rt[c]
        btp_out[0, rows[c], :] = bt[c] * pc[c]
        mrb_out[0, rows[c], :] = m_rb[c]
        g0_out[0, rows[c], :] = g0[c] * pc[c]
        pc_out[0, c * SUBLANES:(c + 1) * SUBLANES, :] = jnp.broadcast_to(pc[c], (SUBLANES, W_MIX))


def _rw_intra(p_rw, bsz, seq, mu, w0, w2, a0, a2, g2, k_k, k_a, r_k, ones_bd):
    p3 = p_rw.reshape(bsz, seq, RW_COLS)
    rows = min(RW_CHUNK * RW_CPS, seq)
    cps = rows // RW_CHUNK
    spec = pl.BlockSpec((1, rows, W_MIX), lambda b, c: (b, c, 0))
    pc_spec = pl.BlockSpec((1, SUBLANES * cps, W_MIX), lambda b, c: (b, c, 0))
    vec = lambda width: pl.BlockSpec((1, width), lambda b, c: (0, 0))
    mat = lambda rws: pl.BlockSpec((rws, W_MIX), lambda b, c: (0, 0))
    big = jax.ShapeDtypeStruct((bsz, seq, W_MIX), F32)
    return pl.pallas_call(
        _rw_intra_kernel,
        grid=(bsz, seq // rows),
        in_specs=[
            pl.BlockSpec((1, rows, RW_COLS), lambda b, c: (b, c, 0)),
            pl.BlockSpec((1, SUBLANES, RW_COLS),
                         lambda b, c: (b, jnp.maximum(c * (rows // SUBLANES) - 1, 0), 0)),
            vec(RW_COLS), vec(W_MIX), mat(RW_DECAY_LORA), vec(W_MIX), mat(RW_AAA_LORA),
            mat(RW_GATE_LORA), vec(W_MIX), vec(W_MIX), vec(W_MIX), mat(W_MIX),
        ],
        out_specs=[spec] * 7 + [pc_spec, spec, spec],
        out_shape=[big] * 7 + [jax.ShapeDtypeStruct((bsz, seq // RW_CHUNK * SUBLANES, W_MIX), F32),
                               big, big],
        compiler_params=_cparams("parallel", "parallel"),
        name="rw_intra",
    )(p3, p3, mu, w0, w2, a0, a2, g2, k_k, k_a, r_k, ones_bd)


def _rw_state_kernel(w_ref, u0_ref, y0_ref, rt_ref, btp_ref, mrb_ref, g0_ref, pc_ref,
                     bonus_ref, g_ref, gnw_ref, gnb_ref, ones_ref, o_ref, state_ref):
    cs = RW_CHUNK

    @pl.when(pl.program_id(1) == 0)
    def _():
        state_ref[...] = jnp.zeros_like(state_ref)

    batch = range(w_ref.shape[0])
    each = lambda f: [f(b) for b in batch]
    hb = lambda x: _head_blocks(x.astype(BF16), HEAD_DIM)
    s0 = each(lambda b: state_ref[b])
    wr = each(lambda b: _mm_lhs2(jnp.concatenate([w_ref[b], rt_ref[b]], axis=0), hb(s0[b]), _NT))
    u = each(lambda b: wr[b][:cs] + u0_ref[b])
    y = each(lambda b: wr[b][cs:] + _mm_lhs2(mrb_ref[b], hb(u[b]), _NN) + y0_ref[b])
    su = each(lambda b: _head_diag(_tn_lhs2(u[b], btp_ref[b].astype(BF16)), HEAD_DIM))
    for b in batch:
        state_ref[b] = s0[b] * pc_ref[b, 0:1, :] + g0_ref[b] + su[b]

    inv_n = 1.0 / HEAD_DIM
    mean = each(lambda b: _mm_lhs2(y[b], ones_ref[...], _NN) * inv_n)
    yc = each(lambda b: y[b] - mean[b])
    var = each(lambda b: _mm_lhs2(yc[b] * yc[b], ones_ref[...], _NN) * inv_n)
    for b in batch:
        yn = yc[b] * lax.rsqrt(var[b] + GN_EPS) * gnw_ref[...] + gnb_ref[...]
        o_ref[b] = ((yn + bonus_ref[b]) * g_ref[b]).astype(o_ref.dtype)


def _rw_state(intra, bonus, g, gn_w, gn_b, ones_bd, bsz, seq):
    gb = RW_BATCH_GROUP if bsz % RW_BATCH_GROUP == 0 else 1
    spec = pl.BlockSpec((gb, RW_CHUNK, W_MIX), lambda b, c: (b, c, 0))
    pc_spec = pl.BlockSpec((gb, SUBLANES, W_MIX), lambda b, c: (b, c, 0))
    vec = pl.BlockSpec((1, W_MIX), lambda b, c: (0, 0))
    out = pl.pallas_call(
        _rw_state_kernel,
        grid=(bsz // gb, seq // RW_CHUNK),
        in_specs=[spec] * 7 + [pc_spec, spec, spec, vec, vec,
                               pl.BlockSpec((W_MIX, W_MIX), lambda b, c: (0, 0))],
        out_specs=spec,
        out_shape=jax.ShapeDtypeStruct((bsz, seq, W_MIX), BF16),
        scratch_shapes=[pltpu.VMEM((gb, HEAD_DIM, W_MIX), F32)],
        compiler_params=_cparams("parallel", "arbitrary"),
        name="rw_state",
    )(*intra, bonus.reshape(bsz, seq, W_MIX), g.reshape(bsz, seq, W_MIX), gn_w, gn_b,
      ones_bd.astype(BF16))
    return out.reshape(bsz * seq, W_MIX)


def _store_token_rows(ref, val):
    tm, d = val.shape
    ch = d // LANES
    for s in range(ch):
        ref[pl.ds(s, tm, stride=ch), :] = val[:, s * LANES:(s + 1) * LANES]


def _load_token_rows(ref, tm, ch):
    return jnp.concatenate([ref[pl.ds(s, tm, stride=ch), :] for s in range(ch)], axis=1)


def _outproj_router_kernel(x_ref, sb_ref, gm_ref, rw_ref, fx_ref, wo_ref, g_ref, wr_ref, br_ref,
                           su_ref, x_out, h_out, idx_out, gate_out, rank_out, cnt_out, cnt_ref):
    mix = jnp.concatenate([sb_ref[...], gm_ref[...], rw_ref[...], fx_ref[...]], axis=1)
    x = x_ref[...] + _dot(mix, wo_ref[...])
    x_out[...] = x
    h = x * lax.rsqrt(jnp.mean(x * x, axis=-1, keepdims=True) + RMS_EPS) * g_ref[...]
    _store_token_rows(h_out, h)

    nr = wr_ref.shape[0]
    h_hi, h_lo = _split2(h)
    w_hi, w_lo = _split2(wr_ref[...])
    part = _dot_nt(jnp.concatenate([w_hi, w_lo], axis=0), h_hi)
    lg = part[:nr] + part[nr:] + _dot_nt(w_hi, h_lo) + br_ref[...]
    tm = lg.shape[1]
    gl = [lg[g:g + 1, :] for g in range(N_EXPERT_GROUPS)]
    gmax = gl[0]
    gsel = jnp.zeros((1, tm), jnp.int32)
    for g in range(1, N_EXPERT_GROUPS):
        better = gl[g] > gmax
        gsel = jnp.where(better, g, gsel)
        gmax = jnp.where(better, gl[g], gmax)
    denom = gl[0] * 0.0
    for g in range(N_EXPERT_GROUPS):
        denom = denom + jnp.exp(gl[g] - gmax)
    g_gate = 1.0 / denom

    e0 = SUBLANES
    ing = lg[e0:e0 + EXPERTS_PER_GROUP, :]
    for g in range(1, N_EXPERT_GROUPS):
        ing = jnp.where(gsel == g, lg[e0 + g * EXPERTS_PER_GROUP:e0 + (g + 1) * EXPERTS_PER_GROUP, :], ing)
    ridx = _iota2(ing.shape, 0)
    m1 = jnp.max(ing, axis=0, keepdims=True)
    i1 = jnp.min(jnp.where(ing == m1, ridx, EXPERTS_PER_GROUP), axis=0, keepdims=True)
    rest = jnp.where(ridx == i1, -jnp.inf, ing)
    m2 = jnp.max(rest, axis=0, keepdims=True)
    i2 = jnp.min(jnp.where(rest == m2, ridx, EXPERTS_PER_GROUP), axis=0, keepdims=True)
    e2 = jnp.exp(m2 - m1)
    s = 1.0 + e2
    e_sel = (gsel * EXPERTS_PER_GROUP + i1, gsel * EXPERTS_PER_GROUP + i2)
    idx_out[0:1, :] = e_sel[0]
    idx_out[1:2, :] = e_sel[1]
    gate_out[0:1, :] = (1.0 / s) * g_gate
    gate_out[1:2, :] = (e2 / s) * g_gate

    @pl.when(pl.program_id(0) == 0)
    def _():
        cnt_ref[...] = jnp.zeros_like(cnt_ref)

    erow = _iota2((N_EXPERTS, tm), 0)
    onehot = [(erow == e).astype(F32) for e in e_sel]
    both = onehot[0] + onehot[1]
    before = cnt_ref[...] + _dot(both.astype(BF16), su_ref[...])
    for k in range(TOP_K):
        rank_out[k:k + 1, :] = jnp.sum(onehot[k] * before, axis=0, keepdims=True).astype(jnp.int32)
    cnt_ref[...] = cnt_ref[...] + jnp.sum(both, axis=1, keepdims=True)
    cnt_out[...] = jnp.broadcast_to(cnt_ref[...], cnt_out.shape).astype(jnp.int32)


def _outproj_router(x2, ys, w_out, g2, w_router_t, b_router):
    n, d = x2.shape
    tm = min(ROW_TILE, n)
    nr = w_router_t.shape[0]
    ymix = pl.BlockSpec((tm, W_MIX), lambda i: (i, 0))
    r = jnp.arange(tm)
    strict_upper = (r[:, None] < r[None, :]).astype(BF16)
    return pl.pallas_call(
        _outproj_router_kernel,
        grid=(n // tm,),
        in_specs=[
            pl.BlockSpec((tm, d), lambda i: (i, 0)),
            ymix, ymix, ymix, ymix,
            pl.BlockSpec((4 * W_MIX, d), lambda i: (0, 0)),
            pl.BlockSpec((1, d), lambda i: (0, 0)),
            pl.BlockSpec((nr, d), lambda i: (0, 0)),
            pl.BlockSpec((nr, 1), lambda i: (0, 0)),
            pl.BlockSpec((tm, tm), lambda i: (0, 0)),
        ],
        out_specs=[
            pl.BlockSpec((tm, d), lambda i: (i, 0)),
            pl.BlockSpec((tm * (d // LANES), LANES), lambda i: (i, 0)),
            pl.BlockSpec((TOP_K, tm), lambda i: (0, i)),
            pl.BlockSpec((TOP_K, tm), lambda i: (0, i)),
            pl.BlockSpec((TOP_K, tm), lambda i: (0, i)),
            pl.BlockSpec((N_EXPERTS, LANES), lambda i: (0, 0)),
        ],
        out_shape=[
            jax.ShapeDtypeStruct((n, d), F32),
            jax.ShapeDtypeStruct((n * (d // LANES), LANES), F32),
            jax.ShapeDtypeStruct((TOP_K, n), jnp.int32),
            jax.ShapeDtypeStruct((TOP_K, n), F32),
            jax.ShapeDtypeStruct((TOP_K, n), jnp.int32),
            jax.ShapeDtypeStruct((N_EXPERTS, LANES), jnp.int32),
        ],
        scratch_shapes=[pltpu.VMEM((N_EXPERTS, 1), F32)],
        compiler_params=_cparams("arbitrary"),
        name="outproj_router",
    )(x2, *ys, w_out, g2, w_router_t, b_router, strict_upper)


def _token_copy(src_hbm, dst, src_tok, dst_tok, sem, ch):
    return pltpu.make_async_copy(
        src_hbm.at[pl.ds(pl.multiple_of(src_tok * ch, ch), ch)],
        dst.at[pl.ds(pl.multiple_of(dst_tok * ch, ch), ch)], sem)


def _dispatch_kernel(pad_end_ref, slot_ref, h_ref, o_hbm, zbuf, zsem, sem, *, ch, tm):
    i = pl.program_id(0)

    def zero_copy(first_slot):
        return pltpu.make_async_copy(
            zbuf, o_hbm.at[pl.ds(pl.multiple_of(first_slot * ch, ch), MOE_TB * ch)], zsem)

    @pl.when(i == 0)
    def _():
        zbuf[...] = jnp.zeros_like(zbuf)
        tails = [jnp.maximum(pad_end_ref[e] - MOE_TB, 0) for e in range(N_EXPERTS)]
        for first in tails:
            zero_copy(first).start()
        for first in tails:
            zero_copy(first).wait()

        def unused_block(b, _):
            zero_copy(b * MOE_TB).start()
            zero_copy(b * MOE_TB).wait()
            return 0

        lax.fori_loop(pad_end_ref[N_EXPERTS - 1] // MOE_TB, o_hbm.shape[0] // (MOE_TB * ch),
                      unused_block, 0)

    def start(r, _):
        for k in range(TOP_K):
            _token_copy(h_ref, o_hbm, r, slot_ref[0, 0, r * TOP_K + k], sem, ch).start(
                priority=k % 2)
        return 0

    lax.fori_loop(0, tm, start, 0, unroll=DMA_ISSUE_UNROLL)
    for k in range(TOP_K):
        pltpu.make_async_copy(h_ref, o_hbm.at[pl.ds(0, tm * ch)], sem).wait()


def _dispatch(pad_end, slot_flat, h_rows, n_slots, ch):
    n_tok = h_rows.shape[0] // ch
    tm = min(DISPATCH_TOKENS, n_tok)
    n_steps = n_tok // tm
    return pl.pallas_call(
        functools.partial(_dispatch_kernel, ch=ch, tm=tm),
        grid_spec=pltpu.PrefetchScalarGridSpec(
            num_scalar_prefetch=1,
            grid=(n_steps,),
            in_specs=[
                pl.BlockSpec((1, 1, tm * TOP_K), lambda i, pe: (i, 0, 0), memory_space=pltpu.SMEM),
                pl.BlockSpec((tm * ch, LANES), lambda i, pe: (i, 0)),
            ],
            out_specs=pl.BlockSpec(memory_space=pl.ANY),
            scratch_shapes=[pltpu.VMEM((MOE_TB * ch, LANES), F32), pltpu.SemaphoreType.DMA,
                            pltpu.SemaphoreType.DMA],
        ),
        out_shape=jax.ShapeDtypeStruct((n_slots * ch, LANES), h_rows.dtype),
        compiler_params=_cparams("arbitrary"),
        name="moe_dispatch",
    )(pad_end, slot_flat.reshape(n_steps, 1, tm * TOP_K), h_rows)


def _expert_kernel(be_ref, nb_ref, x_ref, wg_ref, wu_ref, wd_ref, o_ref, wg_c, wu_c, wd_c, *, ch):
    i = pl.program_id(0)

    @pl.when((i < nb_ref[0]) & ((i == 0) | (be_ref[i] != be_ref[jnp.maximum(i - 1, 0)])))
    def _():
        wg_c[...] = wg_ref[0, 0].astype(BF16)
        wu_c[...] = wu_ref[0, 0].astype(BF16)
        wd_c[...] = wd_ref[0, 0].astype(BF16)

    @pl.when(i < nb_ref[0])
    def _():
        x = _load_token_rows(x_ref, MOE_TB, ch).astype(BF16)
        a = _dot(x, wg_c[...])
        u = _dot(x, wu_c[...])
        hid = (a * jax.nn.sigmoid(a) * u).astype(BF16)
        _store_token_rows(o_ref, _dot(hid, wd_c[...]))

    @pl.when(i >= nb_ref[0])
    def _():
        o_ref[...] = jnp.zeros_like(o_ref)


def _expert_blocks(block_expert, n_used, xb_rows, w_gate, w_up, w_down, layer):
    d, hid = w_gate.shape[2], w_gate.shape[3]
    ch = d // LANES
    n_slots = xb_rows.shape[0] // ch
    return pl.pallas_call(
        functools.partial(_expert_kernel, ch=ch),
        grid_spec=pltpu.PrefetchScalarGridSpec(
            num_scalar_prefetch=2,
            grid=(n_slots // MOE_TB,),
            in_specs=[
                pl.BlockSpec((MOE_TB * ch, LANES), lambda i, be, nb: (i, 0)),
                pl.BlockSpec((1, 1, d, hid), lambda i, be, nb: (layer, be[i], 0, 0)),
                pl.BlockSpec((1, 1, d, hid), lambda i, be, nb: (layer, be[i], 0, 0)),
                pl.BlockSpec((1, 1, hid, d), lambda i, be, nb: (layer, be[i], 0, 0)),
            ],
            out_specs=pl.BlockSpec((MOE_TB * ch, LANES), lambda i, be, nb: (i, 0)),
            scratch_shapes=[pltpu.VMEM((d, hid), BF16), pltpu.VMEM((d, hid), BF16),
                            pltpu.VMEM((hid, d), BF16)],
        ),
        out_shape=jax.ShapeDtypeStruct((n_slots * ch, LANES), F32),
        compiler_params=_cparams("arbitrary"),
        name="moe_experts",
    )(block_expert, n_used, xb_rows, w_gate, w_up, w_down)


def _combine_kernel(slot_ref, slot_next_ref, x_ref, gate_ref, norm_ref, yb_hbm, o_ref, buf, sem,
                    *, ch, final_norm):
    i = pl.program_id(0)
    last = pl.num_programs(0) - 1
    tm = x_ref.shape[0]

    def issue(idx_ref, parity):
        def start(r, _):
            for k in range(TOP_K):
                _token_copy(yb_hbm, buf.at[parity, k], idx_ref[0, 0, r * TOP_K + k], r,
                            sem.at[parity], ch).start(priority=k % 2)
            return 0

        lax.fori_loop(0, tm, start, 0, unroll=DMA_ISSUE_UNROLL)

    @pl.when(i == 0)
    def _():
        issue(slot_ref, 0)

    @pl.when(i < last)
    def _():
        issue(slot_next_ref, (i + 1) % 2)

    par = i % 2
    for k in range(TOP_K):
        pltpu.make_async_copy(yb_hbm.at[pl.ds(0, tm * ch)], buf.at[par, k], sem.at[par]).wait()
    y = (_load_token_rows(buf.at[par, 0], tm, ch) * gate_ref[:, 0:1]
         + _load_token_rows(buf.at[par, 1], tm, ch) * gate_ref[:, 1:2])
    x = x_ref[...] + y
    if final_norm:
        x = x * lax.rsqrt(jnp.mean(x * x, axis=-1, keepdims=True) + RMS_EPS) * norm_ref[...]
    o_ref[...] = x


def _combine(slot_flat, x2, gates, yb_rows, norm_g, final_norm):
    n, d = x2.shape
    ch = d // LANES
    tm = min(256, n)
    n_steps = n // tm
    slots3 = slot_flat.reshape(n_steps, 1, tm * TOP_K)
    return pl.pallas_call(
        functools.partial(_combine_kernel, ch=ch, final_norm=final_norm),
        grid=(n_steps,),
        in_specs=[
            pl.BlockSpec((1, 1, tm * TOP_K), lambda i: (i, 0, 0), memory_space=pltpu.SMEM),
            pl.BlockSpec((1, 1, tm * TOP_K), lambda i: (jnp.minimum(i + 1, n_steps - 1), 0, 0),
                         memory_space=pltpu.SMEM),
            pl.BlockSpec((tm, d), lambda i: (i, 0)),
            pl.BlockSpec((tm, TOP_K), lambda i: (i, 0)),
            pl.BlockSpec((1, d), lambda i: (0, 0)),
            pl.BlockSpec(memory_space=pl.ANY),
        ],
        out_specs=pl.BlockSpec((tm, d), lambda i: (i, 0)),
        scratch_shapes=[pltpu.VMEM((2, TOP_K, tm * ch, LANES), F32),
                        pltpu.SemaphoreType.DMA((2,))],
        out_shape=jax.ShapeDtypeStruct((n, d), F32),
        compiler_params=_cparams("arbitrary"),
        name="moe_combine",
    )(slots3, slots3, x2, gates, norm_g, yb_rows)


def _routing_plan(expert_idx_t, rank_t, counts, n_tok):
    padded = (counts + MOE_TB - 1) // MOE_TB * MOE_TB
    pad_end = jnp.cumsum(padded)
    pad_start = pad_end - padded
    start_t = jnp.sum(jnp.where(expert_idx_t[:, :, None] == jnp.arange(N_EXPERTS), pad_start, 0),
                      axis=-1)
    slot = jnp.transpose(start_t + rank_t).reshape(n_tok * TOP_K)
    n_blocks = -(-(n_tok * TOP_K) // MOE_TB) + N_EXPERTS
    block_expert = jnp.minimum(
        jnp.sum(pad_end[None, :] <= (jnp.arange(n_blocks) * MOE_TB)[:, None], axis=1),
        N_EXPERTS - 1).astype(jnp.int32)
    n_used = (pad_end[-1] // MOE_TB).astype(jnp.int32).reshape(1)
    return slot.astype(jnp.int32), pad_end.astype(jnp.int32), block_expert, n_used, n_blocks * MOE_TB


def _moe(x2, h2, expert_idx_t, gates_t, rank_t, counts, w_gate, w_up, w_down, layer, norm_g,
         final_norm):
    n_tok = x2.shape[0]
    slot, pad_end, block_expert, n_used, n_slots = _routing_plan(expert_idx_t, rank_t, counts, n_tok)
    xb = _dispatch(pad_end, slot, h2, n_slots, x2.shape[1] // LANES)
    yb = _expert_blocks(block_expert, n_used, xb, w_gate, w_up, w_down, layer)
    return _combine(slot, x2, jnp.transpose(gates_t), yb, norm_g, final_norm)


def _prepare_params(w_in, gm_w_s, gm_b, rw_mu, rw_w0, rw_w2, rw_a0, rw_a2, rw_g2, rw_k_k, rw_k_a,
                    rw_r_k, rw_gn_w, rw_gn_b, fx_b_f, w_out, router_group_w, router_group_b,
                    router_expert_w, router_expert_b):
    depth, d = w_in.shape[0], w_in.shape[1]
    n_main = 3 * W_MIX + 2 * W_MIX + RW_COLS + 3 * W_MIX
    t = lambda a: jnp.swapaxes(a, 1, 2)
    pad_rows = lambda a, rows: jnp.concatenate(
        [a, jnp.zeros((depth, rows - a.shape[1]) + a.shape[2:], a.dtype)], axis=1)
    row = lambda a: a.reshape(depth, 1, -1)
    wr = jnp.concatenate([pad_rows(t(router_group_w), SUBLANES), t(router_expert_w)], axis=1)
    br = jnp.concatenate([pad_rows(router_group_b[:, :, None], SUBLANES),
                          router_expert_b[:, :, None]], axis=1)
    return dict(
        w_main=w_in[:, :, :n_main].astype(BF16),
        w_gate_t=pad_rows(t(w_in[:, :, n_main:n_main + N_HEADS]), SUBLANES).astype(BF16),
        fx_bias=pad_rows(fx_b_f[:, :, None], SUBLANES),
        gm_w=jnp.transpose(gm_w_s, (0, 2, 1, 3)).reshape(depth, GM_CHUNK, N_HEADS * GM_CHUNK),
        gm_b=jnp.repeat(t(gm_b), HEAD_DIM, axis=2),
        mu=row(rw_mu), w0=row(rw_w0), w2=rw_w2.astype(BF16), a0=row(rw_a0),
        a2=rw_a2.astype(BF16), g2=rw_g2.astype(BF16), k_k=row(rw_k_k), k_a=row(rw_k_a),
        r_k=row(rw_r_k), gn_w=row(rw_gn_w), gn_b=row(rw_gn_b),
        w_out=w_out.astype(BF16), wr=wr, br=br,
    )


def _layer_params(l, w_in, gm_w_s, gm_b, rw_mu, rw_w0, rw_w2, rw_a0, rw_a2, rw_g2, rw_k_k, rw_k_a,
                  rw_r_k, rw_gn_w, rw_gn_b, fx_b_f, w_out, router_group_w, router_group_b,
                  router_expert_w, router_expert_b):
    allp = _prepare_params(w_in, gm_w_s, gm_b, rw_mu, rw_w0, rw_w2, rw_a0, rw_a2, rw_g2, rw_k_k,
                           rw_k_a, rw_r_k, rw_gn_w, rw_gn_b, fx_b_f, w_out, router_group_w,
                           router_group_b, router_expert_w, router_expert_b)
    return {k: v[l] for k, v in allp.items()}


def kernel(x, norm1_g, w_in, gm_w_s, gm_b, rw_mu, rw_w0, rw_w2, rw_a0, rw_a2, rw_g2, rw_k_k, rw_k_a, rw_r_k, rw_gn_w, rw_gn_b, fx_b_f, w_out, norm2_g, router_group_w, router_group_b, router_expert_w, router_expert_b, exp_w_gate, exp_w_up, exp_w_down, final_norm_g):
    bsz, seq, d = x.shape
    depth = w_in.shape[0]
    assert seq % min(ATT_TQB, seq) == 0 and seq % min(FX_TQB, seq) == 0
    assert seq % RW_CHUNK == 0 and seq % (2 * LANES) == 0
    x2 = x.reshape(bsz * seq, d)
    hd = jnp.arange(W_MIX) // HEAD_DIM
    ones_bd = (hd[:, None] == hd[None, :]).astype(BF16)
    allp = _prepare_params(w_in, gm_w_s, gm_b, rw_mu, rw_w0, rw_w2, rw_a0, rw_a2, rw_g2, rw_k_k,
                           rw_k_a, rw_r_k, rw_gn_w, rw_gn_b, fx_b_f, w_out, router_group_w,
                           router_group_b, router_expert_w, router_expert_b)
    for l in range(depth):
        lp = {k: v[l] for k, v in allp.items()}
        p_sb, p_gm, p_rw, p_fx, grow = _norm_inproj(x2, norm1_g[l].reshape(1, d), lp["w_main"],
                                                    lp["w_gate_t"])
        y_sb = _sb_attention(p_sb, bsz, seq)
        y_gm = _gm_mix(p_gm, lp["gm_w"], lp["gm_b"], ones_bd)
        intra = _rw_intra(p_rw, bsz, seq, lp["mu"], lp["w0"], lp["w2"], lp["a0"], lp["a2"],
                          lp["g2"], lp["k_k"], lp["k_a"], lp["r_k"], ones_bd)
        y_rw = _rw_state(intra[:8], intra[8], intra[9], lp["gn_w"], lp["gn_b"], ones_bd, bsz, seq)
        grow3 = jnp.transpose(grow.reshape(SUBLANES, bsz, seq), (1, 0, 2))
        cum_row = _fx_cum(grow3, lp["fx_bias"])
        y_fx = _fx_attention(p_fx, cum_row, bsz, seq)
        x2, h2, eidx_t, gates_t, rank_t, cnt = _outproj_router(
            x2, (y_sb, y_gm, y_rw, y_fx), lp["w_out"], norm2_g[l].reshape(1, d), lp["wr"], lp["br"])
        x2 = _moe(x2, h2, eidx_t, gates_t, rank_t, cnt[:, 0], exp_w_gate, exp_w_up, exp_w_down, l,
                  final_norm_g.reshape(1, d), l == depth - 1)
    return x2.reshape(bsz, seq, d)
```

```python
import functools

import jax
import jax.numpy as jnp
from jax import lax
from jax.experimental import pallas as pl
from jax.experimental.pallas import tpu as pltpu

F32 = jnp.float32
BF16 = jnp.bfloat16
HIGHEST = lax.Precision.HIGHEST

HEAD_DIM = 64
N_HEADS = 4
W_MIX = N_HEADS * HEAD_DIM
GM_CHUNK = 128
GM_ROWS = 1024
RW_DECAY_LORA = 64
RW_AAA_LORA = 64
RW_GATE_LORA = 128
RW_COLS = 3 * W_MIX + RW_DECAY_LORA + RW_AAA_LORA + RW_GATE_LORA
N_EXPERT_GROUPS = 4
EXPERTS_PER_GROUP = 8
N_EXPERTS = N_EXPERT_GROUPS * EXPERTS_PER_GROUP
TOP_K = 2
RMS_EPS = 1e-6
LN_EPS = 1e-5
GN_EPS = 64e-5
L2_EPS = 1e-12
LOG2_E = 1.4426950408889634

V7X_VMEM_LIMIT_BYTES = 56 * 1024 * 1024
LANES = 128
SUBLANES = 8

ATT_TK = 128
FX_TK = 512
SB_TILES_PER_STEP = 2
ATT_TQB = 512
FX_TQB = 1024
ATT_W = N_HEADS * LANES
RW_CHUNK = 64
RW_CPS = 8
RW_BATCH_GROUP = 16
ROW_TILE = 512
MOE_TB = 512
DMA_ISSUE_UNROLL = 8
DISPATCH_TOKENS = 1024


def _cparams(*sem):
    return pltpu.CompilerParams(dimension_semantics=sem, vmem_limit_bytes=V7X_VMEM_LIMIT_BYTES)


def _dot(a, b, precision=None):
    return jnp.dot(a, b, preferred_element_type=F32, precision=precision)


def _dot_nt(a, b, precision=None):
    return lax.dot_general(a, b, (((1,), (1,)), ((), ())), preferred_element_type=F32,
                           precision=precision)


def _dot_tn(a, b, precision=None):
    return lax.dot_general(a, b, (((0,), (0,)), ((), ())), preferred_element_type=F32,
                           precision=precision)


def _iota2(shape, dim):
    return lax.broadcasted_iota(jnp.int32, shape, dim)


def _store_heads_padded(ref, col0, src, fill):
    low = _iota2((src.shape[0], LANES), 1) < HEAD_DIM
    for pair in range(N_HEADS // 2):
        slab = src[:, pair * LANES:(pair + 1) * LANES]
        swapped = pltpu.roll(slab, HEAD_DIM, axis=1)
        for odd, val in enumerate((slab, swapped)):
            c = col0 + (2 * pair + odd) * LANES
            ref[:, c:c + LANES] = jnp.where(low, val, fill).astype(ref.dtype)


def _norm_inproj_kernel(x_ref, g_ref, w_ref, wgt_ref, sb_ref, gm_ref, rw_ref, fx_ref, grow_ref):
    x = x_ref[...]
    h = x * lax.rsqrt(jnp.mean(x * x, axis=-1, keepdims=True) + RMS_EPS) * g_ref[...]
    hb = h.astype(BF16)
    p = _dot(hb, w_ref[...])
    scale = HEAD_DIM ** -0.5
    wp = N_HEADS * LANES
    o = 0
    _store_heads_padded(sb_ref, 0, p[:, o:o + W_MIX] * scale, 0.0)
    _store_heads_padded(sb_ref, wp, p[:, o + W_MIX:o + 2 * W_MIX], 0.0)
    _store_heads_padded(sb_ref, 2 * wp, p[:, o + 2 * W_MIX:o + 3 * W_MIX], 0.0)
    o += 3 * W_MIX
    gm_ref[...] = p[:, o:o + 2 * W_MIX]
    o += 2 * W_MIX
    rw_ref[...] = p[:, o:o + RW_COLS]
    o += RW_COLS
    _store_heads_padded(fx_ref, 0, p[:, o:o + W_MIX] * scale, 0.0)
    _store_heads_padded(fx_ref, wp, p[:, o + W_MIX:o + 2 * W_MIX], 0.0)
    _store_heads_padded(fx_ref, 2 * wp, p[:, o + 2 * W_MIX:o + 3 * W_MIX], 1.0)
    grow_ref[...] = _dot_nt(wgt_ref[...], hb)


def _norm_inproj(x2, g, w_main, w_gate_t):
    n, d = x2.shape
    tm = min(ROW_TILE, n)
    cols = w_main.shape[1]
    return pl.pallas_call(
        _norm_inproj_kernel,
        grid=(n // tm,),
        in_specs=[
            pl.BlockSpec((tm, d), lambda i: (i, 0)),
            pl.BlockSpec((1, d), lambda i: (0, 0)),
            pl.BlockSpec((d, cols), lambda i: (0, 0)),
            pl.BlockSpec((SUBLANES, d), lambda i: (0, 0)),
        ],
        out_specs=[
            pl.BlockSpec((tm, 3 * ATT_W), lambda i: (i, 0)),
            pl.BlockSpec((tm, 2 * W_MIX), lambda i: (i, 0)),
            pl.BlockSpec((tm, RW_COLS), lambda i: (i, 0)),
            pl.BlockSpec((tm, 3 * ATT_W), lambda i: (i, 0)),
            pl.BlockSpec((SUBLANES, tm), lambda i: (0, i)),
        ],
        out_shape=[
            jax.ShapeDtypeStruct((n, 3 * ATT_W), BF16),
            jax.ShapeDtypeStruct((n, 2 * W_MIX), F32),
            jax.ShapeDtypeStruct((n, RW_COLS), F32),
            jax.ShapeDtypeStruct((n, 3 * ATT_W), BF16),
            jax.ShapeDtypeStruct((SUBLANES, n), F32),
        ],
        compiler_params=_cparams("parallel"),
        name="norm_inproj",
    )(x2, g, w_main, w_gate_t)


def _sb_kernel(q_ref, k_ref, v_ref, tri_ref, o_ref, c_ref, acc_ref):
    qi = pl.program_id(1)
    tqb = q_ref.shape[1]
    hl = lambda h: slice(h * LANES, (h + 1) * LANES)

    def key_tiles(tiles):
        meta, zs, rrs = [], [], []
        for j, off in tiles:
            r0 = 0 if off is None else off * ATT_TK
            rows = tqb - r0
            strict = (None if off is None
                      else _iota2((rows, ATT_TK), 1) < _iota2((rows, ATT_TK), 0))
            meta.append((pl.multiple_of(j * ATT_TK, ATT_TK), r0, rows, strict))
        for start, r0, rows, strict in meta:
            zs.append([_dot_nt(q_ref[0, r0:, hl(h)], k_ref[0, pl.ds(start, ATT_TK), hl(h)])
                       for h in range(N_HEADS)])
        for (start, r0, rows, strict), z4 in zip(meta, zs):
            parts = []
            for z in z4:
                sp = jnp.maximum(z, 0.0) + jnp.log(1.0 + jnp.exp2(jnp.abs(z) * -LOG2_E))
                if strict is not None:
                    sp = jnp.where(strict, sp, 0.0)
                hi = sp.astype(BF16)
                lo = (sp - hi.astype(F32)).astype(BF16)
                parts.append(jnp.concatenate([hi, lo], axis=1))
            rrs.append(_dot(jnp.concatenate(parts, axis=0), tri_ref[...]))
        for (start, r0, rows, strict), z4, rr in zip(meta, zs, rrs):
            ps = []
            for h in range(N_HEADS):
                rh = rr[h * rows:(h + 1) * rows]
                p = jnp.exp(z4[h] - rh[:, :ATT_TK] - c_ref[h, r0:, :])
                if strict is not None:
                    p = jnp.where(strict, p, 0.0)
                ps.append(p.astype(BF16))
                c_ref[h, r0:, :] += rh[:, ATT_TK:]
            pv = [_dot(ps[h], v_ref[0, pl.ds(start, ATT_TK), hl(h)]) for h in range(N_HEADS)]
            for h in range(N_HEADS):
                acc_ref[h, r0:, :] += pv[h]

    c_ref[...] = jnp.zeros_like(c_ref)
    acc_ref[...] = jnp.zeros_like(acc_ref)
    n_diag = tqb // ATT_TK
    group = SB_TILES_PER_STEP if n_diag % SB_TILES_PER_STEP == 0 else 1
    for first in reversed(range(0, n_diag, group)):
        key_tiles([(qi * n_diag + off, off) for off in reversed(range(first, first + group))])

    def body(jj, carry):
        right = qi * n_diag - 1 - jj * group
        key_tiles([(right - g, None) for g in range(group)])
        return carry

    lax.fori_loop(0, qi * n_diag // group, body, 0)
    for h in range(N_HEADS):
        o_ref[0, :, h * HEAD_DIM:(h + 1) * HEAD_DIM] = acc_ref[h, :, :HEAD_DIM].astype(o_ref.dtype)


def _sb_attention(qkv, bsz, seq):
    qkv3 = qkv.reshape(bsz, seq, 3 * ATT_W)
    r = jnp.arange(ATT_TK)
    tri = (r[:, None] >= r[None, :]).astype(BF16)
    blk = jnp.concatenate([tri, jnp.ones((ATT_TK, ATT_TK), BF16)], axis=1)
    tri2 = jnp.concatenate([blk, blk], axis=0)
    tqb = min(ATT_TQB, seq)
    out = pl.pallas_call(
        _sb_kernel,
        grid=(bsz, seq // tqb),
        in_specs=[
            pl.BlockSpec((1, tqb, ATT_W), lambda b, i: (b, i, 0)),
            pl.BlockSpec((1, seq, ATT_W), lambda b, i: (b, 0, 1)),
            pl.BlockSpec((1, seq, ATT_W), lambda b, i: (b, 0, 2)),
            pl.BlockSpec((2 * ATT_TK, 2 * ATT_TK), lambda b, i: (0, 0)),
        ],
        out_specs=pl.BlockSpec((1, tqb, W_MIX), lambda b, i: (b, i, 0)),
        out_shape=jax.ShapeDtypeStruct((bsz, seq, W_MIX), BF16),
        scratch_shapes=[pltpu.VMEM((N_HEADS, tqb, ATT_TK), F32),
                        pltpu.VMEM((N_HEADS, tqb, LANES), F32)],
        compiler_params=_cparams("parallel", "parallel"),
        name="sb_attention",
    )(qkv3, qkv3, qkv3, tri2)
    return out.reshape(bsz * seq, W_MIX)


def _fx_cum_kernel(g_ref, b_ref, cum_ref):
    seq = g_ref.shape[2]
    blk = 2 * LANES
    r = _iota2((blk, blk), 0)
    c = _iota2((blk, blk), 1)
    tri = (r <= c).astype(F32)
    carry = jnp.zeros((SUBLANES, 1), F32)
    for s in range(seq // blk):
        x = g_ref[0, :, s * blk:(s + 1) * blk] + b_ref[...]
        lf = -(jnp.maximum(-x, 0.0) + jnp.log(1.0 + jnp.exp(-jnp.abs(x))))
        cs = _dot(lf, tri, precision=HIGHEST) + carry
        cum_ref[0, :, s * blk:(s + 1) * blk] = cs
        carry = cs[:, blk - 1:blk]


def _fx_cum(grow3, bias_col):
    bsz, _, seq = grow3.shape
    return pl.pallas_call(
        _fx_cum_kernel,
        grid=(bsz,),
        in_specs=[
            pl.BlockSpec((1, SUBLANES, seq), lambda b: (b, 0, 0)),
            pl.BlockSpec((SUBLANES, 1), lambda b: (0, 0)),
        ],
        out_specs=pl.BlockSpec((1, SUBLANES, seq), lambda b: (b, 0, 0)),
        out_shape=jax.ShapeDtypeStruct((bsz, SUBLANES, seq), F32),
        compiler_params=_cparams("parallel"),
        name="fx_cum",
    )(grow3, bias_col)


def _fx_kernel(q_ref, k_ref, v_ref, crow_ref, ccol_ref, o_ref, m_ref, acc_ref, ct_ref):
    qi = pl.program_id(1)
    tqb = q_ref.shape[1]
    hl = lambda h: slice(h * LANES, (h + 1) * LANES)

    def key_tile(j, off):
        r0 = 0 if off is None else off * FX_TK
        rows = tqb - r0
        start = pl.multiple_of(j * FX_TK, FX_TK)
        halves = range(FX_TK // LANES)
        if off is not None:
            causal = [_iota2((rows, LANES), 1) + s * LANES <= _iota2((rows, LANES), 0)
                      for s in halves]
        heads = range(N_HEADS)
        z = [_dot_nt(q_ref[0, r0:, hl(h)], k_ref[0, pl.ds(start, FX_TK), hl(h)]) for h in heads]
        ps, alphas = [], []
        for h in heads:
            ct = ct_ref[h, r0:, :]
            xs = []
            for s in halves:
                cs = crow_ref[0, h:h + 1, pl.ds(pl.multiple_of(start + s * LANES, LANES), LANES)]
                x = z[h][:, s * LANES:(s + 1) * LANES] + ct - cs
                if off is not None:
                    x = jnp.where(causal[s], x, -jnp.inf)
                xs.append(x)
            m_old = m_ref[h, r0:, :]
            m_new = jnp.maximum(m_old, jnp.max(functools.reduce(jnp.maximum, xs), axis=-1,
                                               keepdims=True))
            m_ref[h, r0:, :] = m_new
            ps.append(jnp.concatenate([jnp.exp(x - m_new).astype(BF16) for x in xs], axis=1))
            alphas.append(jnp.exp(m_old - m_new))
        pv = [_dot(ps[h], v_ref[0, pl.ds(start, FX_TK), hl(h)]) for h in heads]
        for h in heads:
            acc_ref[h, r0:, :] = alphas[h] * acc_ref[h, r0:, :] + pv[h]

    m_ref[...] = jnp.full_like(m_ref, -jnp.inf)
    acc_ref[...] = jnp.zeros_like(acc_ref)
    for h in range(N_HEADS):
        ct_ref[h] = jnp.broadcast_to(ccol_ref[0, :, h:h + 1], (tqb, LANES))
    n_diag = tqb // FX_TK

    def body(j, carry):
        key_tile(j, None)
        return carry

    lax.fori_loop(0, qi * n_diag, body, 0)
    for off in range(n_diag):
        key_tile(qi * n_diag + off, off)
    for h in range(N_HEADS):
        acc = acc_ref[h]
        den = pltpu.roll(acc, HEAD_DIM, axis=1)
        o_ref[0, :, h * HEAD_DIM:(h + 1) * HEAD_DIM] = (acc / den)[:, :HEAD_DIM].astype(o_ref.dtype)


def _fx_attention(qkv, cum_row, bsz, seq):
    qkv3 = qkv.reshape(bsz, seq, 3 * ATT_W)
    cum_col = jnp.transpose(cum_row, (0, 2, 1))
    tqb = min(FX_TQB, seq)
    out = pl.pallas_call(
        _fx_kernel,
        grid=(bsz, seq // tqb),
        in_specs=[
            pl.BlockSpec((1, tqb, ATT_W), lambda b, i: (b, i, 0)),
            pl.BlockSpec((1, seq, ATT_W), lambda b, i: (b, 0, 1)),
            pl.BlockSpec((1, seq, ATT_W), lambda b, i: (b, 0, 2)),
            pl.BlockSpec((1, SUBLANES, seq), lambda b, i: (b, 0, 0)),
            pl.BlockSpec((1, tqb, SUBLANES), lambda b, i: (b, i, 0)),
        ],
        out_specs=pl.BlockSpec((1, tqb, W_MIX), lambda b, i: (b, i, 0)),
        out_shape=jax.ShapeDtypeStruct((bsz, seq, W_MIX), BF16),
        scratch_shapes=[pltpu.VMEM((N_HEADS, tqb, LANES), F32)] * 3,
        compiler_params=_cparams("parallel", "parallel"),
        name="fx_attention",
    )(qkv3, qkv3, qkv3, cum_row, cum_col)
    return out.reshape(bsz * seq, W_MIX)


def _gm_kernel(p_ref, w_ref, b_ref, ones_ref, o_ref):
    hid = jax.nn.gelu(p_ref[...])
    u = hid[:, :W_MIX]
    v = hid[:, W_MIX:]
    inv_n = 1.0 / HEAD_DIM
    mean = _mm_lhs2(v, ones_ref[...], _NN) * inv_n
    vc = v - mean
    var = _mm_lhs2(vc * vc, ones_ref[...], _NN) * inv_n
    vn = (vc * lax.rsqrt(var + LN_EPS)).astype(BF16)
    wshape = (GM_CHUNK, N_HEADS * GM_CHUNK)
    lower = jnp.bitwise_and(_iota2(wshape, 1), GM_CHUNK - 1) <= _iota2(wshape, 0)
    w = jnp.where(lower, w_ref[...], 0.0).astype(BF16)
    chunks = range(p_ref.shape[0] // GM_CHUNK)
    rows = [slice(c * GM_CHUNK, (c + 1) * GM_CHUNK) for c in chunks]
    mixed = [_dot(w, _head_blocks(vn[rows[c]], HEAD_DIM)) for c in chunks]
    for c in chunks:
        o_ref[rows[c], :] = (u[rows[c]] * (mixed[c] + b_ref[...])).astype(o_ref.dtype)


def _gm_mix(p_gm, w_cat, b_full, ones_bd):
    n = p_gm.shape[0]
    tm = min(GM_ROWS, n)
    return pl.pallas_call(
        _gm_kernel,
        grid=(n // tm,),
        in_specs=[
            pl.BlockSpec((tm, 2 * W_MIX), lambda i: (i, 0)),
            pl.BlockSpec((GM_CHUNK, N_HEADS * GM_CHUNK), lambda i: (0, 0)),
            pl.BlockSpec((GM_CHUNK, W_MIX), lambda i: (0, 0)),
            pl.BlockSpec((W_MIX, W_MIX), lambda i: (0, 0)),
        ],
        out_specs=pl.BlockSpec((tm, W_MIX), lambda i: (i, 0)),
        out_shape=jax.ShapeDtypeStruct((n, W_MIX), BF16),
        compiler_params=_cparams("parallel"),
        name="gm_mix",
    )(p_gm, w_cat, b_full, ones_bd)


def _head_sum(x, ones_bd):
    return _mm_lhs2(x, ones_bd, _NN)


def _rw_prep_math(p, prev_row, mu_ref, w0_ref, w2_ref, a0_ref, a2_ref, g2_ref, kk_ref, ka_ref,
                  rk_ref, bd_ref):
    rows = _iota2(p.shape, 0)
    prev = jnp.where(rows == 0, prev_row, pltpu.roll(p, 1, axis=0))
    p = p + (prev - p) * mu_ref[...]
    r = p[:, 0:W_MIX]
    k = p[:, W_MIX:2 * W_MIX]
    v = p[:, 2 * W_MIX:3 * W_MIX]
    o = 3 * W_MIX
    xw = p[:, o:o + RW_DECAY_LORA]
    o += RW_DECAY_LORA
    xa = p[:, o:o + RW_AAA_LORA]
    o += RW_AAA_LORA
    xg = p[:, o:o + RW_GATE_LORA]
    wpre = -(w0_ref[...] + _dot(jnp.tanh(xw).astype(BF16), w2_ref[...]))
    w = -(jnp.maximum(wpre, 0.0) + jnp.log(1.0 + jnp.exp(-jnp.abs(wpre)))) - 0.5
    a = jax.nn.sigmoid(a0_ref[...] + _dot(xa.astype(BF16), a2_ref[...]))
    g = _dot(jax.nn.sigmoid(xg).astype(BF16), g2_ref[...])
    kk = k * kk_ref[...]
    nrm = jnp.maximum(jnp.sqrt(_head_sum(kk * kk, bd_ref[...])), L2_EPS)
    kk = kk / nrm
    k2 = k * (1.0 + (a - 1.0) * ka_ref[...])
    ld = -jnp.exp(w)
    bonus = _head_sum(r * k2 * rk_ref[...], bd_ref[...]) * v
    return r, ld, k2, v, kk, kk * a, bonus, g


_NN = (((1,), (0,)), ((), ()))
_NT = (((1,), (1,)), ((), ()))


def _split2(x):
    hi = x.astype(BF16)
    return hi, (x - hi.astype(F32)).astype(BF16)


def _mm_lhs2(a, b, dims):
    m = a.shape[0]
    hi, lo = _split2(a)
    r = lax.dot_general(jnp.concatenate([hi, lo], axis=0), b, dims, preferred_element_type=F32)
    return r[:m] + r[m:]


def _mm1(a, b, dims):
    return lax.dot_general(a.astype(BF16), b, dims, preferred_element_type=F32)


def _lane_head(shape, width):
    return lax.shift_right_logical(_iota2(shape, 1), width.bit_length() - 1)


def _head_blocks(x, width):
    lh = _lane_head(x.shape, width)
    return jnp.concatenate([jnp.where(lh == h, x, jnp.zeros_like(x)) for h in range(N_HEADS)],
                           axis=0)


def _head_diag(full, width):
    rows = full.shape[0] // N_HEADS
    lh = _lane_head((rows, full.shape[1]), width)
    out = jnp.zeros((rows, full.shape[1]), F32)
    for h in range(N_HEADS):
        out = out + jnp.where(lh == h, full[h * rows:(h + 1) * rows], 0.0)
    return out


def _tn_lhs2(a, b_bf16):
    hi = a.astype(BF16).astype(F32)
    b = b_bf16.astype(F32)
    return _dot_tn(hi, b) + _dot_tn(a - hi, b)


def _rw_intra_kernel(p_ref, prev_ref, mu_ref, w0_ref, w2_ref, a0_ref, a2_ref, g2_ref, kkw_ref,
                     kaw_ref, rkw_ref, bd_ref, w_out, u0_out, y0_out, rt_out, btp_out, mrb_out,
                     g0_out, pc_out, bonus_out, g_out):
    cs = RW_CHUNK
    cw = N_HEADS * cs
    row = _iota2((cs, cw), 0)
    colc = jnp.bitwise_and(_iota2((cs, cw), 1), cs - 1)
    lower = colc <= row
    strict = colc < row
    eye = (colc == row).astype(F32)
    tri3 = (jnp.bitwise_and(_iota2((cs, 3 * cs), 1), cs - 1) <= _iota2((cs, 3 * cs), 0)).astype(BF16)

    prev_row = jnp.where(pl.program_id(1) == 0, 0.0, prev_ref[0, SUBLANES - 1:SUBLANES, :])
    r_all, ld_all, k_all, v_all, kk_all, ba_all, bonus, g = _rw_prep_math(
        p_ref[0], prev_row, mu_ref, w0_ref, w2_ref, a0_ref, a2_ref, g2_ref, kkw_ref, kaw_ref,
        rkw_ref, bd_ref)
    bonus_out[0] = bonus
    g_out[0] = g

    chunks = range(p_ref.shape[1] // cs)
    rows = [slice(c * cs, (c + 1) * cs) for c in chunks]
    each = lambda f: [f(c) for c in chunks]
    hb = lambda x: _head_blocks(x.astype(BF16), HEAD_DIM)

    def running_log_decay(c):
        ld = ld_all[rows[c]]
        l1 = ld.astype(BF16)
        l2, l3 = _split2(ld - l1.astype(F32))
        return _dot(tri3, jnp.concatenate([l1, l2, l3], axis=0))

    cl = each(running_log_decay)
    p_in = each(lambda c: jnp.exp(cl[c]))
    p_inv = each(lambda c: jnp.exp(-cl[c]))
    at = each(lambda c: -kk_all[rows[c]] * jnp.exp(cl[c] - ld_all[rows[c]]))
    bt = each(lambda c: ba_all[rows[c]] * p_inv[c])
    kt = each(lambda c: k_all[rows[c]] * p_inv[c])
    rt = each(lambda c: r_all[rows[c]] * p_in[c])
    pc = each(lambda c: p_in[c][cs - 1:cs, :])

    lhs = each(lambda c: jnp.concatenate([at[c], rt[c]], axis=0))
    ab_mb = each(lambda c: _mm_lhs2(lhs[c], hb(bt[c]), _NT))
    ak_mk = each(lambda c: _mm_lhs2(lhs[c], hb(kt[c]), _NT))
    a_ab = each(lambda c: jnp.where(strict, ab_mb[c][:cs], 0.0))
    m_rb = each(lambda c: jnp.where(lower, ab_mb[c][cs:], 0.0))
    a_ak = each(lambda c: jnp.where(strict, ak_mk[c][:cs], 0.0))
    m_rk = each(lambda c: jnp.where(lower, ak_mk[c][cs:], 0.0))

    inv = each(lambda c: eye + a_ab[c])
    pw = each(lambda c: _mm1(a_ab[c], _head_blocks(a_ab[c].astype(BF16), cs), _NN))
    power = 2
    while power < cs:
        pw_blocks = each(lambda c: _head_blocks(pw[c].astype(BF16), cs))
        if power * 2 < cs:
            both = each(lambda c: _mm1(jnp.concatenate([inv[c], pw[c]], axis=0), pw_blocks[c], _NN))
            inv = each(lambda c: inv[c] + both[c][:cs])
            pw = each(lambda c: both[c][cs:])
        else:
            inv = each(lambda c: inv[c] + _mm1(inv[c], pw_blocks[c], _NN))
        power *= 2

    akv_mkv = each(lambda c: _mm_lhs2(jnp.concatenate([a_ak[c], m_rk[c]], axis=0),
                                      hb(v_all[rows[c]]), _NN))
    wu = each(lambda c: _mm_lhs2(
        inv[c], jnp.concatenate([hb(at[c]), hb(akv_mkv[c][:cs])], axis=1), _NN))
    g0 = each(lambda c: _head_diag(_tn_lhs2(v_all[rows[c]], kt[c].astype(BF16)), HEAD_DIM))
    for c in chunks:
        w_out[0, rows[c], :] = wu[c][:, :W_MIX]
        u0_out[0, rows[c], :] = wu[c][:, W_MIX:]
        y0_out[0, rows[c], :] = akv_mkv[c][cs:]
        rt_out[0, rows[c], :] = rt[c]
        btp_out[0, rows[c], :] = bt[c] * pc[c]
        mrb_out[0, rows[c], :] = m_rb[c]
        g0_out[0, rows[c], :] = g0[c] * pc[c]
        pc_out[0, c * SUBLANES:(c + 1) * SUBLANES, :] = jnp.broadcast_to(pc[c], (SUBLANES, W_MIX))


def _rw_intra(p_rw, bsz, seq, mu, w0, w2, a0, a2, g2, k_k, k_a, r_k, ones_bd):
    p3 = p_rw.reshape(bsz, seq, RW_COLS)
    rows = min(RW_CHUNK * RW_CPS, seq)
    cps = rows // RW_CHUNK
    spec = pl.BlockSpec((1, rows, W_MIX), lambda b, c: (b, c, 0))
    pc_spec = pl.BlockSpec((1, SUBLANES * cps, W_MIX), lambda b, c: (b, c, 0))
    vec = lambda width: pl.BlockSpec((1, width), lambda b, c: (0, 0))
    mat = lambda rws: pl.BlockSpec((rws, W_MIX), lambda b, c: (0, 0))
    big = jax.ShapeDtypeStruct((bsz, seq, W_MIX), F32)
    return pl.pallas_call(
        _rw_intra_kernel,
        grid=(bsz, seq // rows),
        in_specs=[
            pl.BlockSpec((1, rows, RW_COLS), lambda b, c: (b, c, 0)),
            pl.BlockSpec((1, SUBLANES, RW_COLS),
                         lambda b, c: (b, jnp.maximum(c * (rows // SUBLANES) - 1, 0), 0)),
            vec(RW_COLS), vec(W_MIX), mat(RW_DECAY_LORA), vec(W_MIX), mat(RW_AAA_LORA),
            mat(RW_GATE_LORA), vec(W_MIX), vec(W_MIX), vec(W_MIX), mat(W_MIX),
        ],
        out_specs=[spec] * 7 + [pc_spec, spec, spec],
        out_shape=[big] * 7 + [jax.ShapeDtypeStruct((bsz, seq // RW_CHUNK * SUBLANES, W_MIX), F32),
                               big, big],
        compiler_params=_cparams("parallel", "parallel"),
        name="rw_intra",
    )(p3, p3, mu, w0, w2, a0, a2, g2, k_k, k_a, r_k, ones_bd)


def _rw_state_kernel(w_ref, u0_ref, y0_ref, rt_ref, btp_ref, mrb_ref, g0_ref, pc_ref,
                     bonus_ref, g_ref, gnw_ref, gnb_ref, ones_ref, o_ref, state_ref):
    cs = RW_CHUNK

    @pl.when(pl.program_id(1) == 0)
    def _():
        state_ref[...] = jnp.zeros_like(state_ref)

    batch = range(w_ref.shape[0])
    each = lambda f: [f(b) for b in batch]
    hb = lambda x: _head_blocks(x.astype(BF16), HEAD_DIM)
    s0 = each(lambda b: state_ref[b])
    wr = each(lambda b: _mm_lhs2(jnp.concatenate([w_ref[b], rt_ref[b]], axis=0), hb(s0[b]), _NT))
    u = each(lambda b: wr[b][:cs] + u0_ref[b])
    y = each(lambda b: wr[b][cs:] + _mm_lhs2(mrb_ref[b], hb(u[b]), _NN) + y0_ref[b])
    su = each(lambda b: _head_diag(_tn_lhs2(u[b], btp_ref[b].astype(BF16)), HEAD_DIM))
    for b in batch:
        state_ref[b] = s0[b] * pc_ref[b, 0:1, :] + g0_ref[b] + su[b]

    inv_n = 1.0 / HEAD_DIM
    mean = each(lambda b: _mm_lhs2(y[b], ones_ref[...], _NN) * inv_n)
    yc = each(lambda b: y[b] - mean[b])
    var = each(lambda b: _mm_lhs2(yc[b] * yc[b], ones_ref[...], _NN) * inv_n)
    for b in batch:
        yn = yc[b] * lax.rsqrt(var[b] + GN_EPS) * gnw_ref[...] + gnb_ref[...]
        o_ref[b] = ((yn + bonus_ref[b]) * g_ref[b]).astype(o_ref.dtype)


def _rw_state(intra, bonus, g, gn_w, gn_b, ones_bd, bsz, seq):
    gb = RW_BATCH_GROUP if bsz % RW_BATCH_GROUP == 0 else 1
    spec = pl.BlockSpec((gb, RW_CHUNK, W_MIX), lambda b, c: (b, c, 0))
    pc_spec = pl.BlockSpec((gb, SUBLANES, W_MIX), lambda b, c: (b, c, 0))
    vec = pl.BlockSpec((1, W_MIX), lambda b, c: (0, 0))
    out = pl.pallas_call(
        _rw_state_kernel,
        grid=(bsz // gb, seq // RW_CHUNK),
        in_specs=[spec] * 7 + [pc_spec, spec, spec, vec, vec,
                               pl.BlockSpec((W_MIX, W_MIX), lambda b, c: (0, 0))],
        out_specs=spec,
        out_shape=jax.ShapeDtypeStruct((bsz, seq, W_MIX), BF16),
        scratch_shapes=[pltpu.VMEM((gb, HEAD_DIM, W_MIX), F32)],
        compiler_params=_cparams("parallel", "arbitrary"),
        name="rw_state",
    )(*intra, bonus.reshape(bsz, seq, W_MIX), g.reshape(bsz, seq, W_MIX), gn_w, gn_b,
      ones_bd.astype(BF16))
    return out.reshape(bsz * seq, W_MIX)


def _store_token_rows(ref, val):
    tm, d = val.shape
    ch = d // LANES
    for s in range(ch):
        ref[pl.ds(s, tm, stride=ch), :] = val[:, s * LANES:(s + 1) * LANES]


def _load_token_rows(ref, tm, ch):
    return jnp.concatenate([ref[pl.ds(s, tm, stride=ch), :] for s in range(ch)], axis=1)


def _outproj_router_kernel(x_ref, sb_ref, gm_ref, rw_ref, fx_ref, wo_ref, g_ref, wr_ref, br_ref,
                           su_ref, x_out, h_out, idx_out, gate_out, rank_out, cnt_out, cnt_ref):
    mix = jnp.concatenate([sb_ref[...], gm_ref[...], rw_ref[...], fx_ref[...]], axis=1)
    x = x_ref[...] + _dot(mix, wo_ref[...])
    x_out[...] = x
    h = x * lax.rsqrt(jnp.mean(x * x, axis=-1, keepdims=True) + RMS_EPS) * g_ref[...]
    _store_token_rows(h_out, h)

    nr = wr_ref.shape[0]
    h_hi, h_lo = _split2(h)
    w_hi, w_lo = _split2(wr_ref[...])
    part = _dot_nt(jnp.concatenate([w_hi, w_lo], axis=0), h_hi)
    lg = part[:nr] + part[nr:] + _dot_nt(w_hi, h_lo) + br_ref[...]
    tm = lg.shape[1]
    gl = [lg[g:g + 1, :] for g in range(N_EXPERT_GROUPS)]
    gmax = gl[0]
    gsel = jnp.zeros((1, tm), jnp.int32)
    for g in range(1, N_EXPERT_GROUPS):
        better = gl[g] > gmax
        gsel = jnp.where(better, g, gsel)
        gmax = jnp.where(better, gl[g], gmax)
    denom = gl[0] * 0.0
    for g in range(N_EXPERT_GROUPS):
        denom = denom + jnp.exp(gl[g] - gmax)
    g_gate = 1.0 / denom

    e0 = SUBLANES
    ing = lg[e0:e0 + EXPERTS_PER_GROUP, :]
    for g in range(1, N_EXPERT_GROUPS):
        ing = jnp.where(gsel == g, lg[e0 + g * EXPERTS_PER_GROUP:e0 + (g + 1) * EXPERTS_PER_GROUP, :], ing)
    ridx = _iota2(ing.shape, 0)
    m1 = jnp.max(ing, axis=0, keepdims=True)
    i1 = jnp.min(jnp.where(ing == m1, ridx, EXPERTS_PER_GROUP), axis=0, keepdims=True)
    rest = jnp.where(ridx == i1, -jnp.inf, ing)
    m2 = jnp.max(rest, axis=0, keepdims=True)
    i2 = jnp.min(jnp.where(rest == m2, ridx, EXPERTS_PER_GROUP), axis=0, keepdims=True)
    e2 = jnp.exp(m2 - m1)
    s = 1.0 + e2
    e_sel = (gsel * EXPERTS_PER_GROUP + i1, gsel * EXPERTS_PER_GROUP + i2)
    idx_out[0:1, :] = e_sel[0]
    idx_out[1:2, :] = e_sel[1]
    gate_out[0:1, :] = (1.0 / s) * g_gate
    gate_out[1:2, :] = (e2 / s) * g_gate

    @pl.when(pl.program_id(0) == 0)
    def _():
        cnt_ref[...] = jnp.zeros_like(cnt_ref)

    erow = _iota2((N_EXPERTS, tm), 0)
    onehot = [(erow == e).astype(F32) for e in e_sel]
    both = onehot[0] + onehot[1]
    before = cnt_ref[...] + _dot(both.astype(BF16), su_ref[...])
    for k in range(TOP_K):
        rank_out[k:k + 1, :] = jnp.sum(onehot[k] * before, axis=0, keepdims=True).astype(jnp.int32)
    cnt_ref[...] = cnt_ref[...] + jnp.sum(both, axis=1, keepdims=True)
    cnt_out[...] = jnp.broadcast_to(cnt_ref[...], cnt_out.shape).astype(jnp.int32)


def _outproj_router(x2, ys, w_out, g2, w_router_t, b_router):
    n, d = x2.shape
    tm = min(ROW_TILE, n)
    nr = w_router_t.shape[0]
    ymix = pl.BlockSpec((tm, W_MIX), lambda i: (i, 0))
    r = jnp.arange(tm)
    strict_upper = (r[:, None] < r[None, :]).astype(BF16)
    return pl.pallas_call(
        _outproj_router_kernel,
        grid=(n // tm,),
        in_specs=[
            pl.BlockSpec((tm, d), lambda i: (i, 0)),
            ymix, ymix, ymix, ymix,
            pl.BlockSpec((4 * W_MIX, d), lambda i: (0, 0)),
            pl.BlockSpec((1, d), lambda i: (0, 0)),
            pl.BlockSpec((nr, d), lambda i: (0, 0)),
            pl.BlockSpec((nr, 1), lambda i: (0, 0)),
            pl.BlockSpec((tm, tm), lambda i: (0, 0)),
        ],
        out_specs=[
            pl.BlockSpec((tm, d), lambda i: (i, 0)),
            pl.BlockSpec((tm * (d // LANES), LANES), lambda i: (i, 0)),
            pl.BlockSpec((TOP_K, tm), lambda i: (0, i)),
            pl.BlockSpec((TOP_K, tm), lambda i: (0, i)),
            pl.BlockSpec((TOP_K, tm), lambda i: (0, i)),
            pl.BlockSpec((N_EXPERTS, LANES), lambda i: (0, 0)),
        ],
        out_shape=[
            jax.ShapeDtypeStruct((n, d), F32),
            jax.ShapeDtypeStruct((n * (d // LANES), LANES), F32),
            jax.ShapeDtypeStruct((TOP_K, n), jnp.int32),
            jax.ShapeDtypeStruct((TOP_K, n), F32),
            jax.ShapeDtypeStruct((TOP_K, n), jnp.int32),
            jax.ShapeDtypeStruct((N_EXPERTS, LANES), jnp.int32),
        ],
        scratch_shapes=[pltpu.VMEM((N_EXPERTS, 1), F32)],
        compiler_params=_cparams("arbitrary"),
        name="outproj_router",
    )(x2, *ys, w_out, g2, w_router_t, b_router, strict_upper)


def _token_copy(src_hbm, dst, src_tok, dst_tok, sem, ch):
    return pltpu.make_async_copy(
        src_hbm.at[pl.ds(pl.multiple_of(src_tok * ch, ch), ch)],
        dst.at[pl.ds(pl.multiple_of(dst_tok * ch, ch), ch)], sem)


def _dispatch_kernel(pad_end_ref, slot_ref, h_ref, o_hbm, zbuf, zsem, sem, *, ch, tm):
    i = pl.program_id(0)

    def zero_copy(first_slot):
        return pltpu.make_async_copy(
            zbuf, o_hbm.at[pl.ds(pl.multiple_of(first_slot * ch, ch), MOE_TB * ch)], zsem)

    @pl.when(i == 0)
    def _():
        zbuf[...] = jnp.zeros_like(zbuf)
        tails = [jnp.maximum(pad_end_ref[e] - MOE_TB, 0) for e in range(N_EXPERTS)]
        for first in tails:
            zero_copy(first).start()
        for first in tails:
            zero_copy(first).wait()

        def unused_block(b, _):
            zero_copy(b * MOE_TB).start()
            zero_copy(b * MOE_TB).wait()
            return 0

        lax.fori_loop(pad_end_ref[N_EXPERTS - 1] // MOE_TB, o_hbm.shape[0] // (MOE_TB * ch),
                      unused_block, 0)

    def start(r, _):
        for k in range(TOP_K):
            _token_copy(h_ref, o_hbm, r, slot_ref[0, 0, r * TOP_K + k], sem, ch).start(
                priority=k % 2)
        return 0

    lax.fori_loop(0, tm, start, 0, unroll=DMA_ISSUE_UNROLL)
    for k in range(TOP_K):
        pltpu.make_async_copy(h_ref, o_hbm.at[pl.ds(0, tm * ch)], sem).wait()


def _dispatch(pad_end, slot_flat, h_rows, n_slots, ch):
    n_tok = h_rows.shape[0] // ch
    tm = min(DISPATCH_TOKENS, n_tok)
    n_steps = n_tok // tm
    return pl.pallas_call(
        functools.partial(_dispatch_kernel, ch=ch, tm=tm),
        grid_spec=pltpu.PrefetchScalarGridSpec(
            num_scalar_prefetch=1,
            grid=(n_steps,),
            in_specs=[
                pl.BlockSpec((1, 1, tm * TOP_K), lambda i, pe: (i, 0, 0), memory_space=pltpu.SMEM),
                pl.BlockSpec((tm * ch, LANES), lambda i, pe: (i, 0)),
            ],
            out_specs=pl.BlockSpec(memory_space=pl.ANY),
            scratch_shapes=[pltpu.VMEM((MOE_TB * ch, LANES), F32), pltpu.SemaphoreType.DMA,
                            pltpu.SemaphoreType.DMA],
        ),
        out_shape=jax.ShapeDtypeStruct((n_slots * ch, LANES), h_rows.dtype),
        compiler_params=_cparams("arbitrary"),
        name="moe_dispatch",
    )(pad_end, slot_flat.reshape(n_steps, 1, tm * TOP_K), h_rows)


def _expert_kernel(be_ref, nb_ref, x_ref, wg_ref, wu_ref, wd_ref, o_ref, wg_c, wu_c, wd_c, *, ch):
    i = pl.program_id(0)

    @pl.when((i < nb_ref[0]) & ((i == 0) | (be_ref[i] != be_ref[jnp.maximum(i - 1, 0)])))
    def _():
        wg_c[...] = wg_ref[0, 0].astype(BF16)
        wu_c[...] = wu_ref[0, 0].astype(BF16)
        wd_c[...] = wd_ref[0, 0].astype(BF16)

    @pl.when(i < nb_ref[0])
    def _():
        x = _load_token_rows(x_ref, MOE_TB, ch).astype(BF16)
        a = _dot(x, wg_c[...])
        u = _dot(x, wu_c[...])
        hid = (a * jax.nn.sigmoid(a) * u).astype(BF16)
        _store_token_rows(o_ref, _dot(hid, wd_c[...]))

    @pl.when(i >= nb_ref[0])
    def _():
        o_ref[...] = jnp.zeros_like(o_ref)


def _expert_blocks(block_expert, n_used, xb_rows, w_gate, w_up, w_down, layer):
    d, hid = w_gate.shape[2], w_gate.shape[3]
    ch = d // LANES
    n_slots = xb_rows.shape[0] // ch
    return pl.pallas_call(
        functools.partial(_expert_kernel, ch=ch),
        grid_spec=pltpu.PrefetchScalarGridSpec(
            num_scalar_prefetch=2,
            grid=(n_slots // MOE_TB,),
            in_specs=[
                pl.BlockSpec((MOE_TB * ch, LANES), lambda i, be, nb: (i, 0)),
                pl.BlockSpec((1, 1, d, hid), lambda i, be, nb: (layer, be[i], 0, 0)),
                pl.BlockSpec((1, 1, d, hid), lambda i, be, nb: (layer, be[i], 0, 0)),
                pl.BlockSpec((1, 1, hid, d), lambda i, be, nb: (layer, be[i], 0, 0)),
            ],
            out_specs=pl.BlockSpec((MOE_TB * ch, LANES), lambda i, be, nb: (i, 0)),
            scratch_shapes=[pltpu.VMEM((d, hid), BF16), pltpu.VMEM((d, hid), BF16),
                            pltpu.VMEM((hid, d), BF16)],
        ),
        out_shape=jax.ShapeDtypeStruct((n_slots * ch, LANES), F32),
        compiler_params=_cparams("arbitrary"),
        name="moe_experts",
    )(block_expert, n_used, xb_rows, w_gate, w_up, w_down)


def _combine_kernel(slot_ref, slot_next_ref, x_ref, gate_ref, norm_ref, yb_hbm, o_ref, buf, sem,
                    *, ch, final_norm):
    i = pl.program_id(0)
    last = pl.num_programs(0) - 1
    tm = x_ref.shape[0]

    def issue(idx_ref, parity):
        def start(r, _):
            for k in range(TOP_K):
                _token_copy(yb_hbm, buf.at[parity, k], idx_ref[0, 0, r * TOP_K + k], r,
                            sem.at[parity], ch).start(priority=k % 2)
            return 0

        lax.fori_loop(0, tm, start, 0, unroll=DMA_ISSUE_UNROLL)

    @pl.when(i == 0)
    def _():
        issue(slot_ref, 0)

    @pl.when(i < last)
    def _():
        issue(slot_next_ref, (i + 1) % 2)

    par = i % 2
    for k in range(TOP_K):
        pltpu.make_async_copy(yb_hbm.at[pl.ds(0, tm * ch)], buf.at[par, k], sem.at[par]).wait()
    y = (_load_token_rows(buf.at[par, 0], tm, ch) * gate_ref[:, 0:1]
         + _load_token_rows(buf.at[par, 1], tm, ch) * gate_ref[:, 1:2])
    x = x_ref[...] + y
    if final_norm:
        x = x * lax.rsqrt(jnp.mean(x * x, axis=-1, keepdims=True) + RMS_EPS) * norm_ref[...]
    o_ref[...] = x


def _combine(slot_flat, x2, gates, yb_rows, norm_g, final_norm):
    n, d = x2.shape
    ch = d // LANES
    tm = min(256, n)
    n_steps = n // tm
    slots3 = slot_flat.reshape(n_steps, 1, tm * TOP_K)
    return pl.pallas_call(
        functools.partial(_combine_kernel, ch=ch, final_norm=final_norm),
        grid=(n_steps,),
        in_specs=[
            pl.BlockSpec((1, 1, tm * TOP_K), lambda i: (i, 0, 0), memory_space=pltpu.SMEM),
            pl.BlockSpec((1, 1, tm * TOP_K), lambda i: (jnp.minimum(i + 1, n_steps - 1), 0, 0),
                         memory_space=pltpu.SMEM),
            pl.BlockSpec((tm, d), lambda i: (i, 0)),
            pl.BlockSpec((tm, TOP_K), lambda i: (i, 0)),
            pl.BlockSpec((1, d), lambda i: (0, 0)),
            pl.BlockSpec(memory_space=pl.ANY),
        ],
        out_specs=pl.BlockSpec((tm, d), lambda i: (i, 0)),
        scratch_shapes=[pltpu.VMEM((2, TOP_K, tm * ch, LANES), F32),
                        pltpu.SemaphoreType.DMA((2,))],
        out_shape=jax.ShapeDtypeStruct((n, d), F32),
        compiler_params=_cparams("arbitrary"),
        name="moe_combine",
    )(slots3, slots3, x2, gates, norm_g, yb_rows)


def _routing_plan(expert_idx_t, rank_t, counts, n_tok):
    padded = (counts + MOE_TB - 1) // MOE_TB * MOE_TB
    pad_end = jnp.cumsum(padded)
    pad_start = pad_end - padded
    start_t = jnp.sum(jnp.where(expert_idx_t[:, :, None] == jnp.arange(N_EXPERTS), pad_start, 0),
                      axis=-1)
    slot = jnp.transpose(start_t + rank_t).reshape(n_tok * TOP_K)
    n_blocks = -(-(n_tok * TOP_K) // MOE_TB) + N_EXPERTS
    block_expert = jnp.minimum(
        jnp.sum(pad_end[None, :] <= (jnp.arange(n_blocks) * MOE_TB)[:, None], axis=1),
        N_EXPERTS - 1).astype(jnp.int32)
    n_used = (pad_end[-1] // MOE_TB).astype(jnp.int32).reshape(1)
    return slot.astype(jnp.int32), pad_end.astype(jnp.int32), block_expert, n_used, n_blocks * MOE_TB


def _moe(x2, h2, expert_idx_t, gates_t, rank_t, counts, w_gate, w_up, w_down, layer, norm_g,
         final_norm):
    n_tok = x2.shape[0]
    slot, pad_end, block_expert, n_used, n_slots = _routing_plan(expert_idx_t, rank_t, counts, n_tok)
    xb = _dispatch(pad_end, slot, h2, n_slots, x2.shape[1] // LANES)
    yb = _expert_blocks(block_expert, n_used, xb, w_gate, w_up, w_down, layer)
    return _combine(slot, x2, jnp.transpose(gates_t), yb, norm_g, final_norm)


def _prepare_params(w_in, gm_w_s, gm_b, rw_mu, rw_w0, rw_w2, rw_a0, rw_a2, rw_g2, rw_k_k, rw_k_a,
                    rw_r_k, rw_gn_w, rw_gn_b, fx_b_f, w_out, router_group_w, router_group_b,
                    router_expert_w, router_expert_b):
    depth, d = w_in.shape[0], w_in.shape[1]
    n_main = 3 * W_MIX + 2 * W_MIX + RW_COLS + 3 * W_MIX
    t = lambda a: jnp.swapaxes(a, 1, 2)
    pad_rows = lambda a, rows: jnp.concatenate(
        [a, jnp.zeros((depth, rows - a.shape[1]) + a.shape[2:], a.dtype)], axis=1)
    row = lambda a: a.reshape(depth, 1, -1)
    wr = jnp.concatenate([pad_rows(t(router_group_w), SUBLANES), t(router_expert_w)], axis=1)
    br = jnp.concatenate([pad_rows(router_group_b[:, :, None], SUBLANES),
                          router_expert_b[:, :, None]], axis=1)
    return dict(
        w_main=w_in[:, :, :n_main].astype(BF16),
        w_gate_t=pad_rows(t(w_in[:, :, n_main:n_main + N_HEADS]), SUBLANES).astype(BF16),
        fx_bias=pad_rows(fx_b_f[:, :, None], SUBLANES),
        gm_w=jnp.transpose(gm_w_s, (0, 2, 1, 3)).reshape(depth, GM_CHUNK, N_HEADS * GM_CHUNK),
        gm_b=jnp.repeat(t(gm_b), HEAD_DIM, axis=2),
        mu=row(rw_mu), w0=row(rw_w0), w2=rw_w2.astype(BF16), a0=row(rw_a0),
        a2=rw_a2.astype(BF16), g2=rw_g2.astype(BF16), k_k=row(rw_k_k), k_a=row(rw_k_a),
        r_k=row(rw_r_k), gn_w=row(rw_gn_w), gn_b=row(rw_gn_b),
        w_out=w_out.astype(BF16), wr=wr, br=br,
    )


def kernel(x, norm1_g, w_in, gm_w_s, gm_b, rw_mu, rw_w0, rw_w2, rw_a0, rw_a2, rw_g2, rw_k_k, rw_k_a, rw_r_k, rw_gn_w, rw_gn_b, fx_b_f, w_out, norm2_g, router_group_w, router_group_b, router_expert_w, router_expert_b, exp_w_gate, exp_w_up, exp_w_down, final_norm_g):
    bsz, seq, d = x.shape
    depth = w_in.shape[0]
    assert seq % min(ATT_TQB, seq) == 0 and seq % min(FX_TQB, seq) == 0
    assert seq % RW_CHUNK == 0 and seq % (2 * LANES) == 0
    x2 = x.reshape(bsz * seq, d)
    hd = jnp.arange(W_MIX) // HEAD_DIM
    ones_bd = (hd[:, None] == hd[None, :]).astype(BF16)
    allp = _prepare_params(w_in, gm_w_s, gm_b, rw_mu, rw_w0, rw_w2, rw_a0, rw_a2, rw_g2, rw_k_k,
                           rw_k_a, rw_r_k, rw_gn_w, rw_gn_b, fx_b_f, w_out, router_group_w,
                           router_group_b, router_expert_w, router_expert_b)
    for l in range(depth):
        lp = {k: v[l] for k, v in allp.items()}
        p_sb, p_gm, p_rw, p_fx, grow = _norm_inproj(x2, norm1_g[l].reshape(1, d), lp["w_main"],
                                                    lp["w_gate_t"])
        y_sb = _sb_attention(p_sb, bsz, seq)
        y_gm = _gm_mix(p_gm, lp["gm_w"], lp["gm_b"], ones_bd)
        intra = _rw_intra(p_rw, bsz, seq, lp["mu"], lp["w0"], lp["w2"], lp["a0"], lp["a2"],
                          lp["g2"], lp["k_k"], lp["k_a"], lp["r_k"], ones_bd)
        y_rw = _rw_state(intra[:8], intra[8], intra[9], lp["gn_w"], lp["gn_b"], ones_bd, bsz, seq)
        grow3 = jnp.transpose(grow.reshape(SUBLANES, bsz, seq), (1, 0, 2))
        cum_row = _fx_cum(grow3, lp["fx_bias"])
        y_fx = _fx_attention(p_fx, cum_row, bsz, seq)
        x2, h2, eidx_t, gates_t, rank_t, cnt = _outproj_router(
            x2, (y_sb, y_gm, y_rw, y_fx), lp["w_out"], norm2_g[l].reshape(1, d), lp["wr"], lp["br"])
        x2 = _moe(x2, h2, eidx_t, gates_t, rank_t, cnt[:, 0], exp_w_gate, exp_w_up, exp_w_down, l,
                  final_norm_g.reshape(1, d), l == depth - 1)
    return x2.reshape(bsz, seq, d)
```

```python
import functools

import jax
import jax.numpy as jnp
from jax import lax
from jax.experimental import pallas as pl
from jax.experimental.pallas import tpu as pltpu

F32 = jnp.float32
BF16 = jnp.bfloat16
HIGHEST = lax.Precision.HIGHEST

HEAD_DIM = 64
N_HEADS = 4
W_MIX = N_HEADS * HEAD_DIM
GM_CHUNK = 128
GM_ROWS = 2048
RW_DECAY_LORA = 64
RW_AAA_LORA = 64
RW_GATE_LORA = 128
RW_COLS = 3 * W_MIX + RW_DECAY_LORA + RW_AAA_LORA + RW_GATE_LORA
N_EXPERT_GROUPS = 4
EXPERTS_PER_GROUP = 8
N_EXPERTS = N_EXPERT_GROUPS * EXPERTS_PER_GROUP
TOP_K = 2
RMS_EPS = 1e-6
LN_EPS = 1e-5
GN_EPS = 64e-5
L2_EPS = 1e-12
LOG2_E = 1.4426950408889634

V7X_VMEM_LIMIT_BYTES = 56 * 1024 * 1024
LANES = 128
SUBLANES = 8

ATT_TK = 128
FX_TK = 512
SB_TILES_PER_STEP = 2
ATT_TQB = 512
FX_TQB = 1024
ATT_W = N_HEADS * LANES
RW_CHUNK = 64
RW_CPS = 8
RW_BATCH_GROUP = 16
ROW_TILE = 512
MOE_TB = 512
DMA_ISSUE_UNROLL = 8
DISPATCH_TOKENS = 2048
COMBINE_TOKENS = 512


def _cparams(*sem):
    return pltpu.CompilerParams(dimension_semantics=sem, vmem_limit_bytes=V7X_VMEM_LIMIT_BYTES)


def _dot(a, b, precision=None):
    return jnp.dot(a, b, preferred_element_type=F32, precision=precision)


def _dot_nt(a, b, precision=None):
    return lax.dot_general(a, b, (((1,), (1,)), ((), ())), preferred_element_type=F32,
                           precision=precision)


def _dot_tn(a, b, precision=None):
    return lax.dot_general(a, b, (((0,), (0,)), ((), ())), preferred_element_type=F32,
                           precision=precision)


def _iota2(shape, dim):
    return lax.broadcasted_iota(jnp.int32, shape, dim)


def _store_heads_padded(ref, col0, src, fill):
    low = _iota2((src.shape[0], LANES), 1) < HEAD_DIM
    for pair in range(N_HEADS // 2):
        slab = src[:, pair * LANES:(pair + 1) * LANES]
        swapped = pltpu.roll(slab, HEAD_DIM, axis=1)
        for odd, val in enumerate((slab, swapped)):
            c = col0 + (2 * pair + odd) * LANES
            ref[:, c:c + LANES] = jnp.where(low, val, fill).astype(ref.dtype)


def _norm_inproj_kernel(x_ref, g_ref, w_ref, wgt_ref, sb_ref, gm_ref, rw_ref, fx_ref, grow_ref):
    x = x_ref[...]
    h = x * lax.rsqrt(jnp.mean(x * x, axis=-1, keepdims=True) + RMS_EPS) * g_ref[...]
    hb = h.astype(BF16)
    p = _dot(hb, w_ref[...])
    scale = HEAD_DIM ** -0.5
    wp = N_HEADS * LANES
    o = 0
    _store_heads_padded(sb_ref, 0, p[:, o:o + W_MIX] * scale, 0.0)
    _store_heads_padded(sb_ref, wp, p[:, o + W_MIX:o + 2 * W_MIX], 0.0)
    _store_heads_padded(sb_ref, 2 * wp, p[:, o + 2 * W_MIX:o + 3 * W_MIX], 0.0)
    o += 3 * W_MIX
    gm_ref[...] = p[:, o:o + 2 * W_MIX]
    o += 2 * W_MIX
    rw_ref[...] = p[:, o:o + RW_COLS]
    o += RW_COLS
    _store_heads_padded(fx_ref, 0, p[:, o:o + W_MIX] * scale, 0.0)
    _store_heads_padded(fx_ref, wp, p[:, o + W_MIX:o + 2 * W_MIX], 0.0)
    _store_heads_padded(fx_ref, 2 * wp, p[:, o + 2 * W_MIX:o + 3 * W_MIX], 1.0)
    grow_ref[...] = _dot_nt(wgt_ref[...], hb)


def _norm_inproj(x2, g, w_main, w_gate_t):
    n, d = x2.shape
    tm = min(ROW_TILE, n)
    cols = w_main.shape[1]
    return pl.pallas_call(
        _norm_inproj_kernel,
        grid=(n // tm,),
        in_specs=[
            pl.BlockSpec((tm, d), lambda i: (i, 0)),
            pl.BlockSpec((1, d), lambda i: (0, 0)),
            pl.BlockSpec((d, cols), lambda i: (0, 0)),
            pl.BlockSpec((SUBLANES, d), lambda i: (0, 0)),
        ],
        out_specs=[
            pl.BlockSpec((tm, 3 * ATT_W), lambda i: (i, 0)),
            pl.BlockSpec((tm, 2 * W_MIX), lambda i: (i, 0)),
            pl.BlockSpec((tm, RW_COLS), lambda i: (i, 0)),
            pl.BlockSpec((tm, 3 * ATT_W), lambda i: (i, 0)),
            pl.BlockSpec((SUBLANES, tm), lambda i: (0, i)),
        ],
        out_shape=[
            jax.ShapeDtypeStruct((n, 3 * ATT_W), BF16),
            jax.ShapeDtypeStruct((n, 2 * W_MIX), F32),
            jax.ShapeDtypeStruct((n, RW_COLS), F32),
            jax.ShapeDtypeStruct((n, 3 * ATT_W), BF16),
            jax.ShapeDtypeStruct((SUBLANES, n), F32),
        ],
        compiler_params=_cparams("parallel"),
        name="norm_inproj",
    )(x2, g, w_main, w_gate_t)


def _sb_kernel(q_ref, k_ref, v_ref, tri_ref, o_ref, c_ref, acc_ref):
    qi = pl.program_id(1)
    tqb = q_ref.shape[1]
    hl = lambda h: slice(h * LANES, (h + 1) * LANES)

    def key_tiles(tiles):
        meta, zs, rrs = [], [], []
        for j, off in tiles:
            r0 = 0 if off is None else off * ATT_TK
            rows = tqb - r0
            strict = (None if off is None
                      else _iota2((rows, ATT_TK), 1) < _iota2((rows, ATT_TK), 0))
            meta.append((pl.multiple_of(j * ATT_TK, ATT_TK), r0, rows, strict))
        for start, r0, rows, strict in meta:
            zs.append([_dot_nt(q_ref[0, r0:, hl(h)], k_ref[0, pl.ds(start, ATT_TK), hl(h)])
                       for h in range(N_HEADS)])
        for (start, r0, rows, strict), z4 in zip(meta, zs):
            parts = []
            for z in z4:
                sp = jnp.maximum(z, 0.0) + jnp.log(1.0 + jnp.exp2(jnp.abs(z) * -LOG2_E))
                if strict is not None:
                    sp = jnp.where(strict, sp, 0.0)
                hi = sp.astype(BF16)
                lo = (sp - hi.astype(F32)).astype(BF16)
                parts.append(jnp.concatenate([hi, lo], axis=1))
            rrs.append(_dot(jnp.concatenate(parts, axis=0), tri_ref[...]))
        for (start, r0, rows, strict), z4, rr in zip(meta, zs, rrs):
            ps = []
            for h in range(N_HEADS):
                rh = rr[h * rows:(h + 1) * rows]
                p = jnp.exp(z4[h] - rh[:, :ATT_TK] - c_ref[h, r0:, :])
                if strict is not None:
                    p = jnp.where(strict, p, 0.0)
                ps.append(p.astype(BF16))
                c_ref[h, r0:, :] += rh[:, ATT_TK:]
            pv = [_dot(ps[h], v_ref[0, pl.ds(start, ATT_TK), hl(h)]) for h in range(N_HEADS)]
            for h in range(N_HEADS):
                acc_ref[h, r0:, :] += pv[h]

    c_ref[...] = jnp.zeros_like(c_ref)
    acc_ref[...] = jnp.zeros_like(acc_ref)
    n_diag = tqb // ATT_TK
    group = SB_TILES_PER_STEP if n_diag % SB_TILES_PER_STEP == 0 else 1
    for first in reversed(range(0, n_diag, group)):
        key_tiles([(qi * n_diag + off, off) for off in reversed(range(first, first + group))])

    def body(jj, carry):
        right = qi * n_diag - 1 - jj * group
        key_tiles([(right - g, None) for g in range(group)])
        return carry

    lax.fori_loop(0, qi * n_diag // group, body, 0)
    for h in range(N_HEADS):
        o_ref[0, :, h * HEAD_DIM:(h + 1) * HEAD_DIM] = acc_ref[h, :, :HEAD_DIM].astype(o_ref.dtype)


def _sb_attention(qkv, bsz, seq):
    qkv3 = qkv.reshape(bsz, seq, 3 * ATT_W)
    r = jnp.arange(ATT_TK)
    tri = (r[:, None] >= r[None, :]).astype(BF16)
    blk = jnp.concatenate([tri, jnp.ones((ATT_TK, ATT_TK), BF16)], axis=1)
    tri2 = jnp.concatenate([blk, blk], axis=0)
    tqb = min(ATT_TQB, seq)
    out = pl.pallas_call(
        _sb_kernel,
        grid=(bsz, seq // tqb),
        in_specs=[
            pl.BlockSpec((1, tqb, ATT_W), lambda b, i: (b, i, 0)),
            pl.BlockSpec((1, seq, ATT_W), lambda b, i: (b, 0, 1)),
            pl.BlockSpec((1, seq, ATT_W), lambda b, i: (b, 0, 2)),
            pl.BlockSpec((2 * ATT_TK, 2 * ATT_TK), lambda b, i: (0, 0)),
        ],
        out_specs=pl.BlockSpec((1, tqb, W_MIX), lambda b, i: (b, i, 0)),
        out_shape=jax.ShapeDtypeStruct((bsz, seq, W_MIX), BF16),
        scratch_shapes=[pltpu.VMEM((N_HEADS, tqb, ATT_TK), F32),
                        pltpu.VMEM((N_HEADS, tqb, LANES), F32)],
        compiler_params=_cparams("parallel", "parallel"),
        name="sb_attention",
    )(qkv3, qkv3, qkv3, tri2)
    return out.reshape(bsz * seq, W_MIX)


def _fx_cum_kernel(g_ref, b_ref, cum_ref):
    seq = g_ref.shape[2]
    blk = 2 * LANES
    r = _iota2((blk, blk), 0)
    c = _iota2((blk, blk), 1)
    tri = (r <= c).astype(F32)
    carry = jnp.zeros((SUBLANES, 1), F32)
    for s in range(seq // blk):
        x = g_ref[0, :, s * blk:(s + 1) * blk] + b_ref[...]
        lf = -(jnp.maximum(-x, 0.0) + jnp.log(1.0 + jnp.exp(-jnp.abs(x))))
        cs = _dot(lf, tri, precision=HIGHEST) + carry
        cum_ref[0, :, s * blk:(s + 1) * blk] = cs
        carry = cs[:, blk - 1:blk]


def _fx_cum(grow3, bias_col):
    bsz, _, seq = grow3.shape
    return pl.pallas_call(
        _fx_cum_kernel,
        grid=(bsz,),
        in_specs=[
            pl.BlockSpec((1, SUBLANES, seq), lambda b: (b, 0, 0)),
            pl.BlockSpec((SUBLANES, 1), lambda b: (0, 0)),
        ],
        out_specs=pl.BlockSpec((1, SUBLANES, seq), lambda b: (b, 0, 0)),
        out_shape=jax.ShapeDtypeStruct((bsz, SUBLANES, seq), F32),
        compiler_params=_cparams("parallel"),
        name="fx_cum",
    )(grow3, bias_col)


def _fx_kernel(q_ref, k_ref, v_ref, crow_ref, ccol_ref, o_ref, m_ref, acc_ref, ct_ref):
    qi = pl.program_id(1)
    tqb = q_ref.shape[1]
    hl = lambda h: slice(h * LANES, (h + 1) * LANES)

    def key_tile(j, off):
        r0 = 0 if off is None else off * FX_TK
        rows = tqb - r0
        start = pl.multiple_of(j * FX_TK, FX_TK)
        halves = range(FX_TK // LANES)
        if off is not None:
            causal = [_iota2((rows, LANES), 1) + s * LANES <= _iota2((rows, LANES), 0)
                      for s in halves]
        heads = range(N_HEADS)
        z = [_dot_nt(q_ref[0, r0:, hl(h)], k_ref[0, pl.ds(start, FX_TK), hl(h)]) for h in heads]
        ps, alphas = [], []
        for h in heads:
            ct = ct_ref[h, r0:, :]
            xs = []
            for s in halves:
                cs = crow_ref[0, h:h + 1, pl.ds(pl.multiple_of(start + s * LANES, LANES), LANES)]
                x = z[h][:, s * LANES:(s + 1) * LANES] + ct - cs
                if off is not None:
                    x = jnp.where(causal[s], x, -jnp.inf)
                xs.append(x)
            m_old = m_ref[h, r0:, :]
            m_new = jnp.maximum(m_old, jnp.max(functools.reduce(jnp.maximum, xs), axis=-1,
                                               keepdims=True))
            m_ref[h, r0:, :] = m_new
            ps.append(jnp.concatenate([jnp.exp(x - m_new).astype(BF16) for x in xs], axis=1))
            alphas.append(jnp.exp(m_old - m_new))
        pv = [_dot(ps[h], v_ref[0, pl.ds(start, FX_TK), hl(h)]) for h in heads]
        for h in heads:
            acc_ref[h, r0:, :] = alphas[h] * acc_ref[h, r0:, :] + pv[h]

    m_ref[...] = jnp.full_like(m_ref, -jnp.inf)
    acc_ref[...] = jnp.zeros_like(acc_ref)
    for h in range(N_HEADS):
        ct_ref[h] = jnp.broadcast_to(ccol_ref[0, :, h:h + 1], (tqb, LANES))
    n_diag = tqb // FX_TK

    def body(j, carry):
        key_tile(j, None)
        return carry

    lax.fori_loop(0, qi * n_diag, body, 0)
    for off in range(n_diag):
        key_tile(qi * n_diag + off, off)
    for h in range(N_HEADS):
        acc = acc_ref[h]
        den = pltpu.roll(acc, HEAD_DIM, axis=1)
        o_ref[0, :, h * HEAD_DIM:(h + 1) * HEAD_DIM] = (acc / den)[:, :HEAD_DIM].astype(o_ref.dtype)


def _fx_attention(qkv, cum_row, bsz, seq):
    qkv3 = qkv.reshape(bsz, seq, 3 * ATT_W)
    cum_col = jnp.transpose(cum_row, (0, 2, 1))
    tqb = min(FX_TQB, seq)
    out = pl.pallas_call(
        _fx_kernel,
        grid=(bsz, seq // tqb),
        in_specs=[
            pl.BlockSpec((1, tqb, ATT_W), lambda b, i: (b, i, 0)),
            pl.BlockSpec((1, seq, ATT_W), lambda b, i: (b, 0, 1)),
            pl.BlockSpec((1, seq, ATT_W), lambda b, i: (b, 0, 2)),
            pl.BlockSpec((1, SUBLANES, seq), lambda b, i: (b, 0, 0)),
            pl.BlockSpec((1, tqb, SUBLANES), lambda b, i: (b, i, 0)),
        ],
        out_specs=pl.BlockSpec((1, tqb, W_MIX), lambda b, i: (b, i, 0)),
        out_shape=jax.ShapeDtypeStruct((bsz, seq, W_MIX), BF16),
        scratch_shapes=[pltpu.VMEM((N_HEADS, tqb, LANES), F32)] * 3,
        compiler_params=_cparams("parallel", "parallel"),
        name="fx_attention",
    )(qkv3, qkv3, qkv3, cum_row, cum_col)
    return out.reshape(bsz * seq, W_MIX)


def _gm_kernel(p_ref, w_ref, b_ref, ones_ref, o_ref):
    hid = jax.nn.gelu(p_ref[...])
    u = hid[:, :W_MIX]
    v = hid[:, W_MIX:]
    inv_n = 1.0 / HEAD_DIM
    mean = _mm_lhs2(v, ones_ref[...], _NN) * inv_n
    vc = v - mean
    var = _mm_lhs2(vc * vc, ones_ref[...], _NN) * inv_n
    vn = (vc * lax.rsqrt(var + LN_EPS)).astype(BF16)
    wshape = (GM_CHUNK, N_HEADS * GM_CHUNK)
    lower = jnp.bitwise_and(_iota2(wshape, 1), GM_CHUNK - 1) <= _iota2(wshape, 0)
    w = jnp.where(lower, w_ref[...], 0.0).astype(BF16)
    chunks = range(p_ref.shape[0] // GM_CHUNK)
    rows = [slice(c * GM_CHUNK, (c + 1) * GM_CHUNK) for c in chunks]
    mixed = [_dot(w, _head_blocks(vn[rows[c]], HEAD_DIM)) for c in chunks]
    for c in chunks:
        o_ref[rows[c], :] = (u[rows[c]] * (mixed[c] + b_ref[...])).astype(o_ref.dtype)


def _gm_mix(p_gm, w_cat, b_full, ones_bd):
    n = p_gm.shape[0]
    tm = min(GM_ROWS, n)
    return pl.pallas_call(
        _gm_kernel,
        grid=(n // tm,),
        in_specs=[
            pl.BlockSpec((tm, 2 * W_MIX), lambda i: (i, 0)),
            pl.BlockSpec((GM_CHUNK, N_HEADS * GM_CHUNK), lambda i: (0, 0)),
            pl.BlockSpec((GM_CHUNK, W_MIX), lambda i: (0, 0)),
            pl.BlockSpec((W_MIX, W_MIX), lambda i: (0, 0)),
        ],
        out_specs=pl.BlockSpec((tm, W_MIX), lambda i: (i, 0)),
        out_shape=jax.ShapeDtypeStruct((n, W_MIX), BF16),
        compiler_params=_cparams("parallel"),
        name="gm_mix",
    )(p_gm, w_cat, b_full, ones_bd)


def _head_sum(x, ones_bd):
    return _mm_lhs2(x, ones_bd, _NN)


def _rw_prep_math(p, prev_row, mu_ref, w0_ref, w2_ref, a0_ref, a2_ref, g2_ref, kk_ref, ka_ref,
                  rk_ref, bd_ref):
    rows = _iota2(p.shape, 0)
    prev = jnp.where(rows == 0, prev_row, pltpu.roll(p, 1, axis=0))
    p = p + (prev - p) * mu_ref[...]
    r = p[:, 0:W_MIX]
    k = p[:, W_MIX:2 * W_MIX]
    v = p[:, 2 * W_MIX:3 * W_MIX]
    o = 3 * W_MIX
    xw = p[:, o:o + RW_DECAY_LORA]
    o += RW_DECAY_LORA
    xa = p[:, o:o + RW_AAA_LORA]
    o += RW_AAA_LORA
    xg = p[:, o:o + RW_GATE_LORA]
    wpre = -(w0_ref[...] + _dot(jnp.tanh(xw).astype(BF16), w2_ref[...]))
    w = -(jnp.maximum(wpre, 0.0) + jnp.log(1.0 + jnp.exp(-jnp.abs(wpre)))) - 0.5
    a = jax.nn.sigmoid(a0_ref[...] + _dot(xa.astype(BF16), a2_ref[...]))
    g = _dot(jax.nn.sigmoid(xg).astype(BF16), g2_ref[...])
    kk = k * kk_ref[...]
    nrm = jnp.maximum(jnp.sqrt(_head_sum(kk * kk, bd_ref[...])), L2_EPS)
    kk = kk / nrm
    k2 = k * (1.0 + (a - 1.0) * ka_ref[...])
    ld = -jnp.exp(w)
    bonus = _head_sum(r * k2 * rk_ref[...], bd_ref[...]) * v
    return r, ld, k2, v, kk, kk * a, bonus, g


_NN = (((1,), (0,)), ((), ()))
_NT = (((1,), (1,)), ((), ()))


def _split2(x):
    hi = x.astype(BF16)
    return hi, (x - hi.astype(F32)).astype(BF16)


def _mm_lhs2(a, b, dims):
    m = a.shape[0]
    hi, lo = _split2(a)
    r = lax.dot_general(jnp.concatenate([hi, lo], axis=0), b, dims, preferred_element_type=F32)
    return r[:m] + r[m:]


def _mm1(a, b, dims):
    return lax.dot_general(a.astype(BF16), b, dims, preferred_element_type=F32)


def _lane_head(shape, width):
    return lax.shift_right_logical(_iota2(shape, 1), width.bit_length() - 1)


def _head_blocks(x, width):
    lh = _lane_head(x.shape, width)
    return jnp.concatenate([jnp.where(lh == h, x, jnp.zeros_like(x)) for h in range(N_HEADS)],
                           axis=0)


def _head_diag(full, width):
    rows = full.shape[0] // N_HEADS
    lh = _lane_head((rows, full.shape[1]), width)
    out = jnp.zeros((rows, full.shape[1]), F32)
    for h in range(N_HEADS):
        out = out + jnp.where(lh == h, full[h * rows:(h + 1) * rows], 0.0)
    return out


def _tn_lhs2(a, b_bf16):
    hi = a.astype(BF16).astype(F32)
    b = b_bf16.astype(F32)
    return _dot_tn(hi, b) + _dot_tn(a - hi, b)


def _rw_intra_kernel(p_ref, prev_ref, mu_ref, w0_ref, w2_ref, a0_ref, a2_ref, g2_ref, kkw_ref,
                     kaw_ref, rkw_ref, bd_ref, w_out, u0_out, y0_out, rt_out, btp_out, mrb_out,
                     g0_out, pc_out, bonus_out, g_out):
    cs = RW_CHUNK
    cw = N_HEADS * cs
    row = _iota2((cs, cw), 0)
    colc = jnp.bitwise_and(_iota2((cs, cw), 1), cs - 1)
    lower = colc <= row
    strict = colc < row
    eye = (colc == row).astype(F32)
    tri3 = (jnp.bitwise_and(_iota2((cs, 3 * cs), 1), cs - 1) <= _iota2((cs, 3 * cs), 0)).astype(BF16)

    prev_row = jnp.where(pl.program_id(1) == 0, 0.0, prev_ref[0, SUBLANES - 1:SUBLANES, :])
    r_all, ld_all, k_all, v_all, kk_all, ba_all, bonus, g = _rw_prep_math(
        p_ref[0], prev_row, mu_ref, w0_ref, w2_ref, a0_ref, a2_ref, g2_ref, kkw_ref, kaw_ref,
        rkw_ref, bd_ref)
    bonus_out[0] = bonus
    g_out[0] = g

    chunks = range(p_ref.shape[1] // cs)
    rows = [slice(c * cs, (c + 1) * cs) for c in chunks]
    each = lambda f: [f(c) for c in chunks]
    hb = lambda x: _head_blocks(x.astype(BF16), HEAD_DIM)

    def running_log_decay(c):
        ld = ld_all[rows[c]]
        l1 = ld.astype(BF16)
        l2, l3 = _split2(ld - l1.astype(F32))
        return _dot(tri3, jnp.concatenate([l1, l2, l3], axis=0))

    cl = each(running_log_decay)
    p_in = each(lambda c: jnp.exp(cl[c]))
    p_inv = each(lambda c: jnp.exp(-cl[c]))
    at = each(lambda c: -kk_all[rows[c]] * jnp.exp(cl[c] - ld_all[rows[c]]))
    bt = each(lambda c: ba_all[rows[c]] * p_inv[c])
    kt = each(lambda c: k_all[rows[c]] * p_inv[c])
    rt = each(lambda c: r_all[rows[c]] * p_in[c])
    pc = each(lambda c: p_in[c][cs - 1:cs, :])

    lhs = each(lambda c: jnp.concatenate([at[c], rt[c]], axis=0))
    ab_mb = each(lambda c: _mm_lhs2(lhs[c], hb(bt[c]), _NT))
    ak_mk = each(lambda c: _mm_lhs2(lhs[c], hb(kt[c]), _NT))
    a_ab = each(lambda c: jnp.where(strict, ab_mb[c][:cs], 0.0))
    m_rb = each(lambda c: jnp.where(lower, ab_mb[c][cs:], 0.0))
    a_ak = each(lambda c: jnp.where(strict, ak_mk[c][:cs], 0.0))
    m_rk = each(lambda c: jnp.where(lower, ak_mk[c][cs:], 0.0))

    inv = each(lambda c: eye + a_ab[c])
    pw = each(lambda c: _mm1(a_ab[c], _head_blocks(a_ab[c].astype(BF16), cs), _NN))
    power = 2
    while power < cs:
        pw_blocks = each(lambda c: _head_blocks(pw[c].astype(BF16), cs))
        if power * 2 < cs:
            both = each(lambda c: _mm1(jnp.concatenate([inv[c], pw[c]], axis=0), pw_blocks[c], _NN))
            inv = each(lambda c: inv[c] + both[c][:cs])
            pw = each(lambda c: both[c][cs:])
        else:
            inv = each(lambda c: inv[c] + _mm1(inv[c], pw_blocks[c], _NN))
        power *= 2

    akv_mkv = each(lambda c: _mm_lhs2(jnp.concatenate([a_ak[c], m_rk[c]], axis=0),
                                      hb(v_all[rows[c]]), _NN))
    wu = each(lambda c: _mm_lhs2(
        inv[c], jnp.concatenate([hb(at[c]), hb(akv_mkv[c][:cs])], axis=1), _NN))
    g0 = each(lambda c: _head_diag(_tn_lhs2(v_all[rows[c]], kt[c].astype(BF16)), HEAD_DIM))
    for c in chunks:
        w_out[0, rows[c], :] = wu[c][:, :W_MIX]
        u0_out[0, rows[c], :] = wu[c][:, W_MIX:]
        y0_out[0, rows[c], :] = akv_mkv[c][cs:]
        rt_out[0, rows[c], :] = rt[c]
        btp_out[0, rows[c], :] = bt[c] * pc[c]
        mrb_out[0, rows[c], :] = m_rb[c]
        g0_out[0, rows[c], :] = g0[c] * pc[c]
        pc_out[0, c * SUBLANES:(c + 1) * SUBLANES, :] = jnp.broadcast_to(pc[c], (SUBLANES, W_MIX))


def _rw_intra(p_rw, bsz, seq, mu, w0, w2, a0, a2, g2, k_k, k_a, r_k, ones_bd):
    p3 = p_rw.reshape(bsz, seq, RW_COLS)
    rows = min(RW_CHUNK * RW_CPS, seq)
    cps = rows // RW_CHUNK
    spec = pl.BlockSpec((1, rows, W_MIX), lambda b, c: (b, c, 0))
    pc_spec = pl.BlockSpec((1, SUBLANES * cps, W_MIX), lambda b, c: (b, c, 0))
    vec = lambda width: pl.BlockSpec((1, width), lambda b, c: (0, 0))
    mat = lambda rws: pl.BlockSpec((rws, W_MIX), lambda b, c: (0, 0))
    big = jax.ShapeDtypeStruct((bsz, seq, W_MIX), F32)
    return pl.pallas_call(
        _rw_intra_kernel,
        grid=(bsz, seq // rows),
        in_specs=[
            pl.BlockSpec((1, rows, RW_COLS), lambda b, c: (b, c, 0)),
            pl.BlockSpec((1, SUBLANES, RW_COLS),
                         lambda b, c: (b, jnp.maximum(c * (rows // SUBLANES) - 1, 0), 0)),
            vec(RW_COLS), vec(W_MIX), mat(RW_DECAY_LORA), vec(W_MIX), mat(RW_AAA_LORA),
            mat(RW_GATE_LORA), vec(W_MIX), vec(W_MIX), vec(W_MIX), mat(W_MIX),
        ],
        out_specs=[spec] * 7 + [pc_spec, spec, spec],
        out_shape=[big] * 7 + [jax.ShapeDtypeStruct((bsz, seq // RW_CHUNK * SUBLANES, W_MIX), F32),
                               big, big],
        compiler_params=_cparams("parallel", "parallel"),
        name="rw_intra",
    )(p3, p3, mu, w0, w2, a0, a2, g2, k_k, k_a, r_k, ones_bd)


def _rw_state_kernel(w_ref, u0_ref, y0_ref, rt_ref, btp_ref, mrb_ref, g0_ref, pc_ref,
                     bonus_ref, g_ref, gnw_ref, gnb_ref, ones_ref, o_ref, state_ref):
    cs = RW_CHUNK

    @pl.when(pl.program_id(1) == 0)
    def _():
        state_ref[...] = jnp.zeros_like(state_ref)

    batch = range(w_ref.shape[0])
    each = lambda f: [f(b) for b in batch]
    hb = lambda x: _head_blocks(x.astype(BF16), HEAD_DIM)
    s0 = each(lambda b: state_ref[b])
    wr = each(lambda b: _mm_lhs2(jnp.concatenate([w_ref[b], rt_ref[b]], axis=0), hb(s0[b]), _NT))
    u = each(lambda b: wr[b][:cs] + u0_ref[b])
    y = each(lambda b: wr[b][cs:] + _mm_lhs2(mrb_ref[b], hb(u[b]), _NN) + y0_ref[b])
    su = each(lambda b: _head_diag(_tn_lhs2(u[b], btp_ref[b].astype(BF16)), HEAD_DIM))
    for b in batch:
        state_ref[b] = s0[b] * pc_ref[b, 0:1, :] + g0_ref[b] + su[b]

    inv_n = 1.0 / HEAD_DIM
    mean = each(lambda b: _mm_lhs2(y[b], ones_ref[...], _NN) * inv_n)
    yc = each(lambda b: y[b] - mean[b])
    var = each(lambda b: _mm_lhs2(yc[b] * yc[b], ones_ref[...], _NN) * inv_n)
    for b in batch:
        yn = yc[b] * lax.rsqrt(var[b] + GN_EPS) * gnw_ref[...] + gnb_ref[...]
        o_ref[b] = ((yn + bonus_ref[b]) * g_ref[b]).astype(o_ref.dtype)


def _rw_state(intra, bonus, g, gn_w, gn_b, ones_bd, bsz, seq):
    gb = RW_BATCH_GROUP if bsz % RW_BATCH_GROUP == 0 else 1
    spec = pl.BlockSpec((gb, RW_CHUNK, W_MIX), lambda b, c: (b, c, 0))
    pc_spec = pl.BlockSpec((gb, SUBLANES, W_MIX), lambda b, c: (b, c, 0))
    vec = pl.BlockSpec((1, W_MIX), lambda b, c: (0, 0))
    out = pl.pallas_call(
        _rw_state_kernel,
        grid=(bsz // gb, seq // RW_CHUNK),
        in_specs=[spec] * 7 + [pc_spec, spec, spec, vec, vec,
                               pl.BlockSpec((W_MIX, W_MIX), lambda b, c: (0, 0))],
        out_specs=spec,
        out_shape=jax.ShapeDtypeStruct((bsz, seq, W_MIX), BF16),
        scratch_shapes=[pltpu.VMEM((gb, HEAD_DIM, W_MIX), F32)],
        compiler_params=_cparams("parallel", "arbitrary"),
        name="rw_state",
    )(*intra, bonus.reshape(bsz, seq, W_MIX), g.reshape(bsz, seq, W_MIX), gn_w, gn_b,
      ones_bd.astype(BF16))
    return out.reshape(bsz * seq, W_MIX)


def _store_token_rows(ref, val):
    tm, d = val.shape
    ch = d // LANES
    for s in range(ch):
        ref[pl.ds(s, tm, stride=ch), :] = val[:, s * LANES:(s + 1) * LANES]


def _load_token_rows(ref, tm, ch):
    return jnp.concatenate([ref[pl.ds(s, tm, stride=ch), :] for s in range(ch)], axis=1)


def _outproj_router_kernel(x_ref, sb_ref, gm_ref, rw_ref, fx_ref, wo_ref, g_ref, wr_ref, br_ref,
                           su_ref, x_out, h_out, idx_out, gate_out, rank_out, cnt_out, cnt_ref):
    mix = jnp.concatenate([sb_ref[...], gm_ref[...], rw_ref[...], fx_ref[...]], axis=1)
    x = x_ref[...] + _dot(mix, wo_ref[...])
    x_out[...] = x
    h = x * lax.rsqrt(jnp.mean(x * x, axis=-1, keepdims=True) + RMS_EPS) * g_ref[...]
    _store_token_rows(h_out, h)

    nr = wr_ref.shape[0]
    h_hi, h_lo = _split2(h)
    w_hi, w_lo = _split2(wr_ref[...])
    part = _dot_nt(jnp.concatenate([w_hi, w_lo], axis=0), h_hi)
    lg = part[:nr] + part[nr:] + _dot_nt(w_hi, h_lo) + br_ref[...]
    tm = lg.shape[1]
    gl = [lg[g:g + 1, :] for g in range(N_EXPERT_GROUPS)]
    gmax = gl[0]
    gsel = jnp.zeros((1, tm), jnp.int32)
    for g in range(1, N_EXPERT_GROUPS):
        better = gl[g] > gmax
        gsel = jnp.where(better, g, gsel)
        gmax = jnp.where(better, gl[g], gmax)
    denom = gl[0] * 0.0
    for g in range(N_EXPERT_GROUPS):
        denom = denom + jnp.exp(gl[g] - gmax)
    g_gate = 1.0 / denom

    e0 = SUBLANES
    ing = lg[e0:e0 + EXPERTS_PER_GROUP, :]
    for g in range(1, N_EXPERT_GROUPS):
        ing = jnp.where(gsel == g, lg[e0 + g * EXPERTS_PER_GROUP:e0 + (g + 1) * EXPERTS_PER_GROUP, :], ing)
    ridx = _iota2(ing.shape, 0)
    m1 = jnp.max(ing, axis=0, keepdims=True)
    i1 = jnp.min(jnp.where(ing == m1, ridx, EXPERTS_PER_GROUP), axis=0, keepdims=True)
    rest = jnp.where(ridx == i1, -jnp.inf, ing)
    m2 = jnp.max(rest, axis=0, keepdims=True)
    i2 = jnp.min(jnp.where(rest == m2, ridx, EXPERTS_PER_GROUP), axis=0, keepdims=True)
    e2 = jnp.exp(m2 - m1)
    s = 1.0 + e2
    e_sel = (gsel * EXPERTS_PER_GROUP + i1, gsel * EXPERTS_PER_GROUP + i2)
    idx_out[0:1, :] = e_sel[0]
    idx_out[1:2, :] = e_sel[1]
    gate_out[0:1, :] = (1.0 / s) * g_gate
    gate_out[1:2, :] = (e2 / s) * g_gate

    @pl.when(pl.program_id(0) == 0)
    def _():
        cnt_ref[...] = jnp.zeros_like(cnt_ref)

    erow = _iota2((N_EXPERTS, tm), 0)
    onehot = [(erow == e).astype(F32) for e in e_sel]
    both = onehot[0] + onehot[1]
    before = cnt_ref[...] + _dot(both.astype(BF16), su_ref[...])
    for k in range(TOP_K):
        rank_out[k:k + 1, :] = jnp.sum(onehot[k] * before, axis=0, keepdims=True).astype(jnp.int32)
    cnt_ref[...] = cnt_ref[...] + jnp.sum(both, axis=1, keepdims=True)
    cnt_out[...] = jnp.broadcast_to(cnt_ref[...], cnt_out.shape).astype(jnp.int32)


def _outproj_router(x2, ys, w_out, g2, w_router_t, b_router):
    n, d = x2.shape
    tm = min(ROW_TILE, n)
    nr = w_router_t.shape[0]
    ymix = pl.BlockSpec((tm, W_MIX), lambda i: (i, 0))
    r = jnp.arange(tm)
    strict_upper = (r[:, None] < r[None, :]).astype(BF16)
    return pl.pallas_call(
        _outproj_router_kernel,
        grid=(n // tm,),
        in_specs=[
            pl.BlockSpec((tm, d), lambda i: (i, 0)),
            ymix, ymix, ymix, ymix,
            pl.BlockSpec((4 * W_MIX, d), lambda i: (0, 0)),
            pl.BlockSpec((1, d), lambda i: (0, 0)),
            pl.BlockSpec((nr, d), lambda i: (0, 0)),
            pl.BlockSpec((nr, 1), lambda i: (0, 0)),
            pl.BlockSpec((tm, tm), lambda i: (0, 0)),
        ],
        out_specs=[
            pl.BlockSpec((tm, d), lambda i: (i, 0)),
            pl.BlockSpec((tm * (d // LANES), LANES), lambda i: (i, 0)),
            pl.BlockSpec((TOP_K, tm), lambda i: (0, i)),
            pl.BlockSpec((TOP_K, tm), lambda i: (0, i)),
            pl.BlockSpec((TOP_K, tm), lambda i: (0, i)),
            pl.BlockSpec((N_EXPERTS, LANES), lambda i: (0, 0)),
        ],
        out_shape=[
            jax.ShapeDtypeStruct((n, d), F32),
            jax.ShapeDtypeStruct((n * (d // LANES), LANES), F32),
            jax.ShapeDtypeStruct((TOP_K, n), jnp.int32),
            jax.ShapeDtypeStruct((TOP_K, n), F32),
            jax.ShapeDtypeStruct((TOP_K, n), jnp.int32),
            jax.ShapeDtypeStruct((N_EXPERTS, LANES), jnp.int32),
        ],
        scratch_shapes=[pltpu.VMEM((N_EXPERTS, 1), F32)],
        compiler_params=_cparams("arbitrary"),
        name="outproj_router",
    )(x2, *ys, w_out, g2, w_router_t, b_router, strict_upper)


def _token_copy(src_hbm, dst, src_tok, dst_tok, sem, ch):
    return pltpu.make_async_copy(
        src_hbm.at[pl.ds(pl.multiple_of(src_tok * ch, ch), ch)],
        dst.at[pl.ds(pl.multiple_of(dst_tok * ch, ch), ch)], sem)


def _dispatch_kernel(pad_end_ref, slot_ref, h_ref, o_hbm, zbuf, zsem, sem, *, ch, tm):
    i = pl.program_id(0)

    def zero_copy(first_slot):
        return pltpu.make_async_copy(
            zbuf, o_hbm.at[pl.ds(pl.multiple_of(first_slot * ch, ch), MOE_TB * ch)], zsem)

    @pl.when(i == 0)
    def _():
        zbuf[...] = jnp.zeros_like(zbuf)
        tails = [jnp.maximum(pad_end_ref[e] - MOE_TB, 0) for e in range(N_EXPERTS)]
        for first in tails:
            zero_copy(first).start()
        for first in tails:
            zero_copy(first).wait()

        def unused_block(b, _):
            zero_copy(b * MOE_TB).start()
            zero_copy(b * MOE_TB).wait()
            return 0

        lax.fori_loop(pad_end_ref[N_EXPERTS - 1] // MOE_TB, o_hbm.shape[0] // (MOE_TB * ch),
                      unused_block, 0)

    def start(r, _):
        for k in range(TOP_K):
            _token_copy(h_ref, o_hbm, r, slot_ref[0, 0, r * TOP_K + k], sem, ch).start(
                priority=k % 2)
        return 0

    lax.fori_loop(0, tm, start, 0, unroll=DMA_ISSUE_UNROLL)
    for k in range(TOP_K):
        pltpu.make_async_copy(h_ref, o_hbm.at[pl.ds(0, tm * ch)], sem).wait()


def _dispatch(pad_end, slot_flat, h_rows, n_slots, ch):
    n_tok = h_rows.shape[0] // ch
    tm = min(DISPATCH_TOKENS, n_tok)
    n_steps = n_tok // tm
    return pl.pallas_call(
        functools.partial(_dispatch_kernel, ch=ch, tm=tm),
        grid_spec=pltpu.PrefetchScalarGridSpec(
            num_scalar_prefetch=1,
            grid=(n_steps,),
            in_specs=[
                pl.BlockSpec((1, 1, tm * TOP_K), lambda i, pe: (i, 0, 0), memory_space=pltpu.SMEM),
                pl.BlockSpec((tm * ch, LANES), lambda i, pe: (i, 0)),
            ],
            out_specs=pl.BlockSpec(memory_space=pl.ANY),
            scratch_shapes=[pltpu.VMEM((MOE_TB * ch, LANES), F32), pltpu.SemaphoreType.DMA,
                            pltpu.SemaphoreType.DMA],
        ),
        out_shape=jax.ShapeDtypeStruct((n_slots * ch, LANES), h_rows.dtype),
        compiler_params=_cparams("arbitrary"),
        name="moe_dispatch",
    )(pad_end, slot_flat.reshape(n_steps, 1, tm * TOP_K), h_rows)


def _expert_kernel(be_ref, nb_ref, x_ref, wg_ref, wu_ref, wd_ref, o_ref, wg_c, wu_c, wd_c, *, ch):
    i = pl.program_id(0)

    @pl.when((i < nb_ref[0]) & ((i == 0) | (be_ref[i] != be_ref[jnp.maximum(i - 1, 0)])))
    def _():
        wg_c[...] = wg_ref[0, 0].astype(BF16)
        wu_c[...] = wu_ref[0, 0].astype(BF16)
        wd_c[...] = wd_ref[0, 0].astype(BF16)

    @pl.when(i < nb_ref[0])
    def _():
        x = _load_token_rows(x_ref, MOE_TB, ch).astype(BF16)
        a = _dot(x, wg_c[...])
        u = _dot(x, wu_c[...])
        hid = (a * jax.nn.sigmoid(a) * u).astype(BF16)
        _store_token_rows(o_ref, _dot(hid, wd_c[...]))

    @pl.when(i >= nb_ref[0])
    def _():
        o_ref[...] = jnp.zeros_like(o_ref)


def _expert_blocks(block_expert, n_used, xb_rows, w_gate, w_up, w_down, layer):
    d, hid = w_gate.shape[2], w_gate.shape[3]
    ch = d // LANES
    n_slots = xb_rows.shape[0] // ch
    return pl.pallas_call(
        functools.partial(_expert_kernel, ch=ch),
        grid_spec=pltpu.PrefetchScalarGridSpec(
            num_scalar_prefetch=2,
            grid=(n_slots // MOE_TB,),
            in_specs=[
                pl.BlockSpec((MOE_TB * ch, LANES), lambda i, be, nb: (i, 0)),
                pl.BlockSpec((1, 1, d, hid), lambda i, be, nb: (layer, be[i], 0, 0)),
                pl.BlockSpec((1, 1, d, hid), lambda i, be, nb: (layer, be[i], 0, 0)),
                pl.BlockSpec((1, 1, hid, d), lambda i, be, nb: (layer, be[i], 0, 0)),
            ],
            out_specs=pl.BlockSpec((MOE_TB * ch, LANES), lambda i, be, nb: (i, 0)),
            scratch_shapes=[pltpu.VMEM((d, hid), BF16), pltpu.VMEM((d, hid), BF16),
                            pltpu.VMEM((hid, d), BF16)],
        ),
        out_shape=jax.ShapeDtypeStruct((n_slots * ch, LANES), F32),
        compiler_params=_cparams("arbitrary"),
        name="moe_experts",
    )(block_expert, n_used, xb_rows, w_gate, w_up, w_down)


def _combine_kernel(slot_ref, slot_next_ref, x_ref, gate_ref, norm_ref, yb_hbm, o_ref, buf, sem,
                    *, ch, final_norm):
    i = pl.program_id(0)
    last = pl.num_programs(0) - 1
    tm = x_ref.shape[0]

    def issue(idx_ref, parity):
        def start(r, _):
            for k in range(TOP_K):
                _token_copy(yb_hbm, buf.at[parity, k], idx_ref[0, 0, r * TOP_K + k], r,
                            sem.at[parity], ch).start(priority=k % 2)
            return 0

        lax.fori_loop(0, tm, start, 0, unroll=DMA_ISSUE_UNROLL)

    @pl.when(i == 0)
    def _():
        issue(slot_ref, 0)

    @pl.when(i < last)
    def _():
        issue(slot_next_ref, (i + 1) % 2)

    par = i % 2
    for k in range(TOP_K):
        pltpu.make_async_copy(yb_hbm.at[pl.ds(0, tm * ch)], buf.at[par, k], sem.at[par]).wait()
    y = (_load_token_rows(buf.at[par, 0], tm, ch) * gate_ref[:, 0:1]
         + _load_token_rows(buf.at[par, 1], tm, ch) * gate_ref[:, 1:2])
    x = x_ref[...] + y
    if final_norm:
        x = x * lax.rsqrt(jnp.mean(x * x, axis=-1, keepdims=True) + RMS_EPS) * norm_ref[...]
    o_ref[...] = x


def _combine(slot_flat, x2, gates, yb_rows, norm_g, final_norm):
    n, d = x2.shape
    ch = d // LANES
    tm = min(COMBINE_TOKENS, n)
    n_steps = n // tm
    slots3 = slot_flat.reshape(n_steps, 1, tm * TOP_K)
    return pl.pallas_call(
        functools.partial(_combine_kernel, ch=ch, final_norm=final_norm),
        grid=(n_steps,),
        in_specs=[
            pl.BlockSpec((1, 1, tm * TOP_K), lambda i: (i, 0, 0), memory_space=pltpu.SMEM),
            pl.BlockSpec((1, 1, tm * TOP_K), lambda i: (jnp.minimum(i + 1, n_steps - 1), 0, 0),
                         memory_space=pltpu.SMEM),
            pl.BlockSpec((tm, d), lambda i: (i, 0)),
            pl.BlockSpec((tm, TOP_K), lambda i: (i, 0)),
            pl.BlockSpec((1, d), lambda i: (0, 0)),
            pl.BlockSpec(memory_space=pl.ANY),
        ],
        out_specs=pl.BlockSpec((tm, d), lambda i: (i, 0)),
        scratch_shapes=[pltpu.VMEM((2, TOP_K, tm * ch, LANES), F32),
                        pltpu.SemaphoreType.DMA((2,))],
        out_shape=jax.ShapeDtypeStruct((n, d), F32),
        compiler_params=_cparams("arbitrary"),
        name="moe_combine",
    )(slots3, slots3, x2, gates, norm_g, yb_rows)


def _routing_plan(expert_idx_t, rank_t, counts, n_tok):
    padded = (counts + MOE_TB - 1) // MOE_TB * MOE_TB
    pad_end = jnp.cumsum(padded)
    pad_start = pad_end - padded
    start_t = jnp.sum(jnp.where(expert_idx_t[:, :, None] == jnp.arange(N_EXPERTS), pad_start, 0),
                      axis=-1)
    slot = jnp.transpose(start_t + rank_t).reshape(n_tok * TOP_K)
    n_blocks = -(-(n_tok * TOP_K) // MOE_TB) + N_EXPERTS
    block_expert = jnp.minimum(
        jnp.sum(pad_end[None, :] <= (jnp.arange(n_blocks) * MOE_TB)[:, None], axis=1),
        N_EXPERTS - 1).astype(jnp.int32)
    n_used = (pad_end[-1] // MOE_TB).astype(jnp.int32).reshape(1)
    return slot.astype(jnp.int32), pad_end.astype(jnp.int32), block_expert, n_used, n_blocks * MOE_TB


def _moe(x2, h2, expert_idx_t, gates_t, rank_t, counts, w_gate, w_up, w_down, layer, norm_g,
         final_norm):
    n_tok = x2.shape[0]
    slot, pad_end, block_expert, n_used, n_slots = _routing_plan(expert_idx_t, rank_t, counts, n_tok)
    xb = _dispatch(pad_end, slot, h2, n_slots, x2.shape[1] // LANES)
    yb = _expert_blocks(block_expert, n_used, xb, w_gate, w_up, w_down, layer)
    return _combine(slot, x2, jnp.transpose(gates_t), yb, norm_g, final_norm)


def _prepare_params(w_in, gm_w_s, gm_b, rw_mu, rw_w0, rw_w2, rw_a0, rw_a2, rw_g2, rw_k_k, rw_k_a,
                    rw_r_k, rw_gn_w, rw_gn_b, fx_b_f, w_out, router_group_w, router_group_b,
                    router_expert_w, router_expert_b):
    depth, d = w_in.shape[0], w_in.shape[1]
    n_main = 3 * W_MIX + 2 * W_MIX + RW_COLS + 3 * W_MIX
    t = lambda a: jnp.swapaxes(a, 1, 2)
    pad_rows = lambda a, rows: jnp.concatenate(
        [a, jnp.zeros((depth, rows - a.shape[1]) + a.shape[2:], a.dtype)], axis=1)
    row = lambda a: a.reshape(depth, 1, -1)
    wr = jnp.concatenate([pad_rows(t(router_group_w), SUBLANES), t(router_expert_w)], axis=1)
    br = jnp.concatenate([pad_rows(router_group_b[:, :, None], SUBLANES),
                          router_expert_b[:, :, None]], axis=1)
    return dict(
        w_main=w_in[:, :, :n_main].astype(BF16),
        w_gate_t=pad_rows(t(w_in[:, :, n_main:n_main + N_HEADS]), SUBLANES).astype(BF16),
        fx_bias=pad_rows(fx_b_f[:, :, None], SUBLANES),
        gm_w=jnp.transpose(gm_w_s, (0, 2, 1, 3)).reshape(depth, GM_CHUNK, N_HEADS * GM_CHUNK),
        gm_b=jnp.repeat(t(gm_b), HEAD_DIM, axis=2),
        mu=row(rw_mu), w0=row(rw_w0), w2=rw_w2.astype(BF16), a0=row(rw_a0),
        a2=rw_a2.astype(BF16), g2=rw_g2.astype(BF16), k_k=row(rw_k_k), k_a=row(rw_k_a),
        r_k=row(rw_r_k), gn_w=row(rw_gn_w), gn_b=row(rw_gn_b),
        w_out=w_out.astype(BF16), wr=wr, br=br,
    )


def kernel(x, norm1_g, w_in, gm_w_s, gm_b, rw_mu, rw_w0, rw_w2, rw_a0, rw_a2, rw_g2, rw_k_k, rw_k_a, rw_r_k, rw_gn_w, rw_gn_b, fx_b_f, w_out, norm2_g, router_group_w, router_group_b, router_expert_w, router_expert_b, exp_w_gate, exp_w_up, exp_w_down, final_norm_g):
    bsz, seq, d = x.shape
    depth = w_in.shape[0]
    assert seq % min(ATT_TQB, seq) == 0 and seq % min(FX_TQB, seq) == 0
    assert seq % RW_CHUNK == 0 and seq % (2 * LANES) == 0
    x2 = x.reshape(bsz * seq, d)
    hd = jnp.arange(W_MIX) // HEAD_DIM
    ones_bd = (hd[:, None] == hd[None, :]).astype(BF16)
    allp = _prepare_params(w_in, gm_w_s, gm_b, rw_mu, rw_w0, rw_w2, rw_a0, rw_a2, rw_g2, rw_k_k,
                           rw_k_a, rw_r_k, rw_gn_w, rw_gn_b, fx_b_f, w_out, router_group_w,
                           router_group_b, router_expert_w, router_expert_b)
    for l in range(depth):
        lp = {k: v[l] for k, v in allp.items()}
        p_sb, p_gm, p_rw, p_fx, grow = _norm_inproj(x2, norm1_g[l].reshape(1, d), lp["w_main"],
                                                    lp["w_gate_t"])
        y_sb = _sb_attention(p_sb, bsz, seq)
        y_gm = _gm_mix(p_gm, lp["gm_w"], lp["gm_b"], ones_bd)
        intra = _rw_intra(p_rw, bsz, seq, lp["mu"], lp["w0"], lp["w2"], lp["a0"], lp["a2"],
                          lp["g2"], lp["k_k"], lp["k_a"], lp["r_k"], ones_bd)
        y_rw = _rw_state(intra[:8], intra[8], intra[9], lp["gn_w"], lp["gn_b"], ones_bd, bsz, seq)
        grow3 = jnp.transpose(grow.reshape(SUBLANES, bsz, seq), (1, 0, 2))
        cum_row = _fx_cum(grow3, lp["fx_bias"])
        y_fx = _fx_attention(p_fx, cum_row, bsz, seq)
        x2, h2, eidx_t, gates_t, rank_t, cnt = _outproj_router(
            x2, (y_sb, y_gm, y_rw, y_fx), lp["w_out"], norm2_g[l].reshape(1, d), lp["wr"], lp["br"])
        x2 = _moe(x2, h2, eidx_t, gates_t, rank_t, cnt[:, 0], exp_w_gate, exp_w_up, exp_w_down, l,
                  final_norm_g.reshape(1, d), l == depth - 1)
    return x2.reshape(bsz, seq, d)
```
